```python
import jax, jax.numpy as jnp
from jax import lax
import numpy as np

D_MODEL = 2048
BATCH = 1
SEQ = 16384
DEPTH = 1
DEC_BATCH = 8
DEC_SEQ = 64
PAST_LEN = 1024

CHUNK = 64
HEAD_DIM = 128
N_MEM = 4
N_MEM_TOK = 256
N_FOX = (D_MODEL // HEAD_DIM - N_MEM) // 2
N_RET = (D_MODEL // HEAD_DIM - N_MEM) // 2
D_FOX = N_FOX * HEAD_DIM
D_RET = N_RET * HEAD_DIM
D_MEMQ = N_MEM * HEAD_DIM
D_MIX = D_FOX + D_RET + D_MEMQ
D_FF = 256 * (-(-8 * D_MODEL // (3 * 256)))
CONV_W = 3
Q_BLOCK = 128
ROPE_BASE = 10000.0
LN_EPS = 1e-5
GN_EPS = 1e-5
ALPHA = (2.0 * DEPTH) ** 0.25
BETA = (8.0 * DEPTH) ** -0.25
IN_SPLITS = (D_FOX, D_FOX, D_FOX, N_FOX, D_RET, D_RET, D_RET, D_RET, D_MEMQ)
D_IN = sum(IN_SPLITS)

kernel_name = 'hybrid_fox_retention_mem_convffn_step'

F32 = jnp.float32


def _layernorm(x, g, b):
    xf = x.astype(F32)
    mu = jnp.mean(xf, axis=-1, keepdims=True)
    var = jnp.mean(jnp.square(xf - mu), axis=-1, keepdims=True)
    return ((xf - mu) * lax.rsqrt(var + LN_EPS) * g.astype(F32) + b.astype(F32)).astype(x.dtype)


def _rope(x, pos):
    half = HEAD_DIM // 2
    inv = ROPE_BASE ** (-jnp.arange(half, dtype=F32) / half)
    ang = pos.astype(F32)[:, None] * inv[None, :]
    cos = jnp.cos(ang)[None, :, None, :]
    sin = jnp.sin(ang)[None, :, None, :]
    xf = x.astype(F32)
    x1, x2 = xf[..., :half], xf[..., half:]
    return jnp.concatenate([x1 * cos - x2 * sin, x2 * cos + x1 * sin], axis=-1).astype(x.dtype)


def _in_proj(x, w_in, b_f, pos):
    B, L, _ = x.shape
    z = x @ w_in
    idx = tuple(int(i) for i in np.cumsum(IN_SPLITS)[:-1])
    fq, fk, fv, ff, rq, rk, rv, rg, mq = jnp.split(z, idx, axis=-1)
    hd = lambda t, h: t.reshape(B, L, h, HEAD_DIM)
    logf = jax.nn.log_sigmoid((ff + b_f).astype(F32))
    rq = _rope(hd(rq, N_RET), pos)
    rk = _rope(hd(rk, N_RET), pos) * (HEAD_DIM ** -0.5)
    return hd(fq, N_FOX), hd(fk, N_FOX), hd(fv, N_FOX), logf, rq, rk, hd(rv, N_RET), rg, hd(mq, N_MEM)


def _fox_attend(q, k, v, cq, ck, qpos, kpos):
    s = jnp.einsum('bqhd,bkhd->bhqk', q, k).astype(F32) * (HEAD_DIM ** -0.5)
    s = s + jnp.swapaxes(cq, 1, 2)[..., :, None] - jnp.swapaxes(ck, 1, 2)[..., None, :]
    s = jnp.where(kpos[None, :] <= qpos[:, None], s, -jnp.inf)
    p = jax.nn.softmax(s, axis=-1).astype(v.dtype)
    return jnp.einsum('bhqk,bkhd->bqhd', p, v)


def _fox_prompt(q, k, v, logf):
    B, S, H, D = q.shape
    c = jnp.cumsum(logf, axis=1)
    nb = S // Q_BLOCK
    qb = jnp.moveaxis(q.reshape(B, nb, Q_BLOCK, H, D), 1, 0)
    cb = jnp.moveaxis(c.reshape(B, nb, Q_BLOCK, H), 1, 0)
    kpos = jnp.arange(S)

    def blk(args):
        qi, ci, i = args
        qpos = i * Q_BLOCK + jnp.arange(Q_BLOCK)
        return _fox_attend(qi, k, v, ci, c, qpos, kpos)

    o = lax.map(blk, (qb, cb, jnp.arange(nb)))
    return jnp.moveaxis(o, 0, 1).reshape(B, S, H, D)


def _fox_sample(q, k_all, v_all, logf_all):
    L = q.shape[1]
    P = k_all.shape[1] - L
    c = jnp.cumsum(logf_all, axis=1)
    return _fox_attend(q, k_all, v_all, c[:, P:], c, P + jnp.arange(L), jnp.arange(P + L))


def _log_gamma():
    return jnp.log1p(-jnp.exp2(-5.0 - jnp.arange(N_RET, dtype=F32)))


def _retention_chunk(S, q, k, v):
    L = q.shape[1]
    lg = _log_gamma()
    i = jnp.arange(L, dtype=F32)
    d_intra = jnp.exp(lg[:, None, None] * jnp.abs(i[:, None] - i[None, :]))
    d_cross = jnp.exp(lg[None, :] * (i[:, None] + 1.0))
    d_state = jnp.exp(lg[None, :] * (L - 1.0 - i[:, None]))
    qf, kf, vf = q.astype(F32), k.astype(F32), v.astype(F32)
    a = jnp.einsum('blhd,bmhd->bhlm', qf, kf) * d_intra[None]
    o = jnp.einsum('bhlm,bmhe->blhe', a, vf) + jnp.einsum('blhd,bhde->blhe', qf, S) * d_cross[None, :, :, None]
    S_new = jnp.exp(lg * L)[None, :, None, None] * S + jnp.einsum('blhd,blhe->bhde', kf * d_state[None, :, :, None], vf)
    return S_new, o


def _retention_prompt(q, k, v):
    B, S, H, D = q.shape
    nc = S // CHUNK
    to_c = lambda t: jnp.moveaxis(t.reshape(B, nc, CHUNK, H, D), 1, 0)
    S0 = jnp.zeros((B, H, D, D), F32)
    S_fin, o = lax.scan(lambda s, xs: _retention_chunk(s, *xs), S0, (to_c(q), to_c(k), to_c(v)))
    return jnp.moveaxis(o, 0, 1).reshape(B, S, H, D), S_fin


def _mem_attend(q, mk, mv):
    s = jnp.einsum('bqhd,bmhd->bhqm', q, mk).astype(F32) * (HEAD_DIM ** -0.5)
    p = jax.nn.softmax(s, axis=-1).astype(mv.dtype)
    return jnp.einsum('bhqm,bmhd->bqhd', p, mv)


def _mem_kv(mem, w_mem_kv):
    B, M, _ = mem.shape
    mk, mv = jnp.split(mem @ w_mem_kv, 2, axis=-1)
    return mk.reshape(B, M, N_MEM, HEAD_DIM), mv.reshape(B, M, N_MEM, HEAD_DIM)


def _mix_out(fo, ro, rg, mo, gn_g, w_o):
    B, L = fo.shape[:2]
    mu = jnp.mean(ro, axis=-1, keepdims=True)
    var = jnp.mean(jnp.square(ro - mu), axis=-1, keepdims=True)
    rn = (ro - mu) * lax.rsqrt(var + GN_EPS) * gn_g.astype(F32).reshape(N_RET, HEAD_DIM)
    rout = (jax.nn.silu(rg.astype(F32)) * rn.reshape(B, L, D_RET)).astype(fo.dtype)
    cat = jnp.concatenate([fo.reshape(B, L, D_FOX), rout, mo.reshape(B, L, D_MEMQ)], axis=-1)
    return cat @ w_o


def _conv_ffn(x, conv_past, w_gate, w_up, conv_w, conv_b, w_down):
    g = x @ w_gate
    L = g.shape[1]
    full = jnp.concatenate([conv_past.astype(g.dtype), g], axis=1)
    gc = conv_b
    for j in range(CONV_W):
        gc = gc + conv_w[j] * full[:, j:j + L]
    h = jax.nn.gelu(gc) * (x @ w_up)
    return h @ w_down, full[:, -(CONV_W - 1):]


def setup_inputs(seed: int = 0) -> dict:
    key = jax.random.key(seed)
    ks = jax.random.split(key, 24)
    nrm = lambda k, shape, s: jax.random.normal(k, shape, F32) * s
    return {
        'x_prompt': nrm(ks[0], (BATCH, SEQ, D_MODEL), 1.0),
        'x_sample': nrm(ks[1], (DEC_BATCH, DEC_SEQ, D_MODEL), 1.0),
        'cache_fox_k': nrm(ks[2], (DEPTH, DEC_BATCH, PAST_LEN, N_FOX, HEAD_DIM), 1.0),
        'cache_fox_v': nrm(ks[3], (DEPTH, DEC_BATCH, PAST_LEN, N_FOX, HEAD_DIM), 1.0),
        'cache_fox_logf': jax.nn.log_sigmoid(2.0 + nrm(ks[4], (DEPTH, DEC_BATCH, PAST_LEN, N_FOX), 0.5)),
        'state_ret': nrm(ks[5], (DEPTH, DEC_BATCH, N_RET, HEAD_DIM, HEAD_DIM), 1.0),
        'cache_mem_k': nrm(ks[6], (DEPTH, DEC_BATCH, N_MEM_TOK, N_MEM, HEAD_DIM), 1.0),
        'cache_mem_v': nrm(ks[7], (DEPTH, DEC_BATCH, N_MEM_TOK, N_MEM, HEAD_DIM), 1.0),
        'state_conv': nrm(ks[8], (DEPTH, DEC_BATCH, CONV_W - 1, D_FF), 1.0),
        'mem_prompt': nrm(ks[9], (BATCH, N_MEM_TOK, D_MODEL), 1.0),
        'w_in': nrm(ks[10], (DEPTH, D_MODEL, D_IN), D_MODEL ** -0.5),
        'b_f': 2.0 + nrm(ks[11], (DEPTH, N_FOX), 0.5),
        'w_mem_kv': nrm(ks[12], (DEPTH, D_MODEL, 2 * D_MEMQ), D_MODEL ** -0.5),
        'ret_gn_g': 1.0 + nrm(ks[13], (DEPTH, D_RET), 0.02),
        'w_o': nrm(ks[14], (DEPTH, D_MIX, D_MODEL), BETA * D_MIX ** -0.5),
        'ln1_g': 1.0 + nrm(ks[15], (DEPTH, D_MODEL), 0.02),
        'ln1_b': nrm(ks[16], (DEPTH, D_MODEL), 0.02),
        'w_gate': nrm(ks[17], (DEPTH, D_MODEL, D_FF), D_MODEL ** -0.5),
        'w_up': nrm(ks[18], (DEPTH, D_MODEL, D_FF), D_MODEL ** -0.5),
        'conv_w': nrm(ks[19], (DEPTH, CONV_W, D_FF), CONV_W ** -0.5),
        'conv_b': nrm(ks[20], (DEPTH, D_FF), 0.02),
        'w_down': nrm(ks[21], (DEPTH, D_FF, D_MODEL), BETA * D_FF ** -0.5),
        'ln2_g': 1.0 + nrm(ks[22], (DEPTH, D_MODEL), 0.02),
        'ln2_b': nrm(ks[23], (DEPTH, D_MODEL), 0.02),
    }


def reference(x_prompt, x_sample, cache_fox_k, cache_fox_v, cache_fox_logf, state_ret,
              cache_mem_k, cache_mem_v, state_conv, mem_prompt, w_in, b_f, w_mem_kv, ret_gn_g,
              w_o, ln1_g, ln1_b, w_gate, w_up, conv_w, conv_b, w_down, ln2_g, ln2_b):
    B, S, _ = x_prompt.shape
    L = x_sample.shape[1]
    P = cache_fox_k.shape[2]
    pos_p = jnp.arange(S)
    pos_s = P + jnp.arange(L)
    hp, hs = x_prompt, x_sample
    names = ('fk_p', 'fv_p', 'lf_p', 'S_p', 'mk_p', 'mv_p', 'cv_p', 'fk_s', 'fv_s', 'lf_s', 'S_s', 'cv_s')
    outs = {n: [] for n in names}
    for l in range(DEPTH):
        fq, fk, fv, lf, rq, rk, rv, rg, mq = _in_proj(hp, w_in[l], b_f[l], pos_p)
        mk, mv = _mem_kv(mem_prompt, w_mem_kv[l])
        fo = _fox_prompt(fq, fk, fv, lf)
        ro, S_p = _retention_prompt(rq, rk, rv)
        mo = _mem_attend(mq, mk, mv)
        h1 = _layernorm(ALPHA * hp + _mix_out(fo, ro, rg, mo, ret_gn_g[l], w_o[l]), ln1_g[l], ln1_b[l])
        f, cv_p = _conv_ffn(h1, jnp.zeros((B, CONV_W - 1, D_FF), h1.dtype),
                            w_gate[l], w_up[l], conv_w[l], conv_b[l], w_down[l])
        hp = _layernorm(ALPHA * h1 + f, ln2_g[l], ln2_b[l])
        outs['fk_p'].append(fk); outs['fv_p'].append(fv); outs['lf_p'].append(lf)
        outs['S_p'].append(S_p.astype(x_prompt.dtype)); outs['mk_p'].append(mk); outs['mv_p'].append(mv)
        outs['cv_p'].append(cv_p)
        fq, fk, fv, lf, rq, rk, rv, rg, mq = _in_proj(hs, w_in[l], b_f[l], pos_s)
        k_all = jnp.concatenate([cache_fox_k[l].astype(fk.dtype), fk], axis=1)
        v_all = jnp.concatenate([cache_fox_v[l].astype(fv.dtype), fv], axis=1)
        lf_all = jnp.concatenate([cache_fox_logf[l].astype(F32), lf], axis=1)
        fo = _fox_sample(fq, k_all, v_all, lf_all)
        S_s, ro = _retention_chunk(state_ret[l].astype(F32), rq, rk, rv)
        mo = _mem_attend(mq, cache_mem_k[l].astype(mq.dtype), cache_mem_v[l].astype(mq.dtype))
        h1 = _layernorm(ALPHA * hs + _mix_out(fo, ro, rg, mo, ret_gn_g[l], w_o[l]), ln1_g[l], ln1_b[l])
        f, cv_s = _conv_ffn(h1, state_conv[l], w_gate[l], w_up[l], conv_w[l], conv_b[l], w_down[l])
        hs = _layernorm(ALPHA * h1 + f, ln2_g[l], ln2_b[l])
        outs['fk_s'].append(fk); outs['fv_s'].append(fv); outs['lf_s'].append(lf)
        outs['S_s'].append(S_s.astype(state_ret.dtype)); outs['cv_s'].append(cv_s)
    return (hp, hs,
            jnp.stack(outs['fk_p']), jnp.stack(outs['fv_p']), jnp.stack(outs['lf_p']),
            jnp.stack(outs['S_p']), jnp.stack(outs['mk_p']), jnp.stack(outs['mv_p']),
            jnp.stack(outs['cv_p']),
            jnp.stack(outs['fk_s']), jnp.stack(outs['fv_s']), jnp.stack(outs['lf_s']),
            jnp.stack(outs['S_s']), jnp.stack(outs['cv_s']))
```

```python
import functools
import math

import numpy as np
import jax
import jax.numpy as jnp
from jax import lax
from jax.experimental import pallas as pl
from jax.experimental.pallas import tpu as pltpu

F32 = jnp.float32
BF16 = jnp.bfloat16

HEAD_DIM = 128
CHUNK = 64
CONV_W = 3
ROPE_BASE = 10000.0
LN_EPS = 1e-5
GN_EPS = 1e-5
LANES = 128
V7X_VMEM_BYTES = 64 * 1024 * 1024
VMEM_LIMIT = V7X_VMEM_BYTES - 8 * 1024 * 1024
SCALE = HEAD_DIM ** -0.5
NEG_INF = float("-inf")
HEAD_ROWS = 16


def _cparams(sem):
    return pltpu.CompilerParams(dimension_semantics=sem, vmem_limit_bytes=VMEM_LIMIT)


def _resident(shape):
    nd = len(shape)
    return pl.BlockSpec(shape, lambda *_: (0,) * nd, pipeline_mode=pl.Buffered(1))


def _dot(a, b):
    return jnp.dot(a, b, preferred_element_type=F32)


def _dot_nt(a, b):
    return lax.dot_general(a, b, (((1,), (1,)), ((), ())), preferred_element_type=F32)


def _dot_tn(a, b):
    return lax.dot_general(a, b, (((0,), (0,)), ((), ())), preferred_element_type=F32)


def _split3_dot(a, tri, *, tri_first):
    hi = a.astype(BF16)
    r1 = a - hi.astype(F32)
    mid = r1.astype(BF16)
    lo = (r1 - mid.astype(F32)).astype(BF16)
    if tri_first:
        return _dot(tri, hi) + _dot(tri, mid) + _dot(tri, lo)
    return _dot(hi, tri) + _dot(mid, tri) + _dot(lo, tri)


def _layernorm(z, g, b):
    mu = jnp.mean(z, axis=-1, keepdims=True)
    zc = z - mu
    var = jnp.mean(zc * zc, axis=-1, keepdims=True)
    return zc * lax.rsqrt(var + LN_EPS) * g + b


def _memkv_kernel(m_ref, w_ref, o_ref):
    o_ref[...] = _dot(m_ref[...].astype(BF16), w_ref[...])


def _mem_kv(mem, w_bf):
    m, _ = mem.shape
    return pl.pallas_call(
        _memkv_kernel,
        out_shape=jax.ShapeDtypeStruct((m, w_bf.shape[1]), F32),
        compiler_params=pltpu.CompilerParams(vmem_limit_bytes=VMEM_LIMIT),
        name="mem_kv",
    )(mem, w_bf)


def _in_proj_kernel(x_ref, wa_ref, wf_ref, bf_ref, cos_ref, sin_ref, mk_ref, mv_ref, tri_ref,
                    fq_ref, fk_ref, fv_ref, lf_ref, c_ref, rq_ref, rk_ref, rv_ref, rg_ref, mo_ref,
                    carry_ref, *, d_fox, d_ret, d_memq, tiles_per_seq):
    i = pl.program_id(0)
    xb = x_ref[...].astype(BF16)

    def proj(lo, width):
        return _dot(xb, wa_ref[:, lo:lo + width])

    o_fk = d_fox
    o_fv = 2 * d_fox
    o_rq = 3 * d_fox
    o_rk = o_rq + d_ret
    o_rv = o_rk + d_ret
    o_rg = o_rv + d_ret
    o_mq = o_rg + d_ret

    fq_ref[...] = (proj(0, d_fox) * SCALE).astype(BF16)
    fk_ref[...] = proj(o_fk, d_fox)
    fv_ref[...] = proj(o_fv, d_fox)

    z = _dot(xb, wf_ref[...]) + bf_ref[...]
    lf = jnp.minimum(z, 0.0) - jnp.log1p(jnp.exp(-jnp.abs(z)))
    lf_ref[...] = lf

    @pl.when(i % tiles_per_seq == 0)
    def _():
        carry_ref[...] = jnp.zeros_like(carry_ref)

    c = _split3_dot(lf, tri_ref[...], tri_first=True) + carry_ref[0:1, :]
    c_ref[...] = c
    tm = c.shape[0]
    carry_ref[...] = jnp.broadcast_to(c[tm - 1:tm, :], carry_ref.shape)

    cos = cos_ref[...]
    sin = sin_ref[...]

    def rope(t):
        return t * cos + pltpu.roll(t, HEAD_DIM // 2, axis=1) * sin

    zq = proj(o_rq, d_ret)
    zk = proj(o_rk, d_ret)
    for h in range(d_ret // HEAD_DIM):
        sl = slice(h * HEAD_DIM, (h + 1) * HEAD_DIM)
        rq_ref[:, sl] = rope(zq[:, sl]).astype(BF16)
        rk_ref[:, sl] = (rope(zk[:, sl]) * SCALE).astype(BF16)
    rv_ref[...] = proj(o_rv, d_ret).astype(BF16)
    rg_ref[...] = proj(o_rg, d_ret).astype(BF16)

    zm = proj(o_mq, d_memq)
    for h in range(d_memq // HEAD_DIM):
        sl = slice(h * HEAD_DIM, (h + 1) * HEAD_DIM)
        q = (zm[:, sl] * SCALE).astype(BF16)
        s = _dot_nt(q, mk_ref[0, :, sl].astype(BF16))
        p = jnp.exp(s - jnp.max(s, axis=-1, keepdims=True))
        l = jnp.sum(p, axis=-1, keepdims=True)
        o = _dot(p.astype(BF16), mv_ref[0, :, sl].astype(BF16))
        mo_ref[:, sl] = (o / l).astype(BF16)


def _in_proj(x, wa, wf, bfp, cos, sin, mk, mv, *, tm, seq_len, d_fox, d_ret, d_memq):
    m, d_model = x.shape
    assert m % tm == 0 and seq_len % tm == 0
    tiles_per_seq = seq_len // tm
    n_mem_tok = mk.shape[1]
    tri = (np.arange(tm)[:, None] >= np.arange(tm)[None, :])
    tri = jnp.asarray(tri, dtype=BF16)
    row = lambda w: pl.BlockSpec((tm, w), lambda i: (i, 0))
    memspec = pl.BlockSpec((1, n_mem_tok, d_memq), lambda i: (i // tiles_per_seq, 0, 0))
    kern = functools.partial(_in_proj_kernel, d_fox=d_fox, d_ret=d_ret, d_memq=d_memq,
                             tiles_per_seq=tiles_per_seq)
    out_shape = (
        jax.ShapeDtypeStruct((m, d_fox), BF16),
        jax.ShapeDtypeStruct((m, d_fox), F32),
        jax.ShapeDtypeStruct((m, d_fox), F32),
        jax.ShapeDtypeStruct((m, LANES), F32),
        jax.ShapeDtypeStruct((m, LANES), F32),
        jax.ShapeDtypeStruct((m, d_ret), BF16),
        jax.ShapeDtypeStruct((m, d_ret), BF16),
        jax.ShapeDtypeStruct((m, d_ret), BF16),
        jax.ShapeDtypeStruct((m, d_ret), BF16),
        jax.ShapeDtypeStruct((m, d_memq), BF16),
    )
    return pl.pallas_call(
        kern,
        grid=(m // tm,),
        in_specs=[row(d_model), _resident(wa.shape), _resident(wf.shape), _resident(bfp.shape),
                  row(HEAD_DIM), row(HEAD_DIM), memspec, memspec, _resident(tri.shape)],
        out_specs=(row(d_fox), row(d_fox), row(d_fox), row(LANES), row(LANES),
                   row(d_ret), row(d_ret), row(d_ret), row(d_ret), row(d_memq)),
        out_shape=out_shape,
        scratch_shapes=[pltpu.VMEM((8, LANES), F32)],
        compiler_params=_cparams(("arbitrary",)),
        name="in_proj",
    )(x, wa, wf, bfp, cos, sin, mk, mv, tri)


def _fox_prompt_kernel(q_ref, k_ref, v_ref, ccol_ref, crow_ref, o_ref, kbuf, vbuf, *, tq):
    h = pl.program_id(0)
    qi = pl.program_id(1)
    row0 = pl.multiple_of(qi * tq, tq)
    kbuf[pl.ds(row0, tq), :] = k_ref[...].astype(BF16)
    vbuf[pl.ds(row0, tq), :] = v_ref[...].astype(BF16)

    q = q_ref[...]
    lane = lax.broadcasted_iota(jnp.int32, (tq, LANES), 1)
    cq = jnp.sum(jnp.where(lane == h, ccol_ref[...], 0.0), axis=1, keepdims=True)

    def step(kj, carry, masked):
        m, l, acc = carry
        col0 = pl.multiple_of(kj * tq, tq)
        k = kbuf[pl.ds(col0, tq), :]
        v = vbuf[pl.ds(col0, tq), :]
        s = _dot_nt(q, k) + (cq - crow_ref[h, kj])
        if masked:
            r = lax.broadcasted_iota(jnp.int32, (tq, tq), 0)
            c = lax.broadcasted_iota(jnp.int32, (tq, tq), 1)
            s = jnp.where(c <= r, s, NEG_INF)
        m_new = jnp.maximum(m, jnp.max(s, axis=-1, keepdims=True))
        alpha = jnp.exp(m - m_new)
        p = jnp.exp(s - m_new)
        l = alpha * l + jnp.sum(p, axis=-1, keepdims=True)
        acc = alpha * acc + _dot(p.astype(BF16), v)
        return m_new, l, acc

    init = (jnp.full((tq, 1), NEG_INF, F32), jnp.zeros((tq, 1), F32), jnp.zeros((tq, HEAD_DIM), F32))
    carry = lax.fori_loop(0, qi, lambda kj, cr: step(kj, cr, False), init)
    _, l, acc = step(qi, carry, True)
    o_ref[...] = (acc / l).astype(BF16)


def _fox_prompt(fq, fk, fv, c_col, c_row, *, tq, n_heads):
    s, _ = fq.shape
    nq = s // tq
    blk = lambda: pl.BlockSpec((tq, HEAD_DIM), lambda h, qi: (qi, h))
    return pl.pallas_call(
        functools.partial(_fox_prompt_kernel, tq=tq),
        grid=(n_heads, nq),
        in_specs=[blk(), blk(), blk(),
                  pl.BlockSpec((tq, LANES), lambda h, qi: (qi, 0)),
                  _resident(c_row.shape)],
        out_specs=blk(),
        out_shape=jax.ShapeDtypeStruct(fq.shape, BF16),
        scratch_shapes=[pltpu.VMEM((s, HEAD_DIM), BF16), pltpu.VMEM((s, HEAD_DIM), BF16)],
        compiler_params=_cparams(("arbitrary", "arbitrary")),
        name="fox_prompt",
    )(fq, fk, fv, c_col, c_row)


def _fox_sample_kernel(q_ref, kc_ref, vc_ref, kn_ref, vn_ref, lfc_ref, lfn_ref, ccol_ref,
                       tric_ref, trin_ref, o_ref, *, n_heads):
    c_cache = _split3_dot(lfc_ref[0], tric_ref[...], tri_first=False)
    p_len = c_cache.shape[1]
    ck_cache = c_cache - c_cache[:, p_len - 1:p_len]
    ck_new = _split3_dot(lfn_ref[0], trin_ref[...], tri_first=False)
    ccol = ccol_ref[...]
    n_new = ck_new.shape[1]
    r = lax.broadcasted_iota(jnp.int32, (n_new, n_new), 0)
    c = lax.broadcasted_iota(jnp.int32, (n_new, n_new), 1)
    for h in range(n_heads):
        sl = slice(h * HEAD_DIM, (h + 1) * HEAD_DIM)
        q = q_ref[:, sl]
        cq = ccol[:, h:h + 1]
        s_c = _dot_nt(q, kc_ref[0, :, sl].astype(BF16)) + (cq - ck_cache[h:h + 1, :])
        s_n = _dot_nt(q, kn_ref[:, sl].astype(BF16)) + (cq - ck_new[h:h + 1, :])
        s_n = jnp.where(c <= r, s_n, NEG_INF)
        m = jnp.maximum(jnp.max(s_c, axis=-1, keepdims=True), jnp.max(s_n, axis=-1, keepdims=True))
        p_c = jnp.exp(s_c - m)
        p_n = jnp.exp(s_n - m)
        l = jnp.sum(p_c, axis=-1, keepdims=True) + jnp.sum(p_n, axis=-1, keepdims=True)
        o = _dot(p_c.astype(BF16), vc_ref[0, :, sl].astype(BF16)) + \
            _dot(p_n.astype(BF16), vn_ref[:, sl].astype(BF16))
        o_ref[:, sl] = (o / l).astype(BF16)


def _fox_sample(fq, kc, vc, kn, vn, lfc_row, lfn_row, c_col, *, n_heads, seq_len):
    m, d_fox = fq.shape
    nb, p_len, _ = kc.shape
    tric = jnp.asarray(np.arange(p_len)[:, None] <= np.arange(p_len)[None, :], dtype=BF16)
    trin = jnp.asarray(np.arange(seq_len)[:, None] <= np.arange(seq_len)[None, :], dtype=BF16)
    row = lambda w: pl.BlockSpec((seq_len, w), lambda b: (b, 0))
    cache = pl.BlockSpec((1, p_len, d_fox), lambda b: (b, 0, 0))
    return pl.pallas_call(
        functools.partial(_fox_sample_kernel, n_heads=n_heads),
        grid=(nb,),
        in_specs=[row(d_fox), cache, cache, row(d_fox), row(d_fox),
                  pl.BlockSpec((1, HEAD_ROWS, p_len), lambda b: (b, 0, 0)),
                  pl.BlockSpec((1, HEAD_ROWS, seq_len), lambda b: (b, 0, 0)),
                  row(LANES), _resident(tric.shape), _resident(trin.shape)],
        out_specs=row(d_fox),
        out_shape=jax.ShapeDtypeStruct((m, d_fox), BF16),
        compiler_params=_cparams(("arbitrary",)),
        name="fox_sample",
    )(fq, kc, vc, kn, vn, lfc_row, lfn_row, c_col, tric, trin)


def _log_gamma(h):
    return float(np.log1p(-np.exp2(np.float32(-5.0 - h)), dtype=np.float32))


def _retention_kernel(q_ref, k_ref, v_ref, g_ref, s0_ref, gn_ref, o_ref, s_ref, *, n_heads, t_blk):
    step = pl.program_id(1)

    @pl.when(step == 0)
    def _():
        s_ref[...] = s0_ref[...]

    ti = lax.broadcasted_iota(jnp.int32, (t_blk, t_blk), 0)
    si = lax.broadcasted_iota(jnp.int32, (t_blk, t_blk), 1)
    dist = jnp.abs(ti - si).astype(F32)
    shift = CHUNK.bit_length() - 1
    visible = jnp.right_shift(si, shift) <= jnp.right_shift(ti, shift)
    tcol = lax.broadcasted_iota(jnp.int32, (t_blk, 1), 0).astype(F32)
    for h in range(n_heads):
        lg = _log_gamma(h)
        sl = slice(h * HEAD_DIM, (h + 1) * HEAD_DIM)
        q = q_ref[:, sl]
        k = k_ref[:, sl]
        v = v_ref[:, sl]
        decay = jnp.where(visible, jnp.exp(lg * dist), 0.0)
        a = _dot_nt(q, k) * decay
        state = s_ref[0, h]
        o = _dot(a.astype(BF16), v) + _dot(q, state.astype(BF16)) * jnp.exp(lg * (tcol + 1.0))
        kd = (k.astype(F32) * jnp.exp(lg * (t_blk - 1.0 - tcol))).astype(BF16)
        s_ref[0, h] = math.exp(lg * t_blk) * state + _dot_tn(kd, v)
        mu = jnp.mean(o, axis=-1, keepdims=True)
        oc = o - mu
        var = jnp.mean(oc * oc, axis=-1, keepdims=True)
        rn = oc * lax.rsqrt(var + GN_EPS) * gn_ref[:, sl]
        gate = g_ref[:, sl].astype(F32)
        o_ref[:, sl] = (gate / (1.0 + jnp.exp(-gate)) * rn).astype(BF16)


def _retention(rq, rk, rv, rg, s0, gn_g, *, t_blk, seq_len, n_heads):
    m, d_ret = rq.shape
    nb = m // seq_len
    steps = seq_len // t_blk
    row = lambda: pl.BlockSpec((t_blk, d_ret), lambda b, s: (b * steps + s, 0))
    st = lambda: pl.BlockSpec((1, n_heads, HEAD_DIM, HEAD_DIM), lambda b, s: (b, 0, 0, 0))
    return pl.pallas_call(
        functools.partial(_retention_kernel, n_heads=n_heads, t_blk=t_blk),
        grid=(nb, steps),
        in_specs=[row(), row(), row(), row(), st(), pl.BlockSpec((1, d_ret), lambda b, s: (0, 0))],
        out_specs=(row(), st()),
        out_shape=(jax.ShapeDtypeStruct((m, d_ret), BF16),
                   jax.ShapeDtypeStruct((nb, n_heads, HEAD_DIM, HEAD_DIM), F32)),
        compiler_params=_cparams(("arbitrary", "arbitrary")),
        name="retention",
    )(rq, rk, rv, rg, s0, gn_g)


def _mix_out_kernel(x_ref, fo_ref, ro_ref, mo_ref, wo_ref, g_ref, b_ref, o_ref, cat_ref, *, alpha):
    d_fox = fo_ref.shape[1]
    d_ret = ro_ref.shape[1]
    cat_ref[:, 0:d_fox] = fo_ref[...]
    cat_ref[:, d_fox:d_fox + d_ret] = ro_ref[...]
    cat_ref[:, d_fox + d_ret:] = mo_ref[...]
    z = alpha * x_ref[...] + _dot(cat_ref[...], wo_ref[...])
    o_ref[...] = _layernorm(z, g_ref[...], b_ref[...])


def _mix_out(x, fo, ro, mo, wo, g, b, *, tm, alpha):
    m, d_model = x.shape
    row = lambda w: pl.BlockSpec((tm, w), lambda i: (i, 0))
    return pl.pallas_call(
        functools.partial(_mix_out_kernel, alpha=alpha),
        grid=(m // tm,),
        in_specs=[row(d_model), row(fo.shape[1]), row(ro.shape[1]), row(mo.shape[1]),
                  _resident(wo.shape), _resident(g.shape), _resident(b.shape)],
        out_specs=row(d_model),
        out_shape=jax.ShapeDtypeStruct((m, d_model), F32),
        scratch_shapes=[pltpu.VMEM((tm, wo.shape[0]), BF16)],
        compiler_params=_cparams(("arbitrary",)),
        name="mix_out",
    )(x, fo, ro, mo, wo, g, b)


def _gelu_tanh(x):
    c = math.sqrt(2.0 / math.pi)
    return x * (0.5 * (1.0 + jnp.tanh(c * (x + 0.044715 * (x * x * x)))))


def _conv_ffn_kernel(h_ref, wg_ref, wu_ref, wd_ref, cw_ref, cb_ref, past_ref, g_ref, b_ref,
                     y_ref, cv_ref, hb_ref, gbuf_ref, carry_ref, *, alpha, seq_len, tm):
    i = pl.program_id(0)
    f = pl.program_id(1)
    nf = pl.num_programs(1)

    @pl.when(f == 0)
    def _():
        hb_ref[...] = h_ref[...].astype(BF16)

    hb = hb_ref[...]
    gate = _dot(hb, wg_ref[...])
    up = _dot(hb, wu_ref[...])
    w0 = cw_ref[0:1, :]
    w1 = cw_ref[1:2, :]
    w2 = cw_ref[2:3, :]
    cb = cb_ref[...]

    lt = min(seq_len, tm)
    halo = 8
    parts = []
    for s in range(tm // lt):
        gs = gate[s * lt:(s + 1) * lt]
        if seq_len > tm:
            tiles_per_seq = seq_len // tm
            first = i % tiles_per_seq == 0

            @pl.when(first)
            def _():
                gbuf_ref[halo - 2:halo, :] = past_ref[i // tiles_per_seq]

            @pl.when(jnp.logical_not(first))
            def _():
                gbuf_ref[halo - 2:halo, :] = carry_ref[f]
        else:
            gbuf_ref[halo - 2:halo, :] = past_ref[i * (tm // lt) + s]
        gbuf_ref[halo:halo + lt, :] = gs
        g1 = gbuf_ref[halo - 1:halo - 1 + lt, :]
        g2 = gbuf_ref[halo - 2:halo - 2 + lt, :]
        last2 = gbuf_ref[halo + lt - 2:halo + lt, :]
        if seq_len > tm:
            carry_ref[f] = last2
        cv_ref[s] = last2
        gc = cb + w0 * g2 + w1 * g1 + w2 * gs
        parts.append((_gelu_tanh(gc) * up[s * lt:(s + 1) * lt]).astype(BF16))
    hid = parts[0] if len(parts) == 1 else jnp.concatenate(parts, axis=0)
    contrib = _dot(hid, wd_ref[...])

    @pl.when(f == 0)
    def _():
        y_ref[...] = contrib

    @pl.when(f > 0)
    def _():
        y_ref[...] += contrib

    @pl.when(f == nf - 1)
    def _():
        y_ref[...] = _layernorm(alpha * h_ref[...] + y_ref[...], g_ref[...], b_ref[...])


def _conv_ffn(h1, wg, wu, wd, conv_w, conv_b, past, g, b, *, tm, tf, seq_len, alpha):
    m, d_model = h1.shape
    d_ff = wg.shape[1]
    n_seq = past.shape[0]
    assert m % tm == 0 and d_ff % tf == 0 and (seq_len % tm == 0 or tm % seq_len == 0)
    nf = d_ff // tf
    lt = min(seq_len, tm)
    y, cv_all = pl.pallas_call(
        functools.partial(_conv_ffn_kernel, alpha=alpha, seq_len=seq_len, tm=tm),
        grid=(m // tm, nf),
        in_specs=[pl.BlockSpec((tm, d_model), lambda i, f: (i, 0)),
                  pl.BlockSpec((d_model, tf), lambda i, f: (0, f)),
                  pl.BlockSpec((d_model, tf), lambda i, f: (0, f)),
                  pl.BlockSpec((tf, d_model), lambda i, f: (f, 0)),
                  pl.BlockSpec((CONV_W, tf), lambda i, f: (0, f)),
                  pl.BlockSpec((1, tf), lambda i, f: (0, f)),
                  pl.BlockSpec((n_seq, CONV_W - 1, tf), lambda i, f: (0, 0, f)),
                  pl.BlockSpec((1, d_model), lambda i, f: (0, 0)),
                  pl.BlockSpec((1, d_model), lambda i, f: (0, 0))],
        out_specs=(pl.BlockSpec((tm, d_model), lambda i, f: (i, 0)),
                   pl.BlockSpec((tm // lt, CONV_W - 1, tf), lambda i, f: (i, 0, f))),
        out_shape=(jax.ShapeDtypeStruct((m, d_model), F32),
                   jax.ShapeDtypeStruct((m // lt, CONV_W - 1, d_ff), F32)),
        scratch_shapes=[pltpu.VMEM((tm, d_model), BF16),
                        pltpu.VMEM((8 + lt, tf), F32),
                        pltpu.VMEM((nf, CONV_W - 1, tf), F32)],
        compiler_params=_cparams(("arbitrary", "arbitrary")),
        name="conv_ffn",
    )(h1, wg, wu, wd, conv_w, conv_b, past, g, b)
    return y, cv_all.reshape(n_seq, seq_len // lt, CONV_W - 1, d_ff)[:, -1]


def _rope_tables(pos):
    half = HEAD_DIM // 2
    inv = ROPE_BASE ** (-jnp.arange(half, dtype=F32) / half)
    ang = pos.astype(F32)[:, None] * inv[None, :]
    cos = jnp.cos(ang)
    sin = jnp.sin(ang)
    return jnp.concatenate([cos, cos], axis=-1), jnp.concatenate([-sin, sin], axis=-1)


def _layer(x, pos, seq_len, mk, mv, w, *, alpha, tm_proj, tm_mix, tm_ffn, tf, t_ret, s0, conv_past,
           fox_fn):
    cos, sin = _rope_tables(pos)
    fq, fk, fv, lf, c, rq, rk, rv, rg, mo = _in_proj(
        x, w["wa"], w["wf"], w["bf"], cos, sin, mk, mv, tm=tm_proj, seq_len=seq_len,
        d_fox=w["d_fox"], d_ret=w["d_ret"], d_memq=w["d_memq"])
    fo = fox_fn(fq, fk, fv, lf, c)
    ro, s_new = _retention(rq, rk, rv, rg, s0, w["gn_g"], t_blk=t_ret, seq_len=seq_len,
                           n_heads=w["d_ret"] // HEAD_DIM)
    h1 = _mix_out(x, fo, ro, mo, w["wo"], w["ln1_g"], w["ln1_b"], tm=tm_mix, alpha=alpha)
    y, cv = _conv_ffn(h1, w["wg"], w["wu"], w["wd"], w["conv_w"], w["conv_b"], conv_past,
                      w["ln2_g"], w["ln2_b"], tm=tm_ffn, tf=tf, seq_len=seq_len, alpha=alpha)
    return y, fk, fv, lf, s_new, cv


def kernel(x_prompt, x_sample, cache_fox_k, cache_fox_v, cache_fox_logf, state_ret, cache_mem_k,
           cache_mem_v, state_conv, mem_prompt, w_in, b_f, w_mem_kv, ret_gn_g, w_o, ln1_g, ln1_b,
           w_gate, w_up, conv_w, conv_b, w_down, ln2_g, ln2_b):
    batch, seq, d_model = x_prompt.shape
    dec_batch, dec_seq, _ = x_sample.shape
    depth = w_in.shape[0]
    past_len = cache_fox_k.shape[2]
    n_fox = cache_fox_k.shape[3]
    n_ret = state_ret.shape[2]
    n_mem = cache_mem_k.shape[3]
    n_mem_tok = cache_mem_k.shape[2]
    d_fox, d_ret, d_memq = n_fox * HEAD_DIM, n_ret * HEAD_DIM, n_mem * HEAD_DIM
    d_ff = w_gate.shape[2]
    assert batch == 1 and depth == 1 and n_fox <= 8
    alpha = (2.0 * depth) ** 0.25
    l = 0

    o_ff = 3 * d_fox
    w_in_l = w_in[l]
    wa = jnp.concatenate([w_in_l[:, :o_ff], w_in_l[:, o_ff + n_fox:]], axis=1).astype(BF16)
    wf = jnp.pad(w_in_l[:, o_ff:o_ff + n_fox], ((0, 0), (0, LANES - n_fox))).astype(BF16)
    bfp = jnp.pad(b_f[l], (0, LANES - n_fox)).reshape(1, LANES)
    w = dict(wa=wa, wf=wf, bf=bfp, d_fox=d_fox, d_ret=d_ret, d_memq=d_memq,
             gn_g=ret_gn_g[l].reshape(1, d_ret), wo=w_o[l].astype(BF16),
             ln1_g=ln1_g[l].reshape(1, d_model), ln1_b=ln1_b[l].reshape(1, d_model),
             wg=w_gate[l].astype(BF16), wu=w_up[l].astype(BF16), wd=w_down[l].astype(BF16),
             conv_w=conv_w[l], conv_b=conv_b[l].reshape(1, d_ff),
             ln2_g=ln2_g[l].reshape(1, d_model), ln2_b=ln2_b[l].reshape(1, d_model))

    mkv = _mem_kv(mem_prompt.reshape(n_mem_tok, d_model), w_mem_kv[l].astype(BF16))
    mk_p, mv_p = mkv[:, :d_memq], mkv[:, d_memq:]

    tq = 512

    def fox_prompt_fn(fq, fk, fv, lf, c):
        c_row = jnp.transpose(c[:, :8]).reshape(8, seq // tq, 1, tq)
        return _fox_prompt(fq, fk, fv, c, c_row, tq=tq, n_heads=n_fox)

    y_p, fk_p, fv_p, lf_p, s_p, cv_p = _layer(
        x_prompt.reshape(seq, d_model), jnp.arange(seq), seq, mk_p[None], mv_p[None], w,
        alpha=alpha, tm_proj=256, tm_mix=512, tm_ffn=512, tf=512, t_ret=256,
        s0=jnp.zeros((1, n_ret, HEAD_DIM, HEAD_DIM), F32),
        conv_past=jnp.zeros((1, CONV_W - 1, d_ff), F32), fox_fn=fox_prompt_fn)

    m_s = dec_batch * dec_seq
    kc = cache_fox_k[l].reshape(dec_batch, past_len, d_fox)
    vc = cache_fox_v[l].reshape(dec_batch, past_len, d_fox)
    lfc_row = jnp.pad(jnp.swapaxes(cache_fox_logf[l], 1, 2), ((0, 0), (0, HEAD_ROWS - n_fox), (0, 0)))

    def fox_sample_fn(fq, fk, fv, lf, c):
        lfn_row = jnp.swapaxes(lf.reshape(dec_batch, dec_seq, LANES)[:, :, :HEAD_ROWS], 1, 2)
        return _fox_sample(fq, kc, vc, fk, fv, lfc_row, lfn_row, c, n_heads=n_fox, seq_len=dec_seq)

    y_s, fk_s, fv_s, lf_s, s_s, cv_s = _layer(
        x_sample.reshape(m_s, d_model), jnp.tile(past_len + jnp.arange(dec_seq), dec_batch), dec_seq,
        cache_mem_k[l].reshape(dec_batch, n_mem_tok, d_memq),
        cache_mem_v[l].reshape(dec_batch, n_mem_tok, d_memq), w,
        alpha=alpha, tm_proj=dec_seq, tm_mix=m_s, tm_ffn=m_s, tf=512, t_ret=dec_seq,
        s0=state_ret[l], conv_past=state_conv[l], fox_fn=fox_sample_fn)

    return (y_p.reshape(batch, seq, d_model), y_s.reshape(dec_batch, dec_seq, d_model),
            fk_p.reshape(1, batch, seq, n_fox, HEAD_DIM), fv_p.reshape(1, batch, seq, n_fox, HEAD_DIM),
            lf_p[:, :n_fox].reshape(1, batch, seq, n_fox),
            s_p.reshape(1, batch, n_ret, HEAD_DIM, HEAD_DIM),
            mk_p.reshape(1, batch, n_mem_tok, n_mem, HEAD_DIM),
            mv_p.reshape(1, batch, n_mem_tok, n_mem, HEAD_DIM),
            cv_p.reshape(1, batch, CONV_W - 1, d_ff),
            fk_s.reshape(1, dec_batch, dec_seq, n_fox, HEAD_DIM),
            fv_s.reshape(1, dec_batch, dec_seq, n_fox, HEAD_DIM),
            lf_s[:, :n_fox].reshape(1, dec_batch, dec_seq, n_fox),
            s_s.reshape(1, dec_batch, n_ret, HEAD_DIM, HEAD_DIM),
            cv_s.reshape(1, dec_batch, CONV_W - 1, d_ff))
```

```python
import functools
import math

import numpy as np
import jax
import jax.numpy as jnp
from jax import lax
from jax.experimental import pallas as pl
from jax.experimental.pallas import tpu as pltpu

F32 = jnp.float32
BF16 = jnp.bfloat16

HEAD_DIM = 128
CHUNK = 64
CONV_W = 3
ROPE_BASE = 10000.0
LN_EPS = 1e-5
GN_EPS = 1e-5
LANES = 128
V7X_VMEM_BYTES = 64 * 1024 * 1024
VMEM_LIMIT = V7X_VMEM_BYTES - 8 * 1024 * 1024
SCALE = HEAD_DIM ** -0.5
LOG2E = math.log2(math.e)
NEG_INF = float("-inf")
HEAD_ROWS = 16


def _cparams(sem):
    return pltpu.CompilerParams(dimension_semantics=sem, vmem_limit_bytes=VMEM_LIMIT)


def _resident(shape):
    nd = len(shape)
    return pl.BlockSpec(shape, lambda *_: (0,) * nd, pipeline_mode=pl.Buffered(1))


def _dot(a, b):
    return jnp.dot(a, b, preferred_element_type=F32)


def _dot_nt(a, b):
    return lax.dot_general(a, b, (((1,), (1,)), ((), ())), preferred_element_type=F32)


def _dot_tn(a, b):
    return lax.dot_general(a, b, (((0,), (0,)), ((), ())), preferred_element_type=F32)


def _split3_dot(a, tri, *, tri_first):
    hi = a.astype(BF16)
    r1 = a - hi.astype(F32)
    mid = r1.astype(BF16)
    lo = (r1 - mid.astype(F32)).astype(BF16)
    if tri_first:
        return _dot(tri, hi) + _dot(tri, mid) + _dot(tri, lo)
    return _dot(hi, tri) + _dot(mid, tri) + _dot(lo, tri)


def _layernorm(z, g, b):
    mu = jnp.mean(z, axis=-1, keepdims=True)
    zc = z - mu
    var = jnp.mean(zc * zc, axis=-1, keepdims=True)
    return zc * lax.rsqrt(var + LN_EPS) * g + b


def _memkv_kernel(m_ref, w_ref, o_ref):
    o_ref[...] = _dot(m_ref[...].astype(BF16), w_ref[...])


def _mem_kv(mem, w_bf):
    m, _ = mem.shape
    return pl.pallas_call(
        _memkv_kernel,
        out_shape=jax.ShapeDtypeStruct((m, w_bf.shape[1]), F32),
        compiler_params=pltpu.CompilerParams(vmem_limit_bytes=VMEM_LIMIT),
        name="mem_kv",
    )(mem, w_bf)


def _in_proj_kernel(x_ref, wa_ref, wf_ref, bf_ref, cos_ref, sin_ref, mk_ref, mv_ref, tri_ref,
                    fq_ref, fk_ref, fv_ref, lf_ref, c_ref, rq_ref, rk_ref, rv_ref, rg_ref, mo_ref,
                    carry_ref, *, d_fox, d_ret, d_memq, tiles_per_seq):
    i = pl.program_id(0)
    xb = x_ref[...].astype(BF16)

    def proj(lo, width):
        return _dot(xb, wa_ref[:, lo:lo + width])

    o_fk = d_fox
    o_fv = 2 * d_fox
    o_rq = 3 * d_fox
    o_rk = o_rq + d_ret
    o_rv = o_rk + d_ret
    o_rg = o_rv + d_ret
    o_mq = o_rg + d_ret

    fq_ref[...] = (proj(0, d_fox) * (SCALE * LOG2E)).astype(BF16)
    fk_ref[...] = proj(o_fk, d_fox)
    fv_ref[...] = proj(o_fv, d_fox)

    z = _dot(xb, wf_ref[...]) + bf_ref[...]
    lf = jnp.minimum(z, 0.0) - jnp.log1p(jnp.exp(-jnp.abs(z)))
    lf_ref[...] = lf

    @pl.when(i % tiles_per_seq == 0)
    def _():
        carry_ref[...] = jnp.zeros_like(carry_ref)

    c = _split3_dot(lf, tri_ref[...], tri_first=True) + carry_ref[0:1, :]
    c_ref[...] = c
    tm = c.shape[0]
    carry_ref[...] = jnp.broadcast_to(c[tm - 1:tm, :], carry_ref.shape)

    cos = cos_ref[...]
    sin = sin_ref[...]

    def rope(t):
        return t * cos + pltpu.roll(t, HEAD_DIM // 2, axis=1) * sin

    zq = proj(o_rq, d_ret)
    zk = proj(o_rk, d_ret)
    for h in range(d_ret // HEAD_DIM):
        sl = slice(h * HEAD_DIM, (h + 1) * HEAD_DIM)
        rq_ref[:, sl] = rope(zq[:, sl]).astype(BF16)
        rk_ref[:, sl] = (rope(zk[:, sl]) * SCALE).astype(BF16)
    rv_ref[...] = proj(o_rv, d_ret).astype(BF16)
    rg_ref[...] = proj(o_rg, d_ret).astype(BF16)

    zm = proj(o_mq, d_memq)
    for h in range(d_memq // HEAD_DIM):
        sl = slice(h * HEAD_DIM, (h + 1) * HEAD_DIM)
        q = (zm[:, sl] * SCALE).astype(BF16)
        s = _dot_nt(q, mk_ref[0, :, sl].astype(BF16))
        p = jnp.exp(s - jnp.max(s, axis=-1, keepdims=True))
        l = jnp.sum(p, axis=-1, keepdims=True)
        o = _dot(p.astype(BF16), mv_ref[0, :, sl].astype(BF16))
        mo_ref[:, sl] = (o / l).astype(BF16)


def _in_proj(x, wa, wf, bfp, cos, sin, mk, mv, *, tm, seq_len, d_fox, d_ret, d_memq):
    m, d_model = x.shape
    assert m % tm == 0 and seq_len % tm == 0
    tiles_per_seq = seq_len // tm
    n_mem_tok = mk.shape[1]
    tri = (np.arange(tm)[:, None] >= np.arange(tm)[None, :])
    tri = jnp.asarray(tri, dtype=BF16)
    row = lambda w: pl.BlockSpec((tm, w), lambda i: (i, 0))
    memspec = pl.BlockSpec((1, n_mem_tok, d_memq), lambda i: (i // tiles_per_seq, 0, 0))
    kern = functools.partial(_in_proj_kernel, d_fox=d_fox, d_ret=d_ret, d_memq=d_memq,
                             tiles_per_seq=tiles_per_seq)
    out_shape = (
        jax.ShapeDtypeStruct((m, d_fox), BF16),
        jax.ShapeDtypeStruct((m, d_fox), F32),
        jax.ShapeDtypeStruct((m, d_fox), F32),
        jax.ShapeDtypeStruct((m, LANES), F32),
        jax.ShapeDtypeStruct((m, LANES), F32),
        jax.ShapeDtypeStruct((m, d_ret), BF16),
        jax.ShapeDtypeStruct((m, d_ret), BF16),
        jax.ShapeDtypeStruct((m, d_ret), BF16),
        jax.ShapeDtypeStruct((m, d_ret), BF16),
        jax.ShapeDtypeStruct((m, d_memq), BF16),
    )
    return pl.pallas_call(
        kern,
        grid=(m // tm,),
        in_specs=[row(d_model), _resident(wa.shape), _resident(wf.shape), _resident(bfp.shape),
                  row(HEAD_DIM), row(HEAD_DIM), memspec, memspec, _resident(tri.shape)],
        out_specs=(row(d_fox), row(d_fox), row(d_fox), row(LANES), row(LANES),
                   row(d_ret), row(d_ret), row(d_ret), row(d_ret), row(d_memq)),
        out_shape=out_shape,
        scratch_shapes=[pltpu.VMEM((8, LANES), F32)],
        compiler_params=_cparams(("arbitrary",)),
        name="in_proj",
    )(x, wa, wf, bfp, cos, sin, mk, mv, tri)


def _split3_f32(a):
    hi = a.astype(BF16).astype(F32)
    r1 = a - hi
    mid = r1.astype(BF16).astype(F32)
    lo = (r1 - mid).astype(BF16).astype(F32)
    return hi, mid, lo


def _fox_prompt_kernel(q_ref, k_ref, v_ref, ccol_ref, o_ref, qa, kbuf, vbuf, cstart, acc, sbuf, *, tb,
                       nsub):
    h = pl.program_id(0)
    qi = pl.program_id(1)
    tq = tb * nsub
    row0 = pl.multiple_of(qi * tq, tq)
    lane = lax.broadcasted_iota(jnp.int32, (tq, LANES), 1)
    ch = jnp.sum(jnp.where(lane == h, ccol_ref[...], 0.0), axis=1, keepdims=True)
    rel = []
    for s in range(nsub):
        blk = ch[s * tb:(s + 1) * tb]
        start = blk[0:1]
        cstart[nsub * qi + s] = jnp.broadcast_to(start, (8, LANES))
        rel.append((blk - start) * LOG2E)
    rel = rel[0] if nsub == 1 else jnp.concatenate(rel, axis=0)
    hi, mid, lo = _split3_f32(rel)
    qbias = jnp.where(lane == 0, hi, jnp.where(lane == 1, mid, jnp.where(lane == 2, lo,
                      jnp.where(lane < 6, 1.0, 0.0))))
    kbias = jnp.where(lane < 3, 1.0, jnp.where(lane == 3, -hi, jnp.where(lane == 4, -mid,
                      jnp.where(lane == 5, -lo, 0.0))))
    qa[:, 0:HEAD_DIM] = q_ref[...]
    qa[:, HEAD_DIM:] = qbias.astype(BF16)
    kbuf[pl.ds(row0, tq), 0:HEAD_DIM] = k_ref[...].astype(BF16)
    kbuf[pl.ds(row0, tq), HEAD_DIM:] = kbias.astype(BF16)
    vbuf[pl.ds(row0, tq), 0:HEAD_DIM] = v_ref[...].astype(BF16)
    vbuf[pl.ds(row0, tq), HEAD_DIM:] = jnp.where(lane == 0, 1.0, 0.0).astype(BF16)
    acc[...] = jnp.zeros_like(acc)

    r_id = lax.broadcasted_iota(jnp.int32, (tb, LANES), 0)
    c_id = lax.broadcasted_iota(jnp.int32, (tb, LANES), 1)
    n_ct = tb // LANES
    subs = range(nsub)

    def scores(s, kj):
        col0 = pl.multiple_of(kj * tb, tb)
        return _dot_nt(qa[s * tb:(s + 1) * tb, :], kbuf[pl.ds(col0, tb), :])

    def softmax_pv(s, kj, s2, m, masked):
        col0 = pl.multiple_of(kj * tb, tb)
        tiles = [s2[:, t * LANES:(t + 1) * LANES] for t in range(n_ct)]
        if masked:
            tiles = [jnp.where(c_id + t * LANES <= r_id, tiles[t], NEG_INF) for t in range(n_ct)]
        delta = (cstart[nsub * qi + s][0:1, :] - cstart[kj][0:1, :]) * LOG2E
        m_rel = m - delta
        tmax = functools.reduce(jnp.maximum, tiles)
        m_new = jnp.maximum(m_rel, jnp.max(tmax, axis=-1, keepdims=True))
        alpha = jnp.exp2(m_rel - m_new)
        p = jnp.concatenate([jnp.exp2(t - m_new).astype(BF16) for t in tiles], axis=1)
        pv = _dot(p, vbuf[pl.ds(col0, tb), :])
        a = acc[s]
        acc[s] = jnp.concatenate([alpha * a[:, 0:LANES], alpha * a[:, LANES:]], axis=1) + pv
        return m_new + delta

    for s in subs:
        sbuf[s] = scores(s, 0)

    def body(t, ms):
        kj = 2 * t
        s_a = [sbuf[s] for s in subs]
        s_b = [scores(s, kj + 1) for s in subs]
        ms = [softmax_pv(s, kj, s_a[s], ms[s], False) for s in subs]
        s_c = [scores(s, kj + 2) for s in subs]
        ms = [softmax_pv(s, kj + 1, s_b[s], ms[s], False) for s in subs]
        for s in subs:
            sbuf[s] = s_c[s]
        return tuple(ms)

    assert nsub % 2 == 0
    ms = lax.fori_loop(0, (nsub // 2) * qi, body,
                       tuple(jnp.full((tb, LANES), NEG_INF, F32) for _ in subs))
    for s in subs:
        m = ms[s]
        for t in range(s + 1):
            s2 = sbuf[s] if t == 0 else scores(s, nsub * qi + t)
            m = softmax_pv(s, nsub * qi + t, s2, m, t == s)
        a = acc[s]
        o_ref[s * tb:(s + 1) * tb, :] = (a[:, 0:HEAD_DIM] / a[:, HEAD_DIM:HEAD_DIM + 1]).astype(BF16)


def _fox_prompt(fq, fk, fv, c_col, *, tb, nsub, n_heads):
    s, _ = fq.shape
    tq = tb * nsub
    assert s % tq == 0
    blk = lambda: pl.BlockSpec((tq, HEAD_DIM), lambda h, qi: (qi, h))
    return pl.pallas_call(
        functools.partial(_fox_prompt_kernel, tb=tb, nsub=nsub),
        grid=(n_heads, s // tq),
        in_specs=[blk(), blk(), blk(), pl.BlockSpec((tq, LANES), lambda h, qi: (qi, 0))],
        out_specs=blk(),
        out_shape=jax.ShapeDtypeStruct(fq.shape, BF16),
        scratch_shapes=[pltpu.VMEM((tq, 2 * HEAD_DIM), BF16),
                        pltpu.VMEM((s, 2 * HEAD_DIM), BF16),
                        pltpu.VMEM((s, 2 * HEAD_DIM), BF16),
                        pltpu.VMEM((s // tb, 8, LANES), F32),
                        pltpu.VMEM((nsub, tb, 2 * HEAD_DIM), F32),
                        pltpu.VMEM((nsub, tb, tb), F32)],
        compiler_params=_cparams(("arbitrary", "arbitrary")),
        name="fox_prompt",
    )(fq, fk, fv, c_col)


def _fox_sample_kernel(q_ref, kc_ref, vc_ref, kn_ref, vn_ref, lfc_ref, lfn_ref, ccol_ref,
                       tric_ref, trin_ref, o_ref, *, n_heads):
    c_cache = _split3_dot(lfc_ref[0], tric_ref[...], tri_first=False)
    p_len = c_cache.shape[1]
    ck_cache = c_cache - c_cache[:, p_len - 1:p_len]
    ck_new = _split3_dot(lfn_ref[0], trin_ref[...], tri_first=False)
    ccol = ccol_ref[...]
    n_new = ck_new.shape[1]
    r = lax.broadcasted_iota(jnp.int32, (n_new, n_new), 0)
    c = lax.broadcasted_iota(jnp.int32, (n_new, n_new), 1)
    for h in range(n_heads):
        sl = slice(h * HEAD_DIM, (h + 1) * HEAD_DIM)
        q = q_ref[:, sl]
        cq = ccol[:, h:h + 1]
        s_c = _dot_nt(q, kc_ref[0, :, sl].astype(BF16)) + (cq - ck_cache[h:h + 1, :]) * LOG2E
        s_n = _dot_nt(q, kn_ref[:, sl].astype(BF16)) + (cq - ck_new[h:h + 1, :]) * LOG2E
        s_n = jnp.where(c <= r, s_n, NEG_INF)
        m = jnp.maximum(jnp.max(s_c, axis=-1, keepdims=True), jnp.max(s_n, axis=-1, keepdims=True))
        p_c = jnp.exp2(s_c - m)
        p_n = jnp.exp2(s_n - m)
        l = jnp.sum(p_c, axis=-1, keepdims=True) + jnp.sum(p_n, axis=-1, keepdims=True)
        o = _dot(p_c.astype(BF16), vc_ref[0, :, sl].astype(BF16)) + \
            _dot(p_n.astype(BF16), vn_ref[:, sl].astype(BF16))
        o_ref[:, sl] = (o / l).astype(BF16)


def _fox_sample(fq, kc, vc, kn, vn, lfc_row, lfn_row, c_col, *, n_heads, seq_len):
    m, d_fox = fq.shape
    nb, p_len, _ = kc.shape
    tric = jnp.asarray(np.arange(p_len)[:, None] <= np.arange(p_len)[None, :], dtype=BF16)
    trin = jnp.asarray(np.arange(seq_len)[:, None] <= np.arange(seq_len)[None, :], dtype=BF16)
    row = lambda w: pl.BlockSpec((seq_len, w), lambda b: (b, 0))
    cache = pl.BlockSpec((1, p_len, d_fox), lambda b: (b, 0, 0))
    return pl.pallas_call(
        functools.partial(_fox_sample_kernel, n_heads=n_heads),
        grid=(nb,),
        in_specs=[row(d_fox), cache, cache, row(d_fox), row(d_fox),
                  pl.BlockSpec((1, HEAD_ROWS, p_len), lambda b: (b, 0, 0)),
                  pl.BlockSpec((1, HEAD_ROWS, seq_len), lambda b: (b, 0, 0)),
                  row(LANES), _resident(tric.shape), _resident(trin.shape)],
        out_specs=row(d_fox),
        out_shape=jax.ShapeDtypeStruct((m, d_fox), BF16),
        compiler_params=_cparams(("arbitrary",)),
        name="fox_sample",
    )(fq, kc, vc, kn, vn, lfc_row, lfn_row, c_col, tric, trin)


def _log_gamma(h):
    return float(np.log1p(-np.exp2(np.float32(-5.0 - h)), dtype=np.float32))


def _retention_kernel(q_ref, k_ref, v_ref, g_ref, s0_ref, gn_ref, o_ref, s_ref, *, n_heads, t_blk):
    step = pl.program_id(1)

    @pl.when(step == 0)
    def _():
        s_ref[...] = s0_ref[...]

    ti = lax.broadcasted_iota(jnp.int32, (t_blk, t_blk), 0)
    si = lax.broadcasted_iota(jnp.int32, (t_blk, t_blk), 1)
    dist = jnp.abs(ti - si).astype(F32)
    shift = CHUNK.bit_length() - 1
    visible = jnp.right_shift(si, shift) <= jnp.right_shift(ti, shift)
    tcol = lax.broadcasted_iota(jnp.int32, (t_blk, 1), 0).astype(F32)
    for h in range(n_heads):
        lg = _log_gamma(h)
        sl = slice(h * HEAD_DIM, (h + 1) * HEAD_DIM)
        q = q_ref[:, sl]
        k = k_ref[:, sl]
        v = v_ref[:, sl]
        decay = jnp.where(visible, jnp.exp(lg * dist), 0.0)
        a = _dot_nt(q, k) * decay
        state = s_ref[0, h]
        o = _dot(a.astype(BF16), v) + _dot(q, state.astype(BF16)) * jnp.exp(lg * (tcol + 1.0))
        kd = (k.astype(F32) * jnp.exp(lg * (t_blk - 1.0 - tcol))).astype(BF16)
        s_ref[0, h] = math.exp(lg * t_blk) * state + _dot_tn(kd, v)
        mu = jnp.mean(o, axis=-1, keepdims=True)
        oc = o - mu
        var = jnp.mean(oc * oc, axis=-1, keepdims=True)
        rn = oc * lax.rsqrt(var + GN_EPS) * gn_ref[:, sl]
        gate = g_ref[:, sl].astype(F32)
        o_ref[:, sl] = (gate / (1.0 + jnp.exp(-gate)) * rn).astype(BF16)


def _retention(rq, rk, rv, rg, s0, gn_g, *, t_blk, seq_len, n_heads):
    m, d_ret = rq.shape
    nb = m // seq_len
    steps = seq_len // t_blk
    row = lambda: pl.BlockSpec((t_blk, d_ret), lambda b, s: (b * steps + s, 0))
    st = lambda: pl.BlockSpec((1, n_heads, HEAD_DIM, HEAD_DIM), lambda b, s: (b, 0, 0, 0))
    return pl.pallas_call(
        functools.partial(_retention_kernel, n_heads=n_heads, t_blk=t_blk),
        grid=(nb, steps),
        in_specs=[row(), row(), row(), row(), st(), pl.BlockSpec((1, d_ret), lambda b, s: (0, 0))],
        out_specs=(row(), st()),
        out_shape=(jax.ShapeDtypeStruct((m, d_ret), BF16),
                   jax.ShapeDtypeStruct((nb, n_heads, HEAD_DIM, HEAD_DIM), F32)),
        compiler_params=_cparams(("arbitrary", "arbitrary")),
        name="retention",
    )(rq, rk, rv, rg, s0, gn_g)


def _mix_out_kernel(x_ref, fo_ref, ro_ref, mo_ref, wo_ref, g_ref, b_ref, o_ref, cat_ref, *, alpha):
    d_fox = fo_ref.shape[1]
    d_ret = ro_ref.shape[1]
    cat_ref[:, 0:d_fox] = fo_ref[...]
    cat_ref[:, d_fox:d_fox + d_ret] = ro_ref[...]
    cat_ref[:, d_fox + d_ret:] = mo_ref[...]
    z = alpha * x_ref[...] + _dot(cat_ref[...], wo_ref[...])
    o_ref[...] = _layernorm(z, g_ref[...], b_ref[...])


def _mix_out(x, fo, ro, mo, wo, g, b, *, tm, alpha):
    m, d_model = x.shape
    row = lambda w: pl.BlockSpec((tm, w), lambda i: (i, 0))
    return pl.pallas_call(
        functools.partial(_mix_out_kernel, alpha=alpha),
        grid=(m // tm,),
        in_specs=[row(d_model), row(fo.shape[1]), row(ro.shape[1]), row(mo.shape[1]),
                  _resident(wo.shape), _resident(g.shape), _resident(b.shape)],
        out_specs=row(d_model),
        out_shape=jax.ShapeDtypeStruct((m, d_model), F32),
        scratch_shapes=[pltpu.VMEM((tm, wo.shape[0]), BF16)],
        compiler_params=_cparams(("arbitrary",)),
        name="mix_out",
    )(x, fo, ro, mo, wo, g, b)


def _gelu_tanh(x):
    c = math.sqrt(2.0 / math.pi)
    return x * (0.5 * (1.0 + jnp.tanh(c * (x + 0.044715 * (x * x * x)))))


def _conv_ffn_kernel(h_ref, wg_ref, wu_ref, wd_ref, cw_ref, cb_ref, past_ref, g_ref, b_ref,
                     y_ref, cv_ref, hb_ref, gbuf_ref, carry_ref, *, alpha, seq_len, tm, tc):
    i = pl.program_id(0)
    f = pl.program_id(1)
    nf = pl.num_programs(1)
    lt = min(seq_len, tm)
    ns = tm // lt
    halo = 8
    carried = seq_len > tm

    @pl.when(f == 0)
    def _():
        h = h_ref[...]
        hb_ref[...] = h.astype(BF16)
        y_ref[...] = alpha * h

    if carried:
        @pl.when(jnp.logical_and(i == 0, f == 0))
        def _():
            carry_ref[...] = jnp.zeros_like(carry_ref)

        tiles_per_seq = seq_len // tm
        gbuf_ref[0, halo - 2:halo, :] = jnp.where(i % tiles_per_seq == 0, past_ref[i // tiles_per_seq],
                                                  carry_ref[f])
    else:
        for s in range(ns):
            gbuf_ref[s, halo - 2:halo, :] = past_ref[i * ns + s]

    hb = hb_ref[...]
    hid = []
    for c0 in range(0, wg_ref.shape[1], tc):
        cs = slice(c0, c0 + tc)
        gate = _dot(hb, wg_ref[:, cs])
        up = _dot(hb, wu_ref[:, cs])
        w0 = cw_ref[0:1, cs]
        w1 = cw_ref[1:2, cs]
        w2 = cw_ref[2:3, cs]
        cb = cb_ref[:, cs]
        parts = []
        for s in range(ns):
            gs = gate[s * lt:(s + 1) * lt]
            gbuf_ref[s, halo:halo + lt, cs] = gs
            g1 = gbuf_ref[s, halo - 1:halo - 1 + lt, cs]
            g2 = gbuf_ref[s, halo - 2:halo - 2 + lt, cs]
            gc = cb + w0 * g2 + w1 * g1 + w2 * gs
            parts.append((_gelu_tanh(gc) * up[s * lt:(s + 1) * lt]).astype(BF16))
        hid.append(parts[0] if ns == 1 else jnp.concatenate(parts, axis=0))
    hid = hid[0] if len(hid) == 1 else jnp.concatenate(hid, axis=1)
    y_ref[...] += _dot(hid, wd_ref[...])

    for s in range(ns):
        last2 = gbuf_ref[s, halo + lt - 2:halo + lt, :]
        cv_ref[s] = last2
        if carried:
            carry_ref[f] = last2

    @pl.when(f == nf - 1)
    def _():
        y_ref[...] = _layernorm(y_ref[...], g_ref[...], b_ref[...])


def _conv_ffn(h1, wg, wu, wd, conv_w, conv_b, past, g, b, *, tm, tf, seq_len, alpha):
    tc = 2 * LANES
    assert tf % tc == 0
    m, d_model = h1.shape
    d_ff = wg.shape[1]
    n_seq = past.shape[0]
    assert m % tm == 0 and d_ff % tf == 0 and (seq_len % tm == 0 or tm % seq_len == 0)
    nf = d_ff // tf
    lt = min(seq_len, tm)
    y, cv_all = pl.pallas_call(
        functools.partial(_conv_ffn_kernel, alpha=alpha, seq_len=seq_len, tm=tm, tc=tc),
        grid=(m // tm, nf),
        in_specs=[pl.BlockSpec((tm, d_model), lambda i, f: (i, 0)),
                  pl.BlockSpec((d_model, tf), lambda i, f: (0, f)),
                  pl.BlockSpec((d_model, tf), lambda i, f: (0, f)),
                  pl.BlockSpec((tf, d_model), lambda i, f: (f, 0)),
                  pl.BlockSpec((CONV_W, tf), lambda i, f: (0, f)),
                  pl.BlockSpec((1, tf), lambda i, f: (0, f)),
                  pl.BlockSpec((n_seq, CONV_W - 1, tf), lambda i, f: (0, 0, f)),
                  pl.BlockSpec((1, d_model), lambda i, f: (0, 0)),
                  pl.BlockSpec((1, d_model), lambda i, f: (0, 0))],
        out_specs=(pl.BlockSpec((tm, d_model), lambda i, f: (i, 0)),
                   pl.BlockSpec((tm // lt, CONV_W - 1, tf), lambda i, f: (i, 0, f))),
        out_shape=(jax.ShapeDtypeStruct((m, d_model), F32),
                   jax.ShapeDtypeStruct((m // lt, CONV_W - 1, d_ff), F32)),
        scratch_shapes=[pltpu.VMEM((tm, d_model), BF16),
                        pltpu.VMEM((tm // lt, 8 + lt, tf), F32),
                        pltpu.VMEM((nf, CONV_W - 1, tf), F32)],
        compiler_params=_cparams(("arbitrary", "arbitrary")),
        name="conv_ffn",
    )(h1, wg, wu, wd, conv_w, conv_b, past, g, b)
    return y, cv_all.reshape(n_seq, seq_len // lt, CONV_W - 1, d_ff)[:, -1]


def _rope_tables(pos):
    half = HEAD_DIM // 2
    inv = ROPE_BASE ** (-jnp.arange(half, dtype=F32) / half)
    ang = pos.astype(F32)[:, None] * inv[None, :]
    cos = jnp.cos(ang)
    sin = jnp.sin(ang)
    return jnp.concatenate([cos, cos], axis=-1), jnp.concatenate([-sin, sin], axis=-1)


def _layer(x, pos, seq_len, mk, mv, w, *, alpha, tm_proj, tm_mix, tm_ffn, tf, t_ret, s0, conv_past,
           fox_fn):
    cos, sin = _rope_tables(pos)
    fq, fk, fv, lf, c, rq, rk, rv, rg, mo = _in_proj(
        x, w["wa"], w["wf"], w["bf"], cos, sin, mk, mv, tm=tm_proj, seq_len=seq_len,
        d_fox=w["d_fox"], d_ret=w["d_ret"], d_memq=w["d_memq"])
    fo = fox_fn(fq, fk, fv, lf, c)
    ro, s_new = _retention(rq, rk, rv, rg, s0, w["gn_g"], t_blk=t_ret, seq_len=seq_len,
                           n_heads=w["d_ret"] // HEAD_DIM)
    h1 = _mix_out(x, fo, ro, mo, w["wo"], w["ln1_g"], w["ln1_b"], tm=tm_mix, alpha=alpha)
    y, cv = _conv_ffn(h1, w["wg"], w["wu"], w["wd"], w["conv_w"], w["conv_b"], conv_past,
                      w["ln2_g"], w["ln2_b"], tm=tm_ffn, tf=tf, seq_len=seq_len, alpha=alpha)
    return y, fk, fv, lf, s_new, cv


def kernel(x_prompt, x_sample, cache_fox_k, cache_fox_v, cache_fox_logf, state_ret, cache_mem_k,
           cache_mem_v, state_conv, mem_prompt, w_in, b_f, w_mem_kv, ret_gn_g, w_o, ln1_g, ln1_b,
           w_gate, w_up, conv_w, conv_b, w_down, ln2_g, ln2_b):
    batch, seq, d_model = x_prompt.shape
    dec_batch, dec_seq, _ = x_sample.shape
    depth = w_in.shape[0]
    past_len = cache_fox_k.shape[2]
    n_fox = cache_fox_k.shape[3]
    n_ret = state_ret.shape[2]
    n_mem = cache_mem_k.shape[3]
    n_mem_tok = cache_mem_k.shape[2]
    d_fox, d_ret, d_memq = n_fox * HEAD_DIM, n_ret * HEAD_DIM, n_mem * HEAD_DIM
    d_ff = w_gate.shape[2]
    assert batch == 1 and depth == 1 and n_fox <= 8
    alpha = (2.0 * depth) ** 0.25
    l = 0

    o_ff = 3 * d_fox
    w_in_l = w_in[l]
    wa = jnp.concatenate([w_in_l[:, :o_ff].astype(BF16), w_in_l[:, o_ff + n_fox:].astype(BF16)], axis=1)
    wf = jnp.pad(w_in_l[:, o_ff:o_ff + n_fox], ((0, 0), (0, LANES - n_fox))).astype(BF16)
    bfp = jnp.pad(b_f[l], (0, LANES - n_fox)).reshape(1, LANES)
    w = dict(wa=wa, wf=wf, bf=bfp, d_fox=d_fox, d_ret=d_ret, d_memq=d_memq,
             gn_g=ret_gn_g[l].reshape(1, d_ret), wo=w_o[l].astype(BF16),
             ln1_g=ln1_g[l].reshape(1, d_model), ln1_b=ln1_b[l].reshape(1, d_model),
             wg=w_gate[l].astype(BF16), wu=w_up[l].astype(BF16), wd=w_down[l].astype(BF16),
             conv_w=conv_w[l], conv_b=conv_b[l].reshape(1, d_ff),
             ln2_g=ln2_g[l].reshape(1, d_model), ln2_b=ln2_b[l].reshape(1, d_model))

    mkv = _mem_kv(mem_prompt.reshape(n_mem_tok, d_model), w_mem_kv[l].astype(BF16))
    mk_p, mv_p = mkv[:, :d_memq], mkv[:, d_memq:]

    def fox_prompt_fn(fq, fk, fv, lf, c):
        return _fox_prompt(fq, fk, fv, c, tb=512, nsub=2, n_heads=n_fox)

    y_p, fk_p, fv_p, lf_p, s_p, cv_p = _layer(
        x_prompt.reshape(seq, d_model), jnp.arange(seq), seq, mk_p[None], mv_p[None], w,
        alpha=alpha, tm_proj=256, tm_mix=512, tm_ffn=512, tf=512, t_ret=256,
        s0=jnp.zeros((1, n_ret, HEAD_DIM, HEAD_DIM), F32),
        conv_past=jnp.zeros((1, CONV_W - 1, d_ff), F32), fox_fn=fox_prompt_fn)

    m_s = dec_batch * dec_seq
    kc = cache_fox_k[l].reshape(dec_batch, past_len, d_fox)
    vc = cache_fox_v[l].reshape(dec_batch, past_len, d_fox)
    lfc_row = jnp.pad(jnp.swapaxes(cache_fox_logf[l], 1, 2), ((0, 0), (0, HEAD_ROWS - n_fox), (0, 0)))

    def fox_sample_fn(fq, fk, fv, lf, c):
        lfn_row = jnp.swapaxes(lf.reshape(dec_batch, dec_seq, LANES)[:, :, :HEAD_ROWS], 1, 2)
        return _fox_sample(fq, kc, vc, fk, fv, lfc_row, lfn_row, c, n_heads=n_fox, seq_len=dec_seq)

    y_s, fk_s, fv_s, lf_s, s_s, cv_s = _layer(
        x_sample.reshape(m_s, d_model), jnp.tile(past_len + jnp.arange(dec_seq), dec_batch), dec_seq,
        cache_mem_k[l].reshape(dec_batch, n_mem_tok, d_memq),
        cache_mem_v[l].reshape(dec_batch, n_mem_tok, d_memq), w,
        alpha=alpha, tm_proj=dec_seq, tm_mix=m_s, tm_ffn=m_s, tf=512, t_ret=dec_seq,
        s0=state_ret[l], conv_past=state_conv[l], fox_fn=fox_sample_fn)

    return (y_p.reshape(batch, seq, d_model), y_s.reshape(dec_batch, dec_seq, d_model),
            fk_p.reshape(1, batch, seq, n_fox, HEAD_DIM), fv_p.reshape(1, batch, seq, n_fox, HEAD_DIM),
            lf_p[:, :n_fox].reshape(1, batch, seq, n_fox),
            s_p.reshape(1, batch, n_ret, HEAD_DIM, HEAD_DIM),
            mk_p.reshape(1, batch, n_mem_tok, n_mem, HEAD_DIM),
            mv_p.reshape(1, batch, n_mem_tok, n_mem, HEAD_DIM),
            cv_p.reshape(1, batch, CONV_W - 1, d_ff),
            fk_s.reshape(1, dec_batch, dec_seq, n_fox, HEAD_DIM),
            fv_s.reshape(1, dec_batch, dec_seq, n_fox, HEAD_DIM),
            lf_s[:, :n_fox].reshape(1, dec_batch, dec_seq, n_fox),
            s_s.reshape(1, dec_batch, n_ret, HEAD_DIM, HEAD_DIM),
            cv_s.reshape(1, dec_batch, CONV_W - 1, d_ff))
```

```python
import functools
import math

import numpy as np
import jax
import jax.numpy as jnp
from jax import lax
from jax.experimental import pallas as pl
from jax.experimental.pallas import tpu as pltpu

F32 = jnp.float32
BF16 = jnp.bfloat16

HEAD_DIM = 128
CHUNK = 64
CONV_W = 3
ROPE_BASE = 10000.0
LN_EPS = 1e-5
GN_EPS = 1e-5
LANES = 128
V7X_VMEM_BYTES = 64 * 1024 * 1024
VMEM_LIMIT = V7X_VMEM_BYTES - 8 * 1024 * 1024
SCALE = HEAD_DIM ** -0.5
LOG2E = math.log2(math.e)
NEG_INF = float("-inf")
HEAD_ROWS = 16


def _cparams(sem):
    return pltpu.CompilerParams(dimension_semantics=sem, vmem_limit_bytes=VMEM_LIMIT)


def _resident(shape):
    nd = len(shape)
    return pl.BlockSpec(shape, lambda *_: (0,) * nd, pipeline_mode=pl.Buffered(1))


def _dot(a, b):
    return jnp.dot(a, b, preferred_element_type=F32)


def _dot_nt(a, b):
    return lax.dot_general(a, b, (((1,), (1,)), ((), ())), preferred_element_type=F32)


def _dot_tn(a, b):
    return lax.dot_general(a, b, (((0,), (0,)), ((), ())), preferred_element_type=F32)


def _split3_dot(a, tri, *, tri_first):
    hi = a.astype(BF16)
    r1 = a - hi.astype(F32)
    mid = r1.astype(BF16)
    lo = (r1 - mid.astype(F32)).astype(BF16)
    if tri_first:
        return _dot(tri, hi) + _dot(tri, mid) + _dot(tri, lo)
    return _dot(hi, tri) + _dot(mid, tri) + _dot(lo, tri)


def _layernorm(z, g, b):
    mu = jnp.mean(z, axis=-1, keepdims=True)
    zc = z - mu
    var = jnp.mean(zc * zc, axis=-1, keepdims=True)
    return zc * lax.rsqrt(var + LN_EPS) * g + b


def _memkv_kernel(m_ref, w_ref, o_ref):
    kv = _dot(m_ref[...].astype(BF16), w_ref[...])
    for j in range(o_ref.shape[0]):
        o_ref[j] = kv[:, j * HEAD_DIM:(j + 1) * HEAD_DIM]


def _mem_kv(mem, w_bf):
    m, _ = mem.shape
    return pl.pallas_call(
        _memkv_kernel,
        out_shape=jax.ShapeDtypeStruct((w_bf.shape[1] // HEAD_DIM, m, HEAD_DIM), F32),
        compiler_params=pltpu.CompilerParams(vmem_limit_bytes=VMEM_LIMIT),
        name="mem_kv",
    )(mem, w_bf)


def _w_prep_kernel(w_ref, wa_ref, wf_ref, *, o_ff, n_fox):
    wa_ref[:, 0:o_ff] = w_ref[:, 0:o_ff].astype(BF16)
    wa_ref[:, o_ff:] = w_ref[:, o_ff + n_fox:].astype(BF16)
    lane = lax.broadcasted_iota(jnp.int32, wf_ref.shape, 1)
    wf_ref[...] = jnp.where(lane < n_fox, w_ref[:, o_ff:o_ff + LANES], 0.0).astype(BF16)


def _w_prep(w_in, *, o_ff, n_fox, tr):
    d_model, d_in = w_in.shape
    assert d_model % tr == 0 and o_ff % LANES == 0 and (d_in - n_fox) % LANES == 0
    return pl.pallas_call(
        functools.partial(_w_prep_kernel, o_ff=o_ff, n_fox=n_fox),
        grid=(d_model // tr,),
        in_specs=[pl.BlockSpec((tr, d_in), lambda i: (i, 0))],
        out_specs=(pl.BlockSpec((tr, d_in - n_fox), lambda i: (i, 0)),
                   pl.BlockSpec((tr, LANES), lambda i: (i, 0))),
        out_shape=(jax.ShapeDtypeStruct((d_model, d_in - n_fox), BF16),
                   jax.ShapeDtypeStruct((d_model, LANES), BF16)),
        compiler_params=_cparams(("arbitrary",)),
        name="w_prep",
    )(w_in)


def _in_proj_kernel(x_ref, wa_ref, wf_ref, bf_ref, cos_ref, sin_ref, mk_ref, mv_ref, tri_ref,
                    fq_ref, fk_ref, fv_ref, lf_ref, c_ref, rq_ref, rk_ref, rv_ref, rg_ref, mo_ref,
                    carry_ref, *, d_fox, d_ret, d_memq, tiles_per_seq):
    i = pl.program_id(0)
    xb = x_ref[...].astype(BF16)

    def proj(lo, width):
        return _dot(xb, wa_ref[:, lo:lo + width])

    o_fk = d_fox
    o_fv = 2 * d_fox
    o_rq = 3 * d_fox
    o_rk = o_rq + d_ret
    o_rv = o_rk + d_ret
    o_rg = o_rv + d_ret
    o_mq = o_rg + d_ret

    fq_ref[...] = (proj(0, d_fox) * (SCALE * LOG2E)).astype(BF16)
    zk = proj(o_fk, d_fox)
    zv = proj(o_fv, d_fox)
    for h in range(d_fox // HEAD_DIM):
        sl = slice(h * HEAD_DIM, (h + 1) * HEAD_DIM)
        fk_ref[0, h] = zk[:, sl]
        fv_ref[0, h] = zv[:, sl]

    z = _dot(xb, wf_ref[...]) + bf_ref[...]
    lf = jnp.minimum(z, 0.0) - jnp.log1p(jnp.exp(-jnp.abs(z)))
    lf_ref[...] = lf

    @pl.when(i % tiles_per_seq == 0)
    def _():
        carry_ref[...] = jnp.zeros_like(carry_ref)

    c = _split3_dot(lf, tri_ref[...], tri_first=True) + carry_ref[0:1, :]
    c_ref[...] = c
    tm = c.shape[0]
    carry_ref[...] = jnp.broadcast_to(c[tm - 1:tm, :], carry_ref.shape)

    cos = cos_ref[...]
    sin = sin_ref[...]

    def rope(t):
        return t * cos + pltpu.roll(t, HEAD_DIM // 2, axis=1) * sin

    zq = proj(o_rq, d_ret)
    zk = proj(o_rk, d_ret)
    for h in range(d_ret // HEAD_DIM):
        sl = slice(h * HEAD_DIM, (h + 1) * HEAD_DIM)
        rq_ref[:, sl] = rope(zq[:, sl]).astype(BF16)
        rk_ref[:, sl] = (rope(zk[:, sl]) * SCALE).astype(BF16)
    rv_ref[...] = proj(o_rv, d_ret).astype(BF16)
    rg_ref[...] = proj(o_rg, d_ret).astype(BF16)

    zm = proj(o_mq, d_memq)
    for h in range(d_memq // HEAD_DIM):
        sl = slice(h * HEAD_DIM, (h + 1) * HEAD_DIM)
        q = (zm[:, sl] * SCALE).astype(BF16)
        s = _dot_nt(q, mk_ref[0, h].astype(BF16))
        p = jnp.exp(s - jnp.max(s, axis=-1, keepdims=True))
        l = jnp.sum(p, axis=-1, keepdims=True)
        o = _dot(p.astype(BF16), mv_ref[0, h].astype(BF16))
        mo_ref[:, sl] = (o / l).astype(BF16)


def _in_proj(x, wa, wf, bfp, cos, sin, mk, mv, *, tm, seq_len, d_fox, d_ret, d_memq):
    m, d_model = x.shape
    assert m % tm == 0 and seq_len % tm == 0
    tiles_per_seq = seq_len // tm
    n_fox = d_fox // HEAD_DIM
    tri = (np.arange(tm)[:, None] >= np.arange(tm)[None, :])
    tri = jnp.asarray(tri, dtype=BF16)
    row = lambda w: pl.BlockSpec((tm, w), lambda i: (i, 0))
    memspec = pl.BlockSpec((1,) + mk.shape[1:], lambda i: (i // tiles_per_seq, 0, 0, 0))
    kvspec = pl.BlockSpec((1, n_fox, tm, HEAD_DIM), lambda i: (i // tiles_per_seq, 0, i % tiles_per_seq, 0))
    kern = functools.partial(_in_proj_kernel, d_fox=d_fox, d_ret=d_ret, d_memq=d_memq,
                             tiles_per_seq=tiles_per_seq)
    out_shape = (
        jax.ShapeDtypeStruct((m, d_fox), BF16),
        jax.ShapeDtypeStruct((m // seq_len, n_fox, seq_len, HEAD_DIM), F32),
        jax.ShapeDtypeStruct((m // seq_len, n_fox, seq_len, HEAD_DIM), F32),
        jax.ShapeDtypeStruct((m, LANES), F32),
        jax.ShapeDtypeStruct((m, LANES), F32),
        jax.ShapeDtypeStruct((m, d_ret), BF16),
        jax.ShapeDtypeStruct((m, d_ret), BF16),
        jax.ShapeDtypeStruct((m, d_ret), BF16),
        jax.ShapeDtypeStruct((m, d_ret), BF16),
        jax.ShapeDtypeStruct((m, d_memq), BF16),
    )
    return pl.pallas_call(
        kern,
        grid=(m // tm,),
        in_specs=[row(d_model), _resident(wa.shape), _resident(wf.shape), _resident(bfp.shape),
                  row(HEAD_DIM), row(HEAD_DIM), memspec, memspec, _resident(tri.shape)],
        out_specs=(row(d_fox), kvspec, kvspec, row(LANES), row(LANES),
                   row(d_ret), row(d_ret), row(d_ret), row(d_ret), row(d_memq)),
        out_shape=out_shape,
        scratch_shapes=[pltpu.VMEM((8, LANES), F32)],
        compiler_params=_cparams(("arbitrary",)),
        name="in_proj",
    )(x, wa, wf, bfp, cos, sin, mk, mv, tri)


def _split3_f32(a):
    hi = a.astype(BF16).astype(F32)
    r1 = a - hi
    mid = r1.astype(BF16).astype(F32)
    lo = (r1 - mid).astype(BF16).astype(F32)
    return hi, mid, lo


def _fox_prompt_kernel(q_ref, k_ref, v_ref, ccol_ref, o_ref, qa, kbuf, vbuf, cstart, acc, sbuf, *, tb,
                       nsub):
    h = pl.program_id(0)
    qi = pl.program_id(1)
    tq = tb * nsub
    row0 = pl.multiple_of(qi * tq, tq)
    lane = lax.broadcasted_iota(jnp.int32, (tq, LANES), 1)
    ch = jnp.sum(jnp.where(lane == h, ccol_ref[...], 0.0), axis=1, keepdims=True)
    rel = []
    for s in range(nsub):
        blk = ch[s * tb:(s + 1) * tb]
        start = blk[0:1]
        cstart[nsub * qi + s] = jnp.broadcast_to(start, (8, LANES))
        rel.append((blk - start) * LOG2E)
    rel = rel[0] if nsub == 1 else jnp.concatenate(rel, axis=0)
    hi, mid, lo = _split3_f32(rel)
    qbias = jnp.where(lane == 0, hi, jnp.where(lane == 1, mid, jnp.where(lane == 2, lo,
                      jnp.where(lane < 6, 1.0, 0.0))))
    kbias = jnp.where(lane < 3, 1.0, jnp.where(lane == 3, -hi, jnp.where(lane == 4, -mid,
                      jnp.where(lane == 5, -lo, 0.0))))
    qa[:, 0:HEAD_DIM] = q_ref[...]
    qa[:, HEAD_DIM:] = qbias.astype(BF16)
    kbuf[pl.ds(row0, tq), 0:HEAD_DIM] = k_ref[...].astype(BF16)
    kbuf[pl.ds(row0, tq), HEAD_DIM:] = kbias.astype(BF16)
    vbuf[pl.ds(row0, tq), 0:HEAD_DIM] = v_ref[...].astype(BF16)
    vbuf[pl.ds(row0, tq), HEAD_DIM:] = jnp.where(lane == 0, 1.0, 0.0).astype(BF16)
    acc[...] = jnp.zeros_like(acc)

    r_id = lax.broadcasted_iota(jnp.int32, (tb, LANES), 0)
    c_id = lax.broadcasted_iota(jnp.int32, (tb, LANES), 1)
    n_ct = tb // LANES
    subs = range(nsub)

    def scores(s, kj):
        col0 = pl.multiple_of(kj * tb, tb)
        return _dot_nt(qa[s * tb:(s + 1) * tb, :], kbuf[pl.ds(col0, tb), :])

    def softmax_pv(s, kj, s2, m, masked):
        col0 = pl.multiple_of(kj * tb, tb)
        tiles = [s2[:, t * LANES:(t + 1) * LANES] for t in range(n_ct)]
        if masked:
            tiles = [jnp.where(c_id + t * LANES <= r_id, tiles[t], NEG_INF) for t in range(n_ct)]
        delta = (cstart[nsub * qi + s][0:1, :] - cstart[kj][0:1, :]) * LOG2E
        m_rel = m - delta
        tmax = functools.reduce(jnp.maximum, tiles)
        m_new = jnp.maximum(m_rel, jnp.max(tmax, axis=-1, keepdims=True))
        alpha = jnp.exp2(m_rel - m_new)
        p = jnp.concatenate([jnp.exp2(t - m_new).astype(BF16) for t in tiles], axis=1)
        pv = _dot(p, vbuf[pl.ds(col0, tb), :])
        a = acc[s]
        acc[s] = jnp.concatenate([alpha * a[:, 0:LANES], alpha * a[:, LANES:]], axis=1) + pv
        return m_new + delta

    for s in subs:
        sbuf[s] = scores(s, 0)

    def body(t, ms):
        kj = 2 * t
        s_a = [sbuf[s] for s in subs]
        s_b = [scores(s, kj + 1) for s in subs]
        ms = [softmax_pv(s, kj, s_a[s], ms[s], False) for s in subs]
        s_c = [scores(s, kj + 2) for s in subs]
        ms = [softmax_pv(s, kj + 1, s_b[s], ms[s], False) for s in subs]
        for s in subs:
            sbuf[s] = s_c[s]
        return tuple(ms)

    assert nsub % 2 == 0
    ms = lax.fori_loop(0, (nsub // 2) * qi, body,
                       tuple(jnp.full((tb, LANES), NEG_INF, F32) for _ in subs))
    for s in subs:
        m = ms[s]
        for t in range(s + 1):
            s2 = sbuf[s] if t == 0 else scores(s, nsub * qi + t)
            m = softmax_pv(s, nsub * qi + t, s2, m, t == s)
        a = acc[s]
        o_ref[s * tb:(s + 1) * tb, :] = (a[:, 0:HEAD_DIM] / a[:, HEAD_DIM:HEAD_DIM + 1]).astype(BF16)


def _fox_prompt(fq, fk, fv, c_col, *, tb, nsub, n_heads):
    s, _ = fq.shape
    tq = tb * nsub
    assert s % tq == 0
    blk = lambda: pl.BlockSpec((tq, HEAD_DIM), lambda h, qi: (qi, h))
    kv = lambda: pl.BlockSpec((None, None, tq, HEAD_DIM), lambda h, qi: (0, h, qi, 0))
    return pl.pallas_call(
        functools.partial(_fox_prompt_kernel, tb=tb, nsub=nsub),
        grid=(n_heads, s // tq),
        in_specs=[blk(), kv(), kv(), pl.BlockSpec((tq, LANES), lambda h, qi: (qi, 0))],
        out_specs=blk(),
        out_shape=jax.ShapeDtypeStruct(fq.shape, BF16),
        scratch_shapes=[pltpu.VMEM((tq, 2 * HEAD_DIM), BF16),
                        pltpu.VMEM((s, 2 * HEAD_DIM), BF16),
                        pltpu.VMEM((s, 2 * HEAD_DIM), BF16),
                        pltpu.VMEM((s // tb, 8, LANES), F32),
                        pltpu.VMEM((nsub, tb, 2 * HEAD_DIM), F32),
                        pltpu.VMEM((nsub, tb, tb), F32)],
        compiler_params=_cparams(("arbitrary", "arbitrary")),
        name="fox_prompt",
    )(fq, fk, fv, c_col)


def _fox_sample_kernel(q_ref, kc_ref, vc_ref, kn_ref, vn_ref, lfc_ref, lfn_ref, ccol_ref,
                       tric_ref, trin_ref, o_ref, *, n_heads):
    c_cache = _split3_dot(lfc_ref[0], tric_ref[...], tri_first=False)
    p_len = c_cache.shape[1]
    ck_cache = c_cache - c_cache[:, p_len - 1:p_len]
    ck_new = _split3_dot(lfn_ref[0], trin_ref[...], tri_first=False)
    ccol = ccol_ref[...]
    n_new = ck_new.shape[1]
    r = lax.broadcasted_iota(jnp.int32, (n_new, n_new), 0)
    c = lax.broadcasted_iota(jnp.int32, (n_new, n_new), 1)
    for h in range(n_heads):
        sl = slice(h * HEAD_DIM, (h + 1) * HEAD_DIM)
        q = q_ref[:, sl]
        cq = ccol[:, h:h + 1]
        s_c = _dot_nt(q, kc_ref[0, h].astype(BF16)) + (cq - ck_cache[h:h + 1, :]) * LOG2E
        s_n = _dot_nt(q, kn_ref[0, h].astype(BF16)) + (cq - ck_new[h:h + 1, :]) * LOG2E
        s_n = jnp.where(c <= r, s_n, NEG_INF)
        m = jnp.maximum(jnp.max(s_c, axis=-1, keepdims=True), jnp.max(s_n, axis=-1, keepdims=True))
        p_c = jnp.exp2(s_c - m)
        p_n = jnp.exp2(s_n - m)
        l = jnp.sum(p_c, axis=-1, keepdims=True) + jnp.sum(p_n, axis=-1, keepdims=True)
        o = _dot(p_c.astype(BF16), vc_ref[0, h].astype(BF16)) + \
            _dot(p_n.astype(BF16), vn_ref[0, h].astype(BF16))
        o_ref[:, sl] = (o / l).astype(BF16)


def _fox_sample(fq, kc, vc, kn, vn, lfc_row, lfn_row, c_col, *, n_heads, seq_len):
    m, d_fox = fq.shape
    nb, _, p_len, _ = kc.shape
    tric = jnp.asarray(np.arange(p_len)[:, None] <= np.arange(p_len)[None, :], dtype=BF16)
    trin = jnp.asarray(np.arange(seq_len)[:, None] <= np.arange(seq_len)[None, :], dtype=BF16)
    row = lambda w: pl.BlockSpec((seq_len, w), lambda b: (b, 0))
    cache = pl.BlockSpec((1, n_heads, p_len, HEAD_DIM), lambda b: (b, 0, 0, 0))
    new = pl.BlockSpec((1, n_heads, seq_len, HEAD_DIM), lambda b: (b, 0, 0, 0))
    return pl.pallas_call(
        functools.partial(_fox_sample_kernel, n_heads=n_heads),
        grid=(nb,),
        in_specs=[row(d_fox), cache, cache, new, new,
                  pl.BlockSpec((1, HEAD_ROWS, p_len), lambda b: (b, 0, 0)),
                  pl.BlockSpec((1, HEAD_ROWS, seq_len), lambda b: (b, 0, 0)),
                  row(LANES), _resident(tric.shape), _resident(trin.shape)],
        out_specs=row(d_fox),
        out_shape=jax.ShapeDtypeStruct((m, d_fox), BF16),
        compiler_params=_cparams(("arbitrary",)),
        name="fox_sample",
    )(fq, kc, vc, kn, vn, lfc_row, lfn_row, c_col, tric, trin)


def _log_gamma(h):
    return float(np.log1p(-np.exp2(np.float32(-5.0 - h)), dtype=np.float32))


def _retention_kernel(q_ref, k_ref, v_ref, g_ref, s0_ref, gn_ref, o_ref, s_ref, *, n_heads, t_blk):
    step = pl.program_id(1)

    @pl.when(step == 0)
    def _():
        s_ref[...] = s0_ref[...]

    ti = lax.broadcasted_iota(jnp.int32, (t_blk, t_blk), 0)
    si = lax.broadcasted_iota(jnp.int32, (t_blk, t_blk), 1)
    dist = jnp.abs(ti - si).astype(F32)
    shift = CHUNK.bit_length() - 1
    visible = jnp.right_shift(si, shift) <= jnp.right_shift(ti, shift)
    tcol = lax.broadcasted_iota(jnp.int32, (t_blk, 1), 0).astype(F32)
    for h in range(n_heads):
        lg = _log_gamma(h)
        sl = slice(h * HEAD_DIM, (h + 1) * HEAD_DIM)
        q = q_ref[:, sl]
        k = k_ref[:, sl]
        v = v_ref[:, sl]
        decay = jnp.where(visible, jnp.exp(lg * dist), 0.0)
        a = _dot_nt(q, k) * decay
        state = s_ref[0, h]
        o = _dot(a.astype(BF16), v) + _dot(q, state.astype(BF16)) * jnp.exp(lg * (tcol + 1.0))
        kd = (k.astype(F32) * jnp.exp(lg * (t_blk - 1.0 - tcol))).astype(BF16)
        s_ref[0, h] = math.exp(lg * t_blk) * state + _dot_tn(kd, v)
        mu = jnp.mean(o, axis=-1, keepdims=True)
        oc = o - mu
        var = jnp.mean(oc * oc, axis=-1, keepdims=True)
        rn = oc * lax.rsqrt(var + GN_EPS) * gn_ref[:, sl]
        gate = g_ref[:, sl].astype(F32)
        o_ref[:, sl] = (gate / (1.0 + jnp.exp(-gate)) * rn).astype(BF16)


def _retention(rq, rk, rv, rg, s0, gn_g, *, t_blk, seq_len, n_heads):
    m, d_ret = rq.shape
    nb = m // seq_len
    steps = seq_len // t_blk
    row = lambda: pl.BlockSpec((t_blk, d_ret), lambda b, s: (b * steps + s, 0))
    st = lambda: pl.BlockSpec((1, n_heads, HEAD_DIM, HEAD_DIM), lambda b, s: (b, 0, 0, 0))
    return pl.pallas_call(
        functools.partial(_retention_kernel, n_heads=n_heads, t_blk=t_blk),
        grid=(nb, steps),
        in_specs=[row(), row(), row(), row(), st(), pl.BlockSpec((1, d_ret), lambda b, s: (0, 0))],
        out_specs=(row(), st()),
        out_shape=(jax.ShapeDtypeStruct((m, d_ret), BF16),
                   jax.ShapeDtypeStruct((nb, n_heads, HEAD_DIM, HEAD_DIM), F32)),
        compiler_params=_cparams(("arbitrary", "arbitrary")),
        name="retention",
    )(rq, rk, rv, rg, s0, gn_g)


def _mix_out_kernel(x_ref, fo_ref, ro_ref, mo_ref, wo_ref, g_ref, b_ref, o_ref, cat_ref, *, alpha):
    d_fox = fo_ref.shape[1]
    d_ret = ro_ref.shape[1]
    cat_ref[:, 0:d_fox] = fo_ref[...]
    cat_ref[:, d_fox:d_fox + d_ret] = ro_ref[...]
    cat_ref[:, d_fox + d_ret:] = mo_ref[...]
    z = alpha * x_ref[...] + _dot(cat_ref[...], wo_ref[...])
    o_ref[...] = _layernorm(z, g_ref[...], b_ref[...])


def _mix_out(x, fo, ro, mo, wo, g, b, *, tm, alpha):
    m, d_model = x.shape
    row = lambda w: pl.BlockSpec((tm, w), lambda i: (i, 0))
    return pl.pallas_call(
        functools.partial(_mix_out_kernel, alpha=alpha),
        grid=(m // tm,),
        in_specs=[row(d_model), row(fo.shape[1]), row(ro.shape[1]), row(mo.shape[1]),
                  _resident(wo.shape), _resident(g.shape), _resident(b.shape)],
        out_specs=row(d_model),
        out_shape=jax.ShapeDtypeStruct((m, d_model), F32),
        scratch_shapes=[pltpu.VMEM((tm, wo.shape[0]), BF16)],
        compiler_params=_cparams(("arbitrary",)),
        name="mix_out",
    )(x, fo, ro, mo, wo, g, b)


def _gelu_tanh(x):
    c = math.sqrt(2.0 / math.pi)
    return x * (0.5 * (1.0 + jnp.tanh(c * (x + 0.044715 * (x * x * x)))))


def _conv_ffn_kernel(h_ref, wg_ref, wu_ref, wd_ref, cw_ref, cb_ref, past_ref, g_ref, b_ref,
                     y_ref, cv_ref, hb_ref, gbuf_ref, carry_ref, *, alpha, seq_len, tm, tc):
    i = pl.program_id(0)
    f = pl.program_id(1)
    nf = pl.num_programs(1)
    lt = min(seq_len, tm)
    ns = tm // lt
    halo = 8
    carried = seq_len > tm

    @pl.when(f == 0)
    def _():
        h = h_ref[...]
        hb_ref[...] = h.astype(BF16)
        y_ref[...] = alpha * h

    if carried:
        @pl.when(jnp.logical_and(i == 0, f == 0))
        def _():
            carry_ref[...] = jnp.zeros_like(carry_ref)

        tiles_per_seq = seq_len // tm
        gbuf_ref[0, halo - 2:halo, :] = jnp.where(i % tiles_per_seq == 0, past_ref[i // tiles_per_seq],
                                                  carry_ref[f])
    else:
        for s in range(ns):
            gbuf_ref[s, halo - 2:halo, :] = past_ref[i * ns + s]

    hb = hb_ref[...]
    hid = []
    for c0 in range(0, wg_ref.shape[1], tc):
        cs = slice(c0, c0 + tc)
        gate = _dot(hb, wg_ref[:, cs])
        up = _dot(hb, wu_ref[:, cs])
        w0 = cw_ref[0:1, cs]
        w1 = cw_ref[1:2, cs]
        w2 = cw_ref[2:3, cs]
        cb = cb_ref[:, cs]
        parts = []
        for s in range(ns):
            gs = gate[s * lt:(s + 1) * lt]
            gbuf_ref[s, halo:halo + lt, cs] = gs
            g1 = gbuf_ref[s, halo - 1:halo - 1 + lt, cs]
            g2 = gbuf_ref[s, halo - 2:halo - 2 + lt, cs]
            gc = cb + w0 * g2 + w1 * g1 + w2 * gs
            parts.append((_gelu_tanh(gc) * up[s * lt:(s + 1) * lt]).astype(BF16))
        hid.append(parts[0] if ns == 1 else jnp.concatenate(parts, axis=0))
    hid = hid[0] if len(hid) == 1 else jnp.concatenate(hid, axis=1)
    y_ref[...] += _dot(hid, wd_ref[...])

    for s in range(ns):
        last2 = gbuf_ref[s, halo + lt - 2:halo + lt, :]
        cv_ref[s] = last2
        if carried:
            carry_ref[f] = last2

    @pl.when(f == nf - 1)
    def _():
        y_ref[...] = _layernorm(y_ref[...], g_ref[...], b_ref[...])


def _conv_ffn(h1, wg, wu, wd, conv_w, conv_b, past, g, b, *, tm, tf, seq_len, alpha):
    tc = 2 * LANES
    assert tf % tc == 0
    m, d_model = h1.shape
    d_ff = wg.shape[1]
    n_seq = past.shape[0]
    assert m % tm == 0 and d_ff % tf == 0 and (seq_len % tm == 0 or tm % seq_len == 0)
    nf = d_ff // tf
    lt = min(seq_len, tm)
    y, cv_all = pl.pallas_call(
        functools.partial(_conv_ffn_kernel, alpha=alpha, seq_len=seq_len, tm=tm, tc=tc),
        grid=(m // tm, nf),
        in_specs=[pl.BlockSpec((tm, d_model), lambda i, f: (i, 0)),
                  pl.BlockSpec((d_model, tf), lambda i, f: (0, f)),
                  pl.BlockSpec((d_model, tf), lambda i, f: (0, f)),
                  pl.BlockSpec((tf, d_model), lambda i, f: (f, 0)),
                  pl.BlockSpec((CONV_W, tf), lambda i, f: (0, f)),
                  pl.BlockSpec((1, tf), lambda i, f: (0, f)),
                  pl.BlockSpec((n_seq, CONV_W - 1, tf), lambda i, f: (0, 0, f)),
                  pl.BlockSpec((1, d_model), lambda i, f: (0, 0)),
                  pl.BlockSpec((1, d_model), lambda i, f: (0, 0))],
        out_specs=(pl.BlockSpec((tm, d_model), lambda i, f: (i, 0)),
                   pl.BlockSpec((tm // lt, CONV_W - 1, tf), lambda i, f: (i, 0, f))),
        out_shape=(jax.ShapeDtypeStruct((m, d_model), F32),
                   jax.ShapeDtypeStruct((m // lt, CONV_W - 1, d_ff), F32)),
        scratch_shapes=[pltpu.VMEM((tm, d_model), BF16),
                        pltpu.VMEM((tm // lt, 8 + lt, tf), F32),
                        pltpu.VMEM((nf, CONV_W - 1, tf), F32)],
        compiler_params=_cparams(("arbitrary", "arbitrary")),
        name="conv_ffn",
    )(h1, wg, wu, wd, conv_w, conv_b, past, g, b)
    return y, cv_all.reshape(n_seq, seq_len // lt, CONV_W - 1, d_ff)[:, -1]


def _rope_tables(pos):
    half = HEAD_DIM // 2
    inv = ROPE_BASE ** (-jnp.arange(half, dtype=F32) / half)
    ang = pos.astype(F32)[:, None] * inv[None, :]
    cos = jnp.cos(ang)
    sin = jnp.sin(ang)
    return jnp.concatenate([cos, cos], axis=-1), jnp.concatenate([-sin, sin], axis=-1)


def _layer(x, pos, seq_len, mk, mv, w, *, alpha, tm_proj, tm_mix, tm_ffn, tf, t_ret, s0, conv_past,
           fox_fn):
    cos, sin = _rope_tables(pos)
    fq, fk, fv, lf, c, rq, rk, rv, rg, mo = _in_proj(
        x, w["wa"], w["wf"], w["bf"], cos, sin, mk, mv, tm=tm_proj, seq_len=seq_len,
        d_fox=w["d_fox"], d_ret=w["d_ret"], d_memq=w["d_memq"])
    fo = fox_fn(fq, fk, fv, lf, c)
    ro, s_new = _retention(rq, rk, rv, rg, s0, w["gn_g"], t_blk=t_ret, seq_len=seq_len,
                           n_heads=w["d_ret"] // HEAD_DIM)
    h1 = _mix_out(x, fo, ro, mo, w["wo"], w["ln1_g"], w["ln1_b"], tm=tm_mix, alpha=alpha)
    y, cv = _conv_ffn(h1, w["wg"], w["wu"], w["wd"], w["conv_w"], w["conv_b"], conv_past,
                      w["ln2_g"], w["ln2_b"], tm=tm_ffn, tf=tf, seq_len=seq_len, alpha=alpha)
    return y, fk, fv, lf, s_new, cv


def kernel(x_prompt, x_sample, cache_fox_k, cache_fox_v, cache_fox_logf, state_ret, cache_mem_k,
           cache_mem_v, state_conv, mem_prompt, w_in, b_f, w_mem_kv, ret_gn_g, w_o, ln1_g, ln1_b,
           w_gate, w_up, conv_w, conv_b, w_down, ln2_g, ln2_b):
    batch, seq, d_model = x_prompt.shape
    dec_batch, dec_seq, _ = x_sample.shape
    depth = w_in.shape[0]
    past_len = cache_fox_k.shape[2]
    n_fox = cache_fox_k.shape[3]
    n_ret = state_ret.shape[2]
    n_mem = cache_mem_k.shape[3]
    n_mem_tok = cache_mem_k.shape[2]
    d_fox, d_ret, d_memq = n_fox * HEAD_DIM, n_ret * HEAD_DIM, n_mem * HEAD_DIM
    d_ff = w_gate.shape[2]
    assert batch == 1 and depth == 1 and n_fox <= 8
    alpha = (2.0 * depth) ** 0.25
    l = 0

    wa, wf = _w_prep(w_in[l], o_ff=3 * d_fox, n_fox=n_fox, tr=256)
    bfp = jnp.pad(b_f[l], (0, LANES - n_fox)).reshape(1, LANES)
    w = dict(wa=wa, wf=wf, bf=bfp, d_fox=d_fox, d_ret=d_ret, d_memq=d_memq,
             gn_g=ret_gn_g[l].reshape(1, d_ret), wo=w_o[l].astype(BF16),
             ln1_g=ln1_g[l].reshape(1, d_model), ln1_b=ln1_b[l].reshape(1, d_model),
             wg=w_gate[l].astype(BF16), wu=w_up[l].astype(BF16), wd=w_down[l].astype(BF16),
             conv_w=conv_w[l], conv_b=conv_b[l].reshape(1, d_ff),
             ln2_g=ln2_g[l].reshape(1, d_model), ln2_b=ln2_b[l].reshape(1, d_model))

    to_head_major = lambda t: jnp.swapaxes(t, -3, -2)
    mkv = _mem_kv(mem_prompt.reshape(n_mem_tok, d_model), w_mem_kv[l].astype(BF16))
    mk_p, mv_p = mkv[None, :n_mem], mkv[None, n_mem:]

    def fox_prompt_fn(fq, fk, fv, lf, c):
        return _fox_prompt(fq, fk, fv, c, tb=512, nsub=4, n_heads=n_fox)

    y_p, fk_p, fv_p, lf_p, s_p, cv_p = _layer(
        x_prompt.reshape(seq, d_model), jnp.arange(seq), seq, mk_p, mv_p, w,
        alpha=alpha, tm_proj=256, tm_mix=512, tm_ffn=512, tf=512, t_ret=256,
        s0=jnp.zeros((1, n_ret, HEAD_DIM, HEAD_DIM), F32),
        conv_past=jnp.zeros((1, CONV_W - 1, d_ff), F32), fox_fn=fox_prompt_fn)

    m_s = dec_batch * dec_seq
    kc = to_head_major(cache_fox_k[l])
    vc = to_head_major(cache_fox_v[l])
    lfc_row = jnp.pad(jnp.swapaxes(cache_fox_logf[l], 1, 2), ((0, 0), (0, HEAD_ROWS - n_fox), (0, 0)))

    def fox_sample_fn(fq, fk, fv, lf, c):
        lfn_row = jnp.swapaxes(lf.reshape(dec_batch, dec_seq, LANES)[:, :, :HEAD_ROWS], 1, 2)
        return _fox_sample(fq, kc, vc, fk, fv, lfc_row, lfn_row, c, n_heads=n_fox, seq_len=dec_seq)

    y_s, fk_s, fv_s, lf_s, s_s, cv_s = _layer(
        x_sample.reshape(m_s, d_model), jnp.tile(past_len + jnp.arange(dec_seq), dec_batch), dec_seq,
        to_head_major(cache_mem_k[l]), to_head_major(cache_mem_v[l]), w,
        alpha=alpha, tm_proj=dec_seq, tm_mix=m_s, tm_ffn=m_s, tf=512, t_ret=dec_seq,
        s0=state_ret[l], conv_past=state_conv[l], fox_fn=fox_sample_fn)

    return (y_p.reshape(batch, seq, d_model), y_s.reshape(dec_batch, dec_seq, d_model),
            to_head_major(fk_p)[None], to_head_major(fv_p)[None],
            lf_p[:, :n_fox].reshape(1, batch, seq, n_fox),
            s_p.reshape(1, batch, n_ret, HEAD_DIM, HEAD_DIM),
            to_head_major(mk_p)[None], to_head_major(mv_p)[None],
            cv_p.reshape(1, batch, CONV_W - 1, d_ff),
            to_head_major(fk_s)[None], to_head_major(fv_s)[None],
            lf_s[:, :n_fox].reshape(1, dec_batch, dec_seq, n_fox),
            s_s.reshape(1, dec_batch, n_ret, HEAD_DIM, HEAD_DIM),
            cv_s.reshape(1, dec_batch, CONV_W - 1, d_ff))
```

```python
import functools
import math

import numpy as np
import jax
import jax.numpy as jnp
from jax import lax
from jax.experimental import pallas as pl
from jax.experimental.pallas import tpu as pltpu

F32 = jnp.float32
BF16 = jnp.bfloat16

HEAD_DIM = 128
CHUNK = 64
CONV_W = 3
ROPE_BASE = 10000.0
LN_EPS = 1e-5
GN_EPS = 1e-5
LANES = 128
V7X_VMEM_BYTES = 64 * 1024 * 1024
VMEM_LIMIT = V7X_VMEM_BYTES - 4 * 1024 * 1024
SCALE = HEAD_DIM ** -0.5
LOG2E = math.log2(math.e)
NEG_INF = float("-inf")
HEAD_ROWS = 16


def _cparams(sem):
    return pltpu.CompilerParams(dimension_semantics=sem, vmem_limit_bytes=VMEM_LIMIT)


def _resident(shape):
    nd = len(shape)
    return pl.BlockSpec(shape, lambda *_: (0,) * nd, pipeline_mode=pl.Buffered(1))


def _dot(a, b):
    return jnp.dot(a, b, preferred_element_type=F32)


def _dot_nt(a, b):
    return lax.dot_general(a, b, (((1,), (1,)), ((), ())), preferred_element_type=F32)


def _dot_tn(a, b):
    return lax.dot_general(a, b, (((0,), (0,)), ((), ())), preferred_element_type=F32)


def _split3_dot(a, tri, *, tri_first):
    hi = a.astype(BF16)
    r1 = a - hi.astype(F32)
    mid = r1.astype(BF16)
    lo = (r1 - mid.astype(F32)).astype(BF16)
    if tri_first:
        return _dot(tri, hi) + _dot(tri, mid) + _dot(tri, lo)
    return _dot(hi, tri) + _dot(mid, tri) + _dot(lo, tri)


def _layernorm(z, g, b):
    mu = jnp.mean(z, axis=-1, keepdims=True)
    zc = z - mu
    var = jnp.mean(zc * zc, axis=-1, keepdims=True)
    return zc * lax.rsqrt(var + LN_EPS) * g + b


def _memkv_kernel(m_ref, w_ref, o_ref):
    kv = _dot(m_ref[...].astype(BF16), w_ref[...])
    for j in range(o_ref.shape[0]):
        o_ref[j] = kv[:, j * HEAD_DIM:(j + 1) * HEAD_DIM]


def _mem_kv(mem, w_bf):
    m, _ = mem.shape
    return pl.pallas_call(
        _memkv_kernel,
        out_shape=jax.ShapeDtypeStruct((w_bf.shape[1] // HEAD_DIM, m, HEAD_DIM), F32),
        compiler_params=pltpu.CompilerParams(vmem_limit_bytes=VMEM_LIMIT),
        name="mem_kv",
    )(mem, w_bf)


def _w_prep_kernel(w_ref, wa_ref, wf_ref, *, o_ff, n_fox):
    wa_ref[:, 0:o_ff] = w_ref[:, 0:o_ff].astype(BF16)
    wa_ref[:, o_ff:] = w_ref[:, o_ff + n_fox:].astype(BF16)
    lane = lax.broadcasted_iota(jnp.int32, wf_ref.shape, 1)
    wf_ref[...] = jnp.where(lane < n_fox, w_ref[:, o_ff:o_ff + LANES], 0.0).astype(BF16)


def _w_prep(w_in, *, o_ff, n_fox, tr):
    d_model, d_in = w_in.shape
    assert d_model % tr == 0 and o_ff % LANES == 0 and (d_in - n_fox) % LANES == 0
    return pl.pallas_call(
        functools.partial(_w_prep_kernel, o_ff=o_ff, n_fox=n_fox),
        grid=(d_model // tr,),
        in_specs=[pl.BlockSpec((tr, d_in), lambda i: (i, 0))],
        out_specs=(pl.BlockSpec((tr, d_in - n_fox), lambda i: (i, 0)),
                   pl.BlockSpec((tr, LANES), lambda i: (i, 0))),
        out_shape=(jax.ShapeDtypeStruct((d_model, d_in - n_fox), BF16),
                   jax.ShapeDtypeStruct((d_model, LANES), BF16)),
        compiler_params=_cparams(("arbitrary",)),
        name="w_prep",
    )(w_in)


def _in_proj_kernel(x_ref, wa_ref, wf_ref, bf_ref, base_ref, rot_ref, mk_ref, mv_ref, tri_ref,
                    fq_ref, fk_ref, fv_ref, lf_ref, c_ref, rq_ref, rk_ref, rv_ref, rg_ref, mo_ref,
                    carry_ref, *, d_fox, d_ret, d_memq, tiles_per_seq):
    i = pl.program_id(0)
    xb = x_ref[...].astype(BF16)

    def proj(lo, width):
        return _dot(xb, wa_ref[:, lo:lo + width])

    o_fk = d_fox
    o_fv = 2 * d_fox
    o_rq = 3 * d_fox
    o_rk = o_rq + d_ret
    o_rv = o_rk + d_ret
    o_rg = o_rv + d_ret
    o_mq = o_rg + d_ret

    fq_ref[...] = (proj(0, d_fox) * (SCALE * LOG2E)).astype(BF16)
    zk = proj(o_fk, d_fox)
    zv = proj(o_fv, d_fox)
    for h in range(d_fox // HEAD_DIM):
        sl = slice(h * HEAD_DIM, (h + 1) * HEAD_DIM)
        fk_ref[0, h] = zk[:, sl]
        fv_ref[0, h] = zv[:, sl]

    z = _dot(xb, wf_ref[...]) + bf_ref[...]
    lf = jnp.minimum(z, 0.0) - jnp.log1p(jnp.exp(-jnp.abs(z)))
    lf_ref[...] = lf

    @pl.when(i % tiles_per_seq == 0)
    def _():
        carry_ref[...] = jnp.zeros_like(carry_ref)

    c = _split3_dot(lf, tri_ref[...], tri_first=True) + carry_ref[0:1, :]
    c_ref[...] = c
    tm = c.shape[0]
    carry_ref[...] = jnp.broadcast_to(c[tm - 1:tm, :], carry_ref.shape)

    ca = base_ref[0, 0:1, :]
    sa = base_ref[0, 1:2, :]
    cos = ca * rot_ref[0] - sa * rot_ref[1]
    sin = sa * rot_ref[2] + ca * rot_ref[3]

    def rope(t):
        return t * cos + pltpu.roll(t, HEAD_DIM // 2, axis=1) * sin

    zq = proj(o_rq, d_ret)
    zk = proj(o_rk, d_ret)
    for h in range(d_ret // HEAD_DIM):
        sl = slice(h * HEAD_DIM, (h + 1) * HEAD_DIM)
        rq_ref[:, sl] = rope(zq[:, sl]).astype(BF16)
        rk_ref[:, sl] = (rope(zk[:, sl]) * SCALE).astype(BF16)
    rv_ref[...] = proj(o_rv, d_ret).astype(BF16)
    rg_ref[...] = proj(o_rg, d_ret).astype(BF16)

    zm = proj(o_mq, d_memq)
    for h in range(d_memq // HEAD_DIM):
        sl = slice(h * HEAD_DIM, (h + 1) * HEAD_DIM)
        q = (zm[:, sl] * SCALE).astype(BF16)
        s = _dot_nt(q, mk_ref[0, h].astype(BF16))
        p = jnp.exp(s - jnp.max(s, axis=-1, keepdims=True))
        l = jnp.sum(p, axis=-1, keepdims=True)
        o = _dot(p.astype(BF16), mv_ref[0, h].astype(BF16))
        mo_ref[:, sl] = (o / l).astype(BF16)


def _in_proj(x, wa, wf, bfp, base_tab, rot_tab, mk, mv, *, tm, seq_len, d_fox, d_ret, d_memq):
    m, d_model = x.shape
    assert m % tm == 0 and seq_len % tm == 0
    assert base_tab.shape == (m // tm, 8, HEAD_DIM) and rot_tab.shape == (4, tm, HEAD_DIM)
    tiles_per_seq = seq_len // tm
    n_fox = d_fox // HEAD_DIM
    tri = (np.arange(tm)[:, None] >= np.arange(tm)[None, :])
    tri = jnp.asarray(tri, dtype=BF16)
    row = lambda w: pl.BlockSpec((tm, w), lambda i: (i, 0))
    memspec = pl.BlockSpec((1,) + mk.shape[1:], lambda i: (i // tiles_per_seq, 0, 0, 0))
    kvspec = pl.BlockSpec((1, n_fox, tm, HEAD_DIM), lambda i: (i // tiles_per_seq, 0, i % tiles_per_seq, 0))
    kern = functools.partial(_in_proj_kernel, d_fox=d_fox, d_ret=d_ret, d_memq=d_memq,
                             tiles_per_seq=tiles_per_seq)
    out_shape = (
        jax.ShapeDtypeStruct((m, d_fox), BF16),
        jax.ShapeDtypeStruct((m // seq_len, n_fox, seq_len, HEAD_DIM), F32),
        jax.ShapeDtypeStruct((m // seq_len, n_fox, seq_len, HEAD_DIM), F32),
        jax.ShapeDtypeStruct((m, LANES), F32),
        jax.ShapeDtypeStruct((m, LANES), F32),
        jax.ShapeDtypeStruct((m, d_ret), BF16),
        jax.ShapeDtypeStruct((m, d_ret), BF16),
        jax.ShapeDtypeStruct((m, d_ret), BF16),
        jax.ShapeDtypeStruct((m, d_ret), BF16),
        jax.ShapeDtypeStruct((m, d_memq), BF16),
    )
    return pl.pallas_call(
        kern,
        grid=(m // tm,),
        in_specs=[row(d_model), _resident(wa.shape), _resident(wf.shape), _resident(bfp.shape),
                  pl.BlockSpec((1, 8, HEAD_DIM), lambda i: (i, 0, 0)), _resident(rot_tab.shape),
                  memspec, memspec, _resident(tri.shape)],
        out_specs=(row(d_fox), kvspec, kvspec, row(LANES), row(LANES),
                   row(d_ret), row(d_ret), row(d_ret), row(d_ret), row(d_memq)),
        out_shape=out_shape,
        scratch_shapes=[pltpu.VMEM((8, LANES), F32)],
        compiler_params=_cparams(("arbitrary",)),
        name="in_proj",
    )(x, wa, wf, bfp, base_tab, rot_tab, mk, mv, tri)


def _split3_f32(a):
    hi = a.astype(BF16).astype(F32)
    r1 = a - hi
    mid = r1.astype(BF16).astype(F32)
    lo = (r1 - mid).astype(BF16).astype(F32)
    return hi, mid, lo


def _fox_prompt_kernel(q_ref, k_ref, v_ref, ccol_ref, o_ref, qa, kbuf, vbuf, cstart, acc, sbuf, *, tb,
                       nsub):
    h = pl.program_id(0)
    qi = pl.program_id(1)
    tq = tb * nsub
    row0 = pl.multiple_of(qi * tq, tq)
    lane = lax.broadcasted_iota(jnp.int32, (tq, LANES), 1)
    ch = jnp.sum(jnp.where(lane == h, ccol_ref[...], 0.0), axis=1, keepdims=True)
    rel = []
    for s in range(nsub):
        blk = ch[s * tb:(s + 1) * tb]
        start = blk[0:1]
        cstart[nsub * qi + s] = jnp.broadcast_to(start, (8, LANES))
        rel.append((blk - start) * LOG2E)
    rel = rel[0] if nsub == 1 else jnp.concatenate(rel, axis=0)
    hi, mid, lo = _split3_f32(rel)
    qbias = jnp.where(lane == 0, hi, jnp.where(lane == 1, mid, jnp.where(lane == 2, lo,
                      jnp.where(lane < 6, 1.0, 0.0))))
    kbias = jnp.where(lane < 3, 1.0, jnp.where(lane == 3, -hi, jnp.where(lane == 4, -mid,
                      jnp.where(lane == 5, -lo, 0.0))))
    qa[:, 0:HEAD_DIM] = q_ref[...]
    qa[:, HEAD_DIM:] = qbias.astype(BF16)
    kbuf[pl.ds(row0, tq), 0:HEAD_DIM] = k_ref[...].astype(BF16)
    kbuf[pl.ds(row0, tq), HEAD_DIM:] = kbias.astype(BF16)
    vbuf[pl.ds(row0, tq), 0:HEAD_DIM] = v_ref[...].astype(BF16)
    vbuf[pl.ds(row0, tq), HEAD_DIM:] = jnp.where(lane == 0, 1.0, 0.0).astype(BF16)
    acc[...] = jnp.zeros_like(acc)

    r_id = lax.broadcasted_iota(jnp.int32, (tb, LANES), 0)
    c_id = lax.broadcasted_iota(jnp.int32, (tb, LANES), 1)
    n_ct = tb // LANES
    subs = range(nsub)

    def scores(s, kj):
        col0 = pl.multiple_of(kj * tb, tb)
        return _dot_nt(qa[s * tb:(s + 1) * tb, :], kbuf[pl.ds(col0, tb), :])

    def softmax_pv(s, kj, s2, m, masked):
        col0 = pl.multiple_of(kj * tb, tb)
        tiles = [s2[:, t * LANES:(t + 1) * LANES] for t in range(n_ct)]
        if masked:
            tiles = [jnp.where(c_id + t * LANES <= r_id, tiles[t], NEG_INF) for t in range(n_ct)]
        delta = (cstart[nsub * qi + s][0:1, :] - cstart[kj][0:1, :]) * LOG2E
        m_rel = m - delta
        tmax = functools.reduce(jnp.maximum, tiles)
        m_new = jnp.maximum(m_rel, jnp.max(tmax, axis=-1, keepdims=True))
        alpha = jnp.exp2(m_rel - m_new)
        p = jnp.concatenate([jnp.exp2(t - m_new).astype(BF16) for t in tiles], axis=1)
        pv = _dot(p, vbuf[pl.ds(col0, tb), :])
        a = acc[s]
        acc[s] = jnp.concatenate([alpha * a[:, 0:LANES], alpha * a[:, LANES:]], axis=1) + pv
        return m_new + delta

    for s in subs:
        sbuf[s] = scores(s, 0)

    def body(t, ms):
        kj = 2 * t
        s_a = [sbuf[s] for s in subs]
        s_b = [scores(s, kj + 1) for s in subs]
        ms = [softmax_pv(s, kj, s_a[s], ms[s], False) for s in subs]
        s_c = [scores(s, kj + 2) for s in subs]
        ms = [softmax_pv(s, kj + 1, s_b[s], ms[s], False) for s in subs]
        for s in subs:
            sbuf[s] = s_c[s]
        return tuple(ms)

    assert nsub % 2 == 0
    ms = lax.fori_loop(0, (nsub // 2) * qi, body,
                       tuple(jnp.full((tb, LANES), NEG_INF, F32) for _ in subs))
    for s in subs:
        m = ms[s]
        for t in range(s + 1):
            s2 = sbuf[s] if t == 0 else scores(s, nsub * qi + t)
            m = softmax_pv(s, nsub * qi + t, s2, m, t == s)
        a = acc[s]
        o_ref[s * tb:(s + 1) * tb, :] = (a[:, 0:HEAD_DIM] / a[:, HEAD_DIM:HEAD_DIM + 1]).astype(BF16)


def _fox_prompt(fq, fk, fv, c_col, *, tb, nsub, n_heads):
    s, _ = fq.shape
    tq = tb * nsub
    assert s % tq == 0
    blk = lambda: pl.BlockSpec((tq, HEAD_DIM), lambda h, qi: (qi, h))
    kv = lambda: pl.BlockSpec((None, None, tq, HEAD_DIM), lambda h, qi: (0, h, qi, 0))
    return pl.pallas_call(
        functools.partial(_fox_prompt_kernel, tb=tb, nsub=nsub),
        grid=(n_heads, s // tq),
        in_specs=[blk(), kv(), kv(), pl.BlockSpec((tq, LANES), lambda h, qi: (qi, 0))],
        out_specs=blk(),
        out_shape=jax.ShapeDtypeStruct(fq.shape, BF16),
        scratch_shapes=[pltpu.VMEM((tq, 2 * HEAD_DIM), BF16),
                        pltpu.VMEM((s, 2 * HEAD_DIM), BF16),
                        pltpu.VMEM((s, 2 * HEAD_DIM), BF16),
                        pltpu.VMEM((s // tb, 8, LANES), F32),
                        pltpu.VMEM((nsub, tb, 2 * HEAD_DIM), F32),
                        pltpu.VMEM((nsub, tb, tb), F32)],
        compiler_params=_cparams(("arbitrary", "arbitrary")),
        name="fox_prompt",
    )(fq, fk, fv, c_col)


def _fox_sample_kernel(q_ref, kc_ref, vc_ref, kn_ref, vn_ref, lfc_ref, lfn_ref, ccol_ref,
                       tric_ref, trin_ref, o_ref, *, n_heads):
    c_cache = _split3_dot(lfc_ref[0], tric_ref[...], tri_first=False)
    p_len = c_cache.shape[1]
    ck_cache = c_cache - c_cache[:, p_len - 1:p_len]
    ck_new = _split3_dot(lfn_ref[0], trin_ref[...], tri_first=False)
    ccol = ccol_ref[...]
    n_new = ck_new.shape[1]
    r = lax.broadcasted_iota(jnp.int32, (n_new, n_new), 0)
    c = lax.broadcasted_iota(jnp.int32, (n_new, n_new), 1)
    for h in range(n_heads):
        sl = slice(h * HEAD_DIM, (h + 1) * HEAD_DIM)
        q = q_ref[:, sl]
        cq = ccol[:, h:h + 1]
        s_c = _dot_nt(q, kc_ref[0, h].astype(BF16)) + (cq - ck_cache[h:h + 1, :]) * LOG2E
        s_n = _dot_nt(q, kn_ref[0, h].astype(BF16)) + (cq - ck_new[h:h + 1, :]) * LOG2E
        s_n = jnp.where(c <= r, s_n, NEG_INF)
        m = jnp.maximum(jnp.max(s_c, axis=-1, keepdims=True), jnp.max(s_n, axis=-1, keepdims=True))
        p_c = jnp.exp2(s_c - m)
        p_n = jnp.exp2(s_n - m)
        l = jnp.sum(p_c, axis=-1, keepdims=True) + jnp.sum(p_n, axis=-1, keepdims=True)
        o = _dot(p_c.astype(BF16), vc_ref[0, h].astype(BF16)) + \
            _dot(p_n.astype(BF16), vn_ref[0, h].astype(BF16))
        o_ref[:, sl] = (o / l).astype(BF16)


def _fox_sample(fq, kc, vc, kn, vn, lfc_row, lfn_row, c_col, *, n_heads, seq_len):
    m, d_fox = fq.shape
    nb, _, p_len, _ = kc.shape
    tric = jnp.asarray(np.arange(p_len)[:, None] <= np.arange(p_len)[None, :], dtype=BF16)
    trin = jnp.asarray(np.arange(seq_len)[:, None] <= np.arange(seq_len)[None, :], dtype=BF16)
    row = lambda w: pl.BlockSpec((seq_len, w), lambda b: (b, 0))
    cache = pl.BlockSpec((1, n_heads, p_len, HEAD_DIM), lambda b: (b, 0, 0, 0))
    new = pl.BlockSpec((1, n_heads, seq_len, HEAD_DIM), lambda b: (b, 0, 0, 0))
    return pl.pallas_call(
        functools.partial(_fox_sample_kernel, n_heads=n_heads),
        grid=(nb,),
        in_specs=[row(d_fox), cache, cache, new, new,
                  pl.BlockSpec((1, HEAD_ROWS, p_len), lambda b: (b, 0, 0)),
                  pl.BlockSpec((1, HEAD_ROWS, seq_len), lambda b: (b, 0, 0)),
                  row(LANES), _resident(tric.shape), _resident(trin.shape)],
        out_specs=row(d_fox),
        out_shape=jax.ShapeDtypeStruct((m, d_fox), BF16),
        compiler_params=_cparams(("arbitrary",)),
        name="fox_sample",
    )(fq, kc, vc, kn, vn, lfc_row, lfn_row, c_col, tric, trin)


def _log_gamma(h):
    return float(np.log1p(-np.exp2(np.float32(-5.0 - h)), dtype=np.float32))


def _retention_kernel(q_ref, k_ref, v_ref, g_ref, s0_ref, gn_ref, o_ref, s_ref, decay_ref, *, n_heads,
                      t_blk):
    step = pl.program_id(1)

    @pl.when(step == 0)
    def _():
        s_ref[...] = s0_ref[...]

    @pl.when(jnp.logical_and(pl.program_id(0) == 0, step == 0))
    def _():
        ti = lax.broadcasted_iota(jnp.int32, (t_blk, t_blk), 0)
        si = lax.broadcasted_iota(jnp.int32, (t_blk, t_blk), 1)
        dist = jnp.abs(ti - si).astype(F32)
        shift = CHUNK.bit_length() - 1
        visible = jnp.right_shift(si, shift) <= jnp.right_shift(ti, shift)
        for h in range(n_heads):
            decay_ref[h] = jnp.where(visible, jnp.exp(_log_gamma(h) * dist), 0.0)

    tcol = lax.broadcasted_iota(jnp.int32, (t_blk, 1), 0).astype(F32)
    for h in range(n_heads):
        lg = _log_gamma(h)
        sl = slice(h * HEAD_DIM, (h + 1) * HEAD_DIM)
        q = q_ref[:, sl]
        k = k_ref[:, sl]
        v = v_ref[:, sl]
        a = _dot_nt(q, k) * decay_ref[h]
        state = s_ref[0, h]
        o = _dot(a.astype(BF16), v) + _dot(q, state.astype(BF16)) * jnp.exp(lg * (tcol + 1.0))
        kd = (k.astype(F32) * jnp.exp(lg * (t_blk - 1.0 - tcol))).astype(BF16)
        s_ref[0, h] = math.exp(lg * t_blk) * state + _dot_tn(kd, v)
        mu = jnp.mean(o, axis=-1, keepdims=True)
        oc = o - mu
        var = jnp.mean(oc * oc, axis=-1, keepdims=True)
        rn = oc * lax.rsqrt(var + GN_EPS) * gn_ref[:, sl]
        gate = g_ref[:, sl].astype(F32)
        o_ref[:, sl] = (gate / (1.0 + jnp.exp(-gate)) * rn).astype(BF16)


def _retention(rq, rk, rv, rg, s0, gn_g, *, t_blk, seq_len, n_heads):
    m, d_ret = rq.shape
    nb = m // seq_len
    steps = seq_len // t_blk
    row = lambda: pl.BlockSpec((t_blk, d_ret), lambda b, s: (b * steps + s, 0))
    st = lambda: pl.BlockSpec((1, n_heads, HEAD_DIM, HEAD_DIM), lambda b, s: (b, 0, 0, 0))
    return pl.pallas_call(
        functools.partial(_retention_kernel, n_heads=n_heads, t_blk=t_blk),
        grid=(nb, steps),
        in_specs=[row(), row(), row(), row(), st(), pl.BlockSpec((1, d_ret), lambda b, s: (0, 0))],
        out_specs=(row(), st()),
        out_shape=(jax.ShapeDtypeStruct((m, d_ret), BF16),
                   jax.ShapeDtypeStruct((nb, n_heads, HEAD_DIM, HEAD_DIM), F32)),
        scratch_shapes=[pltpu.VMEM((n_heads, t_blk, t_blk), F32)],
        compiler_params=_cparams(("arbitrary", "arbitrary")),
        name="retention",
    )(rq, rk, rv, rg, s0, gn_g)


def _mix_out_kernel(x_ref, fo_ref, ro_ref, mo_ref, wo_ref, g_ref, b_ref, o_ref, cat_ref, *, alpha):
    d_fox = fo_ref.shape[1]
    d_ret = ro_ref.shape[1]
    cat_ref[:, 0:d_fox] = fo_ref[...]
    cat_ref[:, d_fox:d_fox + d_ret] = ro_ref[...]
    cat_ref[:, d_fox + d_ret:] = mo_ref[...]
    tm = x_ref.shape[0]
    pieces = [slice(0, tm // 2), slice(tm // 2, tm)] if tm % 32 == 0 else [slice(0, tm)]
    mix = [_dot(cat_ref[r, :], wo_ref[...]) for r in pieces]
    for r, mx in zip(pieces, mix):
        o_ref[r, :] = _layernorm(alpha * x_ref[r, :] + mx, g_ref[...], b_ref[...])


def _mix_out(x, fo, ro, mo, wo, g, b, *, tm, alpha):
    m, d_model = x.shape
    row = lambda w: pl.BlockSpec((tm, w), lambda i: (i, 0))
    return pl.pallas_call(
        functools.partial(_mix_out_kernel, alpha=alpha),
        grid=(m // tm,),
        in_specs=[row(d_model), row(fo.shape[1]), row(ro.shape[1]), row(mo.shape[1]),
                  _resident(wo.shape), _resident(g.shape), _resident(b.shape)],
        out_specs=row(d_model),
        out_shape=jax.ShapeDtypeStruct((m, d_model), F32),
        scratch_shapes=[pltpu.VMEM((tm, wo.shape[0]), BF16)],
        compiler_params=_cparams(("arbitrary",)),
        name="mix_out",
    )(x, fo, ro, mo, wo, g, b)


def _gelu_tanh(x):
    c = math.sqrt(2.0 / math.pi)
    return x * (0.5 * (1.0 + jnp.tanh(c * (x + 0.044715 * (x * x * x)))))


def _conv_ffn_kernel(h_ref, wg_ref, wu_ref, wd_ref, cw_ref, cb_ref, past_ref, g_ref, b_ref,
                     y_ref, cv_ref, hb_ref, gbuf_ref, carry_ref, *, alpha, seq_len, tm, tc):
    i = pl.program_id(0)
    f = pl.program_id(1)
    nf = pl.num_programs(1)
    lt = min(seq_len, tm)
    ns = tm // lt
    halo = 8
    carried = seq_len > tm

    @pl.when(f == 0)
    def _():
        h = h_ref[...]
        hb_ref[...] = h.astype(BF16)
        y_ref[...] = alpha * h

    if carried:
        @pl.when(jnp.logical_and(i == 0, f == 0))
        def _():
            carry_ref[...] = jnp.zeros_like(carry_ref)

        tiles_per_seq = seq_len // tm
        gbuf_ref[0, halo - 2:halo, :] = jnp.where(i % tiles_per_seq == 0, past_ref[i // tiles_per_seq],
                                                  carry_ref[f])
    else:
        for s in range(ns):
            gbuf_ref[s, halo - 2:halo, :] = past_ref[i * ns + s]

    hb = hb_ref[...]
    hid = []
    for c0 in range(0, wg_ref.shape[1], tc):
        cs = slice(c0, c0 + tc)
        gate = _dot(hb, wg_ref[:, cs])
        up = _dot(hb, wu_ref[:, cs])
        w0 = cw_ref[0:1, cs]
        w1 = cw_ref[1:2, cs]
        w2 = cw_ref[2:3, cs]
        cb = cb_ref[:, cs]
        parts = []
        for s in range(ns):
            gs = gate[s * lt:(s + 1) * lt]
            gbuf_ref[s, halo:halo + lt, cs] = gs
            g1 = gbuf_ref[s, halo - 1:halo - 1 + lt, cs]
            g2 = gbuf_ref[s, halo - 2:halo - 2 + lt, cs]
            gc = cb + w0 * g2 + w1 * g1 + w2 * gs
            parts.append((_gelu_tanh(gc) * up[s * lt:(s + 1) * lt]).astype(BF16))
        hid.append(parts[0] if ns == 1 else jnp.concatenate(parts, axis=0))
    hid = hid[0] if len(hid) == 1 else jnp.concatenate(hid, axis=1)
    y_ref[...] += _dot(hid, wd_ref[...])

    for s in range(ns):
        last2 = gbuf_ref[s, halo + lt - 2:halo + lt, :]
        cv_ref[s] = last2
        if carried:
            carry_ref[f] = last2

    @pl.when(f == nf - 1)
    def _():
        y_ref[...] = _layernorm(y_ref[...], g_ref[...], b_ref[...])


def _conv_ffn(h1, wg, wu, wd, conv_w, conv_b, past, g, b, *, tm, tf, seq_len, alpha):
    tc = 2 * LANES
    assert tf % tc == 0
    m, d_model = h1.shape
    d_ff = wg.shape[1]
    n_seq = past.shape[0]
    assert m % tm == 0 and d_ff % tf == 0 and (seq_len % tm == 0 or tm % seq_len == 0)
    nf = d_ff // tf
    lt = min(seq_len, tm)
    y, cv_all = pl.pallas_call(
        functools.partial(_conv_ffn_kernel, alpha=alpha, seq_len=seq_len, tm=tm, tc=tc),
        grid=(m // tm, nf),
        in_specs=[pl.BlockSpec((tm, d_model), lambda i, f: (i, 0), pipeline_mode=pl.Buffered(1)),
                  pl.BlockSpec((d_model, tf), lambda i, f: (0, f)),
                  pl.BlockSpec((d_model, tf), lambda i, f: (0, f)),
                  pl.BlockSpec((tf, d_model), lambda i, f: (f, 0)),
                  pl.BlockSpec((CONV_W, tf), lambda i, f: (0, f)),
                  pl.BlockSpec((1, tf), lambda i, f: (0, f)),
                  pl.BlockSpec((n_seq, CONV_W - 1, tf), lambda i, f: (0, 0, f)),
                  pl.BlockSpec((1, d_model), lambda i, f: (0, 0)),
                  pl.BlockSpec((1, d_model), lambda i, f: (0, 0))],
        out_specs=(pl.BlockSpec((tm, d_model), lambda i, f: (i, 0)),
                   pl.BlockSpec((tm // lt, CONV_W - 1, tf), lambda i, f: (i, 0, f))),
        out_shape=(jax.ShapeDtypeStruct((m, d_model), F32),
                   jax.ShapeDtypeStruct((m // lt, CONV_W - 1, d_ff), F32)),
        scratch_shapes=[pltpu.VMEM((tm, d_model), BF16),
                        pltpu.VMEM((tm // lt, 8 + lt, tf), F32),
                        pltpu.VMEM((nf, CONV_W - 1, tf), F32)],
        compiler_params=_cparams(("arbitrary", "arbitrary")),
        name="conv_ffn",
    )(h1, wg, wu, wd, conv_w, conv_b, past, g, b)
    return y, cv_all.reshape(n_seq, seq_len // lt, CONV_W - 1, d_ff)[:, -1]


def _rope_tables(tile_base, tm):
    half = HEAD_DIM // 2
    inv = ROPE_BASE ** (-jnp.arange(half, dtype=F32) / half)
    inv = jnp.concatenate([inv, inv])
    sign = jnp.concatenate([-jnp.ones((half,), F32), jnp.ones((half,), F32)])
    a = tile_base.astype(F32)[:, None] * inv[None, :]
    b = jnp.arange(tm, dtype=F32)[:, None] * inv[None, :]
    base_tab = jnp.stack([jnp.cos(a), jnp.sin(a)], axis=1)
    base_tab = jnp.pad(base_tab, ((0, 0), (0, 6), (0, 0)))
    rot_tab = jnp.stack([jnp.cos(b), jnp.sin(b), sign * jnp.cos(b), sign * jnp.sin(b)])
    return base_tab, rot_tab


def _layer(x, seq_start, seq_len, mk, mv, w, *, alpha, tm_proj, tm_mix, tm_ffn, tf, t_ret, s0, conv_past,
           fox_fn):
    tile_base = (seq_start[:, None] + jnp.arange(0, seq_len, tm_proj)[None, :]).reshape(-1)
    base_tab, rot_tab = _rope_tables(tile_base, tm_proj)
    fq, fk, fv, lf, c, rq, rk, rv, rg, mo = _in_proj(
        x, w["wa"], w["wf"], w["bf"], base_tab, rot_tab, mk, mv, tm=tm_proj, seq_len=seq_len,
        d_fox=w["d_fox"], d_ret=w["d_ret"], d_memq=w["d_memq"])
    fo = fox_fn(fq, fk, fv, lf, c)
    ro, s_new = _retention(rq, rk, rv, rg, s0, w["gn_g"], t_blk=t_ret, seq_len=seq_len,
                           n_heads=w["d_ret"] // HEAD_DIM)
    h1 = _mix_out(x, fo, ro, mo, w["wo"], w["ln1_g"], w["ln1_b"], tm=tm_mix, alpha=alpha)
    y, cv = _conv_ffn(h1, w["wg"], w["wu"], w["wd"], w["conv_w"], w["conv_b"], conv_past,
                      w["ln2_g"], w["ln2_b"], tm=tm_ffn, tf=tf, seq_len=seq_len, alpha=alpha)
    return y, fk, fv, lf, s_new, cv


def kernel(x_prompt, x_sample, cache_fox_k, cache_fox_v, cache_fox_logf, state_ret, cache_mem_k,
           cache_mem_v, state_conv, mem_prompt, w_in, b_f, w_mem_kv, ret_gn_g, w_o, ln1_g, ln1_b,
           w_gate, w_up, conv_w, conv_b, w_down, ln2_g, ln2_b):
    batch, seq, d_model = x_prompt.shape
    dec_batch, dec_seq, _ = x_sample.shape
    depth = w_in.shape[0]
    past_len = cache_fox_k.shape[2]
    n_fox = cache_fox_k.shape[3]
    n_ret = state_ret.shape[2]
    n_mem = cache_mem_k.shape[3]
    n_mem_tok = cache_mem_k.shape[2]
    d_fox, d_ret, d_memq = n_fox * HEAD_DIM, n_ret * HEAD_DIM, n_mem * HEAD_DIM
    d_ff = w_gate.shape[2]
    assert batch == 1 and depth == 1 and n_fox <= 8
    alpha = (2.0 * depth) ** 0.25
    l = 0

    wa, wf = _w_prep(w_in[l], o_ff=3 * d_fox, n_fox=n_fox, tr=256)
    bfp = jnp.pad(b_f[l], (0, LANES - n_fox)).reshape(1, LANES)
    w = dict(wa=wa, wf=wf, bf=bfp, d_fox=d_fox, d_ret=d_ret, d_memq=d_memq,
             gn_g=ret_gn_g[l].reshape(1, d_ret), wo=w_o[l].astype(BF16),
             ln1_g=ln1_g[l].reshape(1, d_model), ln1_b=ln1_b[l].reshape(1, d_model),
             wg=w_gate[l].astype(BF16), wu=w_up[l].astype(BF16), wd=w_down[l].astype(BF16),
             conv_w=conv_w[l], conv_b=conv_b[l].reshape(1, d_ff),
             ln2_g=ln2_g[l].reshape(1, d_model), ln2_b=ln2_b[l].reshape(1, d_model))

    to_head_major = lambda t: jnp.swapaxes(t, -3, -2)
    mkv = _mem_kv(mem_prompt.reshape(n_mem_tok, d_model), w_mem_kv[l].astype(BF16))
    mk_p, mv_p = mkv[None, :n_mem], mkv[None, n_mem:]

    def fox_prompt_fn(fq, fk, fv, lf, c):
        return _fox_prompt(fq, fk, fv, c, tb=512, nsub=4, n_heads=n_fox)

    y_p, fk_p, fv_p, lf_p, s_p, cv_p = _layer(
        x_prompt.reshape(seq, d_model), jnp.zeros((batch,), jnp.int32), seq, mk_p, mv_p, w,
        alpha=alpha, tm_proj=256, tm_mix=512, tm_ffn=1024, tf=512, t_ret=256,
        s0=jnp.zeros((1, n_ret, HEAD_DIM, HEAD_DIM), F32),
        conv_past=jnp.zeros((1, CONV_W - 1, d_ff), F32), fox_fn=fox_prompt_fn)

    m_s = dec_batch * dec_seq
    kc = to_head_major(cache_fox_k[l])
    vc = to_head_major(cache_fox_v[l])
    lfc_row = jnp.pad(jnp.swapaxes(cache_fox_logf[l], 1, 2), ((0, 0), (0, HEAD_ROWS - n_fox), (0, 0)))

    def fox_sample_fn(fq, fk, fv, lf, c):
        lfn_row = jnp.swapaxes(lf.reshape(dec_batch, dec_seq, LANES)[:, :, :HEAD_ROWS], 1, 2)
        return _fox_sample(fq, kc, vc, fk, fv, lfc_row, lfn_row, c, n_heads=n_fox, seq_len=dec_seq)

    y_s, fk_s, fv_s, lf_s, s_s, cv_s = _layer(
        x_sample.reshape(m_s, d_model), jnp.full((dec_batch,), past_len, jnp.int32), dec_seq,
        to_head_major(cache_mem_k[l]), to_head_major(cache_mem_v[l]), w,
        alpha=alpha, tm_proj=dec_seq, tm_mix=m_s, tm_ffn=m_s, tf=512, t_ret=dec_seq,
        s0=state_ret[l], conv_past=state_conv[l], fox_fn=fox_sample_fn)

    return (y_p.reshape(batch, seq, d_model), y_s.reshape(dec_batch, dec_seq, d_model),
            to_head_major(fk_p)[None], to_head_major(fv_p)[None],
            lf_p[:, :n_fox].reshape(1, batch, seq, n_fox),
            s_p.reshape(1, batch, n_ret, HEAD_DIM, HEAD_DIM),
            to_head_major(mk_p)[None], to_head_major(mv_p)[None],
            cv_p.reshape(1, batch, CONV_W - 1, d_ff),
            to_head_major(fk_s)[None], to_head_major(fv_s)[None],
            lf_s[:, :n_fox].reshape(1, dec_batch, dec_seq, n_fox),
            s_s.reshape(1, dec_batch, n_ret, HEAD_DIM, HEAD_DIM),
            cv_s.reshape(1, dec_batch, CONV_W - 1, d_ff))
```

```python
import functools
import math

import numpy as np
import jax
import jax.numpy as jnp
from jax import lax
from jax.experimental import pallas as pl
from jax.experimental.pallas import tpu as pltpu

F32 = jnp.float32
BF16 = jnp.bfloat16

HEAD_DIM = 128
CHUNK = 64
CONV_W = 3
ROPE_BASE = 10000.0
LN_EPS = 1e-5
GN_EPS = 1e-5
LANES = 128
V7X_VMEM_BYTES = 64 * 1024 * 1024
VMEM_LIMIT = V7X_VMEM_BYTES - 4 * 1024 * 1024
SCALE = HEAD_DIM ** -0.5
LOG2E = math.log2(math.e)
NEG_INF = float("-inf")
HEAD_ROWS = 16


def _cparams(sem):
    return pltpu.CompilerParams(dimension_semantics=sem, vmem_limit_bytes=VMEM_LIMIT)


def _resident(shape):
    nd = len(shape)
    return pl.BlockSpec(shape, lambda *_: (0,) * nd, pipeline_mode=pl.Buffered(1))


def _dot(a, b):
    return jnp.dot(a, b, preferred_element_type=F32)


def _dot_nt(a, b):
    return lax.dot_general(a, b, (((1,), (1,)), ((), ())), preferred_element_type=F32)


def _dot_tn(a, b):
    return lax.dot_general(a, b, (((0,), (0,)), ((), ())), preferred_element_type=F32)


def _split3_dot(a, tri, *, tri_first):
    hi = a.astype(BF16)
    r1 = a - hi.astype(F32)
    mid = r1.astype(BF16)
    lo = (r1 - mid.astype(F32)).astype(BF16)
    if tri_first:
        return _dot(tri, hi) + _dot(tri, mid) + _dot(tri, lo)
    return _dot(hi, tri) + _dot(mid, tri) + _dot(lo, tri)


def _layernorm(z, g, b):
    mu = jnp.mean(z, axis=-1, keepdims=True)
    zc = z - mu
    var = jnp.mean(zc * zc, axis=-1, keepdims=True)
    return zc * lax.rsqrt(var + LN_EPS) * g + b


def _memkv_kernel(m_ref, w_ref, o_ref):
    kv = _dot(m_ref[...].astype(BF16), w_ref[...])
    for j in range(o_ref.shape[0]):
        o_ref[j] = kv[:, j * HEAD_DIM:(j + 1) * HEAD_DIM]


def _mem_kv(mem, w_bf):
    m, _ = mem.shape
    return pl.pallas_call(
        _memkv_kernel,
        out_shape=jax.ShapeDtypeStruct((w_bf.shape[1] // HEAD_DIM, m, HEAD_DIM), F32),
        compiler_params=pltpu.CompilerParams(vmem_limit_bytes=VMEM_LIMIT),
        name="mem_kv",
    )(mem, w_bf)


def _w_prep_kernel(w_hbm, wa_ref, wf_ref, buf, sem, *, o_ff, n_fox, tn):
    j = pl.program_id(0)

    def fetch(r0):
        cp = pltpu.make_async_copy(w_hbm.at[pl.ds(r0, tn)], buf, sem)
        cp.start()
        cp.wait()
        return buf[:, 0, :]

    @pl.when(j == 0)
    def _():
        rows = fetch(o_ff)[0:LANES]
        keep = lax.broadcasted_iota(jnp.int32, rows.shape, 0) < n_fox
        wf_ref[...] = jnp.where(keep, rows, 0.0).T.astype(BF16)

    wa_ref[...] = fetch(jnp.where(j * tn < o_ff, j * tn, j * tn + n_fox)).T.astype(BF16)


def _w_prep(w_fm, *, o_ff, n_fox, tn):
    n, _, d_model = w_fm.shape
    assert o_ff % tn == 0 and (n - n_fox) % tn == 0 and tn >= LANES and o_ff + tn <= n
    return pl.pallas_call(
        functools.partial(_w_prep_kernel, o_ff=o_ff, n_fox=n_fox, tn=tn),
        grid=((n - n_fox) // tn,),
        in_specs=[pl.BlockSpec(memory_space=pl.ANY)],
        out_specs=(pl.BlockSpec((d_model, tn), lambda j: (0, j)),
                   pl.BlockSpec((d_model, LANES), lambda j: (0, 0))),
        out_shape=(jax.ShapeDtypeStruct((d_model, n - n_fox), BF16),
                   jax.ShapeDtypeStruct((d_model, LANES), BF16)),
        scratch_shapes=[pltpu.VMEM((tn, 1, d_model), F32), pltpu.SemaphoreType.DMA(())],
        compiler_params=_cparams(("arbitrary",)),
        name="w_prep",
    )(w_fm)


def _in_proj_kernel(x_ref, wa_ref, wf_ref, bf_ref, base_ref, rot_ref, mk_ref, mv_ref, tri_ref,
                    fq_ref, fk_ref, fv_ref, lf_ref, c_ref, rq_ref, rk_ref, rv_ref, rg_ref, mo_ref,
                    carry_ref, *, d_fox, d_ret, d_memq, tiles_per_seq):
    i = pl.program_id(0)
    xb = x_ref[...].astype(BF16)

    def proj(lo, width):
        return _dot(xb, wa_ref[:, lo:lo + width])

    o_fk = d_fox
    o_fv = 2 * d_fox
    o_rq = 3 * d_fox
    o_rk = o_rq + d_ret
    o_rv = o_rk + d_ret
    o_rg = o_rv + d_ret
    o_mq = o_rg + d_ret

    fq_ref[...] = (proj(0, d_fox) * (SCALE * LOG2E)).astype(BF16)
    zk = proj(o_fk, d_fox)
    zv = proj(o_fv, d_fox)
    for h in range(d_fox // HEAD_DIM):
        sl = slice(h * HEAD_DIM, (h + 1) * HEAD_DIM)
        fk_ref[0, h] = zk[:, sl]
        fv_ref[0, h] = zv[:, sl]

    z = _dot(xb, wf_ref[...]) + bf_ref[...]
    lf = jnp.minimum(z, 0.0) - jnp.log1p(jnp.exp(-jnp.abs(z)))
    lf_ref[...] = lf

    @pl.when(i % tiles_per_seq == 0)
    def _():
        carry_ref[...] = jnp.zeros_like(carry_ref)

    c = _split3_dot(lf, tri_ref[...], tri_first=True) + carry_ref[0:1, :]
    c_ref[...] = c
    tm = c.shape[0]
    carry_ref[...] = jnp.broadcast_to(c[tm - 1:tm, :], carry_ref.shape)

    ca = base_ref[0, 0:1, :]
    sa = base_ref[0, 1:2, :]
    cos = ca * rot_ref[0] - sa * rot_ref[1]
    sin = sa * rot_ref[2] + ca * rot_ref[3]

    def rope(t):
        return t * cos + pltpu.roll(t, HEAD_DIM // 2, axis=1) * sin

    zq = proj(o_rq, d_ret)
    zk = proj(o_rk, d_ret)
    for h in range(d_ret // HEAD_DIM):
        sl = slice(h * HEAD_DIM, (h + 1) * HEAD_DIM)
        rq_ref[:, sl] = rope(zq[:, sl]).astype(BF16)
        rk_ref[:, sl] = (rope(zk[:, sl]) * SCALE).astype(BF16)
    rv_ref[...] = proj(o_rv, d_ret).astype(BF16)
    rg_ref[...] = proj(o_rg, d_ret).astype(BF16)

    zm = proj(o_mq, d_memq)
    for h in range(d_memq // HEAD_DIM):
        sl = slice(h * HEAD_DIM, (h + 1) * HEAD_DIM)
        q = (zm[:, sl] * SCALE).astype(BF16)
        s = _dot_nt(q, mk_ref[0, h].astype(BF16))
        p = jnp.exp(s - jnp.max(s, axis=-1, keepdims=True))
        l = jnp.sum(p, axis=-1, keepdims=True)
        o = _dot(p.astype(BF16), mv_ref[0, h].astype(BF16))
        mo_ref[:, sl] = (o / l).astype(BF16)


def _in_proj(x, wa, wf, bfp, base_tab, rot_tab, mk, mv, *, tm, seq_len, d_fox, d_ret, d_memq):
    m, d_model = x.shape
    assert m % tm == 0 and seq_len % tm == 0
    assert base_tab.shape == (m // tm, 8, HEAD_DIM) and rot_tab.shape == (4, tm, HEAD_DIM)
    tiles_per_seq = seq_len // tm
    n_fox = d_fox // HEAD_DIM
    tri = (np.arange(tm)[:, None] >= np.arange(tm)[None, :])
    tri = jnp.asarray(tri, dtype=BF16)
    row = lambda w: pl.BlockSpec((tm, w), lambda i: (i, 0))
    memspec = pl.BlockSpec((1,) + mk.shape[1:], lambda i: (i // tiles_per_seq, 0, 0, 0))
    kvspec = pl.BlockSpec((1, n_fox, tm, HEAD_DIM), lambda i: (i // tiles_per_seq, 0, i % tiles_per_seq, 0))
    kern = functools.partial(_in_proj_kernel, d_fox=d_fox, d_ret=d_ret, d_memq=d_memq,
                             tiles_per_seq=tiles_per_seq)
    out_shape = (
        jax.ShapeDtypeStruct((m, d_fox), BF16),
        jax.ShapeDtypeStruct((m // seq_len, n_fox, seq_len, HEAD_DIM), F32),
        jax.ShapeDtypeStruct((m // seq_len, n_fox, seq_len, HEAD_DIM), F32),
        jax.ShapeDtypeStruct((m, LANES), F32),
        jax.ShapeDtypeStruct((m, LANES), F32),
        jax.ShapeDtypeStruct((m, d_ret), BF16),
        jax.ShapeDtypeStruct((m, d_ret), BF16),
        jax.ShapeDtypeStruct((m, d_ret), BF16),
        jax.ShapeDtypeStruct((m, d_ret), BF16),
        jax.ShapeDtypeStruct((m, d_memq), BF16),
    )
    return pl.pallas_call(
        kern,
        grid=(m // tm,),
        in_specs=[row(d_model), _resident(wa.shape), _resident(wf.shape), _resident(bfp.shape),
                  pl.BlockSpec((1, 8, HEAD_DIM), lambda i: (i, 0, 0)), _resident(rot_tab.shape),
                  memspec, memspec, _resident(tri.shape)],
        out_specs=(row(d_fox), kvspec, kvspec, row(LANES), row(LANES),
                   row(d_ret), row(d_ret), row(d_ret), row(d_ret), row(d_memq)),
        out_shape=out_shape,
        scratch_shapes=[pltpu.VMEM((8, LANES), F32)],
        compiler_params=_cparams(("arbitrary",)),
        name="in_proj",
    )(x, wa, wf, bfp, base_tab, rot_tab, mk, mv, tri)


def _split3_f32(a):
    hi = a.astype(BF16).astype(F32)
    r1 = a - hi
    mid = r1.astype(BF16).astype(F32)
    lo = (r1 - mid).astype(BF16).astype(F32)
    return hi, mid, lo


def _fox_prompt_kernel(q_ref, k_ref, v_ref, ccol_ref, o_ref, qa, kbuf, vbuf, cstart, acc, sbuf, *, tb,
                       nsub, kv_per_trip):
    h = pl.program_id(0)
    qi = pl.program_id(1)
    tq = tb * nsub
    row0 = pl.multiple_of(qi * tq, tq)
    lane = lax.broadcasted_iota(jnp.int32, (tq, LANES), 1)
    ch = jnp.sum(jnp.where(lane == h, ccol_ref[...], 0.0), axis=1, keepdims=True)
    rel = []
    for s in range(nsub):
        blk = ch[s * tb:(s + 1) * tb]
        start = blk[0:1]
        cstart[nsub * qi + s] = jnp.broadcast_to(start, (8, LANES))
        rel.append((blk - start) * LOG2E)
    rel = rel[0] if nsub == 1 else jnp.concatenate(rel, axis=0)
    hi, mid, lo = _split3_f32(rel)
    qbias = jnp.where(lane == 0, hi, jnp.where(lane == 1, mid, jnp.where(lane == 2, lo,
                      jnp.where(lane < 6, 1.0, 0.0))))
    kbias = jnp.where(lane < 3, 1.0, jnp.where(lane == 3, -hi, jnp.where(lane == 4, -mid,
                      jnp.where(lane == 5, -lo, 0.0))))
    qa[:, 0:HEAD_DIM] = q_ref[...]
    qa[:, HEAD_DIM:] = qbias.astype(BF16)
    kbuf[pl.ds(row0, tq), 0:HEAD_DIM] = k_ref[...].astype(BF16)
    kbuf[pl.ds(row0, tq), HEAD_DIM:] = kbias.astype(BF16)
    vbuf[pl.ds(row0, tq), 0:HEAD_DIM] = v_ref[...].astype(BF16)
    vbuf[pl.ds(row0, tq), HEAD_DIM:] = jnp.where(lane == 0, 1.0, 0.0).astype(BF16)
    acc[...] = jnp.zeros_like(acc)

    r_id = lax.broadcasted_iota(jnp.int32, (tb, LANES), 0)
    c_id = lax.broadcasted_iota(jnp.int32, (tb, LANES), 1)
    n_ct = tb // LANES
    subs = range(nsub)

    def scores(s, kj):
        col0 = pl.multiple_of(kj * tb, tb)
        return _dot_nt(qa[s * tb:(s + 1) * tb, :], kbuf[pl.ds(col0, tb), :])

    def softmax_pv(s, kj, s2, m, masked):
        col0 = pl.multiple_of(kj * tb, tb)
        tiles = [s2[:, t * LANES:(t + 1) * LANES] for t in range(n_ct)]
        if masked:
            tiles = [jnp.where(c_id + t * LANES <= r_id, tiles[t], NEG_INF) for t in range(n_ct)]
        delta = (cstart[nsub * qi + s][0:1, :] - cstart[kj][0:1, :]) * LOG2E
        m_rel = m - delta
        tmax = functools.reduce(jnp.maximum, tiles)
        m_new = jnp.maximum(m_rel, jnp.max(tmax, axis=-1, keepdims=True))
        alpha = jnp.exp2(m_rel - m_new)
        p = jnp.concatenate([jnp.exp2(t - m_new).astype(BF16) for t in tiles], axis=1)
        pv = _dot(p, vbuf[pl.ds(col0, tb), :])
        a = acc[s]
        acc[s] = jnp.concatenate([alpha * a[:, 0:LANES], alpha * a[:, LANES:]], axis=1) + pv
        return m_new + delta

    for s in subs:
        sbuf[s] = scores(s, 0)

    def body(t, ms):
        kj = kv_per_trip * t
        cur = [sbuf[s] for s in subs]
        for u in range(kv_per_trip):
            nxt = [scores(s, kj + u + 1) for s in subs]
            ms = [softmax_pv(s, kj + u, cur[s], ms[s], False) for s in subs]
            cur = nxt
        for s in subs:
            sbuf[s] = cur[s]
        return tuple(ms)

    assert nsub % kv_per_trip == 0
    ms = lax.fori_loop(0, (nsub // kv_per_trip) * qi, body,
                       tuple(jnp.full((tb, LANES), NEG_INF, F32) for _ in subs))
    for s in subs:
        m = ms[s]
        for t in range(s + 1):
            s2 = sbuf[s] if t == 0 else scores(s, nsub * qi + t)
            m = softmax_pv(s, nsub * qi + t, s2, m, t == s)
        a = acc[s]
        o_ref[s * tb:(s + 1) * tb, :] = (a[:, 0:HEAD_DIM] / a[:, HEAD_DIM:HEAD_DIM + 1]).astype(BF16)


def _fox_prompt(fq, fk, fv, c_col, *, tb, nsub, n_heads):
    s, _ = fq.shape
    tq = tb * nsub
    assert s % tq == 0
    blk = lambda: pl.BlockSpec((tq, HEAD_DIM), lambda h, qi: (qi, h))
    kv = lambda: pl.BlockSpec((None, None, tq, HEAD_DIM), lambda h, qi: (0, h, qi, 0))
    return pl.pallas_call(
        functools.partial(_fox_prompt_kernel, tb=tb, nsub=nsub, kv_per_trip=nsub),
        grid=(n_heads, s // tq),
        in_specs=[blk(), kv(), kv(), pl.BlockSpec((tq, LANES), lambda h, qi: (qi, 0))],
        out_specs=blk(),
        out_shape=jax.ShapeDtypeStruct(fq.shape, BF16),
        scratch_shapes=[pltpu.VMEM((tq, 2 * HEAD_DIM), BF16),
                        pltpu.VMEM((s, 2 * HEAD_DIM), BF16),
                        pltpu.VMEM((s, 2 * HEAD_DIM), BF16),
                        pltpu.VMEM((s // tb, 8, LANES), F32),
                        pltpu.VMEM((nsub, tb, 2 * HEAD_DIM), F32),
                        pltpu.VMEM((nsub, tb, tb), F32)],
        compiler_params=_cparams(("arbitrary", "arbitrary")),
        name="fox_prompt",
    )(fq, fk, fv, c_col)


def _fox_sample_kernel(q_ref, kc_ref, vc_ref, kn_ref, vn_ref, lfc_ref, lfn_ref, ccol_ref,
                       tric_ref, trin_ref, o_ref, *, n_heads):
    c_cache = _split3_dot(lfc_ref[0], tric_ref[...], tri_first=False)
    p_len = c_cache.shape[1]
    ck_cache = c_cache - c_cache[:, p_len - 1:p_len]
    ck_new = _split3_dot(lfn_ref[0], trin_ref[...], tri_first=False)
    ccol = ccol_ref[...]
    n_new = ck_new.shape[1]
    r = lax.broadcasted_iota(jnp.int32, (n_new, n_new), 0)
    c = lax.broadcasted_iota(jnp.int32, (n_new, n_new), 1)
    for h in range(n_heads):
        sl = slice(h * HEAD_DIM, (h + 1) * HEAD_DIM)
        q = q_ref[:, sl]
        cq = ccol[:, h:h + 1]
        s_c = _dot_nt(q, kc_ref[0, h].astype(BF16)) + (cq - ck_cache[h:h + 1, :]) * LOG2E
        s_n = _dot_nt(q, kn_ref[0, h].astype(BF16)) + (cq - ck_new[h:h + 1, :]) * LOG2E
        s_n = jnp.where(c <= r, s_n, NEG_INF)
        m = jnp.maximum(jnp.max(s_c, axis=-1, keepdims=True), jnp.max(s_n, axis=-1, keepdims=True))
        p_c = jnp.exp2(s_c - m)
        p_n = jnp.exp2(s_n - m)
        l = jnp.sum(p_c, axis=-1, keepdims=True) + jnp.sum(p_n, axis=-1, keepdims=True)
        o = _dot(p_c.astype(BF16), vc_ref[0, h].astype(BF16)) + \
            _dot(p_n.astype(BF16), vn_ref[0, h].astype(BF16))
        o_ref[:, sl] = (o / l).astype(BF16)


def _fox_sample(fq, kc, vc, kn, vn, lfc_row, lfn_row, c_col, *, n_heads, seq_len):
    m, d_fox = fq.shape
    nb, _, p_len, _ = kc.shape
    tric = jnp.asarray(np.arange(p_len)[:, None] <= np.arange(p_len)[None, :], dtype=BF16)
    trin = jnp.asarray(np.arange(seq_len)[:, None] <= np.arange(seq_len)[None, :], dtype=BF16)
    row = lambda w: pl.BlockSpec((seq_len, w), lambda b: (b, 0))
    cache = pl.BlockSpec((1, n_heads, p_len, HEAD_DIM), lambda b: (b, 0, 0, 0))
    new = pl.BlockSpec((1, n_heads, seq_len, HEAD_DIM), lambda b: (b, 0, 0, 0))
    return pl.pallas_call(
        functools.partial(_fox_sample_kernel, n_heads=n_heads),
        grid=(nb,),
        in_specs=[row(d_fox), cache, cache, new, new,
                  pl.BlockSpec((1, HEAD_ROWS, p_len), lambda b: (b, 0, 0)),
                  pl.BlockSpec((1, HEAD_ROWS, seq_len), lambda b: (b, 0, 0)),
                  row(LANES), _resident(tric.shape), _resident(trin.shape)],
        out_specs=row(d_fox),
        out_shape=jax.ShapeDtypeStruct((m, d_fox), BF16),
        compiler_params=_cparams(("arbitrary",)),
        name="fox_sample",
    )(fq, kc, vc, kn, vn, lfc_row, lfn_row, c_col, tric, trin)


def _log_gamma(h):
    return float(np.log1p(-np.exp2(np.float32(-5.0 - h)), dtype=np.float32))


def _retention_kernel(q_ref, k_ref, v_ref, g_ref, s0_ref, gn_ref, o_ref, s_ref, decay_ref, *, n_heads,
                      t_blk):
    step = pl.program_id(1)

    @pl.when(step == 0)
    def _():
        s_ref[...] = s0_ref[...]

    @pl.when(jnp.logical_and(pl.program_id(0) == 0, step == 0))
    def _():
        ti = lax.broadcasted_iota(jnp.int32, (t_blk, t_blk), 0)
        si = lax.broadcasted_iota(jnp.int32, (t_blk, t_blk), 1)
        dist = jnp.abs(ti - si).astype(F32)
        shift = CHUNK.bit_length() - 1
        visible = jnp.right_shift(si, shift) <= jnp.right_shift(ti, shift)
        for h in range(n_heads):
            decay_ref[h] = jnp.where(visible, jnp.exp(_log_gamma(h) * dist), 0.0)

    tcol = lax.broadcasted_iota(jnp.int32, (t_blk, 1), 0).astype(F32)
    for h in range(n_heads):
        lg = _log_gamma(h)
        sl = slice(h * HEAD_DIM, (h + 1) * HEAD_DIM)
        q = q_ref[:, sl]
        k = k_ref[:, sl]
        v = v_ref[:, sl]
        a = _dot_nt(q, k) * decay_ref[h]
        state = s_ref[0, h]
        o = _dot(a.astype(BF16), v) + _dot(q, state.astype(BF16)) * jnp.exp(lg * (tcol + 1.0))
        kd = (k.astype(F32) * jnp.exp(lg * (t_blk - 1.0 - tcol))).astype(BF16)
        s_ref[0, h] = math.exp(lg * t_blk) * state + _dot_tn(kd, v)
        mu = jnp.mean(o, axis=-1, keepdims=True)
        oc = o - mu
        var = jnp.mean(oc * oc, axis=-1, keepdims=True)
        rn = oc * lax.rsqrt(var + GN_EPS) * gn_ref[:, sl]
        gate = g_ref[:, sl].astype(F32)
        o_ref[:, sl] = (gate / (1.0 + jnp.exp(-gate)) * rn).astype(BF16)


def _retention(rq, rk, rv, rg, s0, gn_g, *, t_blk, seq_len, n_heads):
    m, d_ret = rq.shape
    nb = m // seq_len
    steps = seq_len // t_blk
    row = lambda: pl.BlockSpec((t_blk, d_ret), lambda b, s: (b * steps + s, 0))
    st = lambda: pl.BlockSpec((1, n_heads, HEAD_DIM, HEAD_DIM), lambda b, s: (b, 0, 0, 0))
    return pl.pallas_call(
        functools.partial(_retention_kernel, n_heads=n_heads, t_blk=t_blk),
        grid=(nb, steps),
        in_specs=[row(), row(), row(), row(), st(), pl.BlockSpec((1, d_ret), lambda b, s: (0, 0))],
        out_specs=(row(), st()),
        out_shape=(jax.ShapeDtypeStruct((m, d_ret), BF16),
                   jax.ShapeDtypeStruct((nb, n_heads, HEAD_DIM, HEAD_DIM), F32)),
        scratch_shapes=[pltpu.VMEM((n_heads, t_blk, t_blk), F32)],
        compiler_params=_cparams(("arbitrary", "arbitrary")),
        name="retention",
    )(rq, rk, rv, rg, s0, gn_g)


def _mix_out_kernel(x_ref, fo_ref, ro_ref, mo_ref, wo_ref, g_ref, b_ref, o_ref, cat_ref, *, alpha):
    d_fox = fo_ref.shape[1]
    d_ret = ro_ref.shape[1]
    cat_ref[:, 0:d_fox] = fo_ref[...]
    cat_ref[:, d_fox:d_fox + d_ret] = ro_ref[...]
    cat_ref[:, d_fox + d_ret:] = mo_ref[...]
    tm = x_ref.shape[0]
    pieces = [slice(0, tm // 2), slice(tm // 2, tm)] if tm % 32 == 0 else [slice(0, tm)]
    mix = [_dot(cat_ref[r, :], wo_ref[...]) for r in pieces]
    for r, mx in zip(pieces, mix):
        o_ref[r, :] = _layernorm(alpha * x_ref[r, :] + mx, g_ref[...], b_ref[...])


def _mix_out(x, fo, ro, mo, wo, g, b, *, tm, alpha):
    m, d_model = x.shape
    row = lambda w: pl.BlockSpec((tm, w), lambda i: (i, 0))
    return pl.pallas_call(
        functools.partial(_mix_out_kernel, alpha=alpha),
        grid=(m // tm,),
        in_specs=[row(d_model), row(fo.shape[1]), row(ro.shape[1]), row(mo.shape[1]),
                  _resident(wo.shape), _resident(g.shape), _resident(b.shape)],
        out_specs=row(d_model),
        out_shape=jax.ShapeDtypeStruct((m, d_model), F32),
        scratch_shapes=[pltpu.VMEM((tm, wo.shape[0]), BF16)],
        compiler_params=_cparams(("arbitrary",)),
        name="mix_out",
    )(x, fo, ro, mo, wo, g, b)


def _gelu_tanh(x):
    c = math.sqrt(2.0 / math.pi)
    return x * (0.5 * (1.0 + jnp.tanh(c * (x + 0.044715 * (x * x * x)))))


def _conv_ffn_kernel(h_ref, wg_ref, wu_ref, wd_ref, cw_ref, cb_ref, past_ref, g_ref, b_ref,
                     y_ref, cv_ref, hb_ref, gbuf_ref, carry_ref, *, alpha, seq_len, tm, tc):
    i = pl.program_id(0)
    f = pl.program_id(1)
    nf = pl.num_programs(1)
    lt = min(seq_len, tm)
    ns = tm // lt
    halo = 8
    carried = seq_len > tm

    @pl.when(f == 0)
    def _():
        h = h_ref[...]
        hb_ref[...] = h.astype(BF16)
        y_ref[...] = alpha * h

    if carried:
        @pl.when(jnp.logical_and(i == 0, f == 0))
        def _():
            carry_ref[...] = jnp.zeros_like(carry_ref)

        tiles_per_seq = seq_len // tm
        gbuf_ref[0, halo - 2:halo, :] = jnp.where(i % tiles_per_seq == 0, past_ref[i // tiles_per_seq],
                                                  carry_ref[f])
    else:
        for s in range(ns):
            gbuf_ref[s, halo - 2:halo, :] = past_ref[i * ns + s]

    hb = hb_ref[...]
    hid = []
    for c0 in range(0, wg_ref.shape[1], tc):
        cs = slice(c0, c0 + tc)
        gate = _dot(hb, wg_ref[:, cs])
        up = _dot(hb, wu_ref[:, cs])
        w0 = cw_ref[0:1, cs]
        w1 = cw_ref[1:2, cs]
        w2 = cw_ref[2:3, cs]
        cb = cb_ref[:, cs]
        parts = []
        for s in range(ns):
            gs = gate[s * lt:(s + 1) * lt]
            gbuf_ref[s, halo:halo + lt, cs] = gs
            g1 = gbuf_ref[s, halo - 1:halo - 1 + lt, cs]
            g2 = gbuf_ref[s, halo - 2:halo - 2 + lt, cs]
            gc = cb + w0 * g2 + w1 * g1 + w2 * gs
            parts.append((_gelu_tanh(gc) * up[s * lt:(s + 1) * lt]).astype(BF16))
        hid.append(parts[0] if ns == 1 else jnp.concatenate(parts, axis=0))
    hid = hid[0] if len(hid) == 1 else jnp.concatenate(hid, axis=1)
    y_ref[...] += _dot(hid, wd_ref[...])

    for s in range(ns):
        last2 = gbuf_ref[s, halo + lt - 2:halo + lt, :]
        cv_ref[s] = last2
        if carried:
            carry_ref[f] = last2

    @pl.when(f == nf - 1)
    def _():
        y_ref[...] = _layernorm(y_ref[...], g_ref[...], b_ref[...])


def _conv_ffn(h1, wg, wu, wd, conv_w, conv_b, past, g, b, *, tm, tf, seq_len, alpha):
    tc = 2 * LANES
    assert tf % tc == 0
    m, d_model = h1.shape
    d_ff = wg.shape[1]
    n_seq = past.shape[0]
    assert m % tm == 0 and d_ff % tf == 0 and (seq_len % tm == 0 or tm % seq_len == 0)
    nf = d_ff // tf
    lt = min(seq_len, tm)
    y, cv_all = pl.pallas_call(
        functools.partial(_conv_ffn_kernel, alpha=alpha, seq_len=seq_len, tm=tm, tc=tc),
        grid=(m // tm, nf),
        in_specs=[pl.BlockSpec((tm, d_model), lambda i, f: (i, 0), pipeline_mode=pl.Buffered(1)),
                  pl.BlockSpec((d_model, tf), lambda i, f: (0, f)),
                  pl.BlockSpec((d_model, tf), lambda i, f: (0, f)),
                  pl.BlockSpec((tf, d_model), lambda i, f: (f, 0)),
                  pl.BlockSpec((CONV_W, tf), lambda i, f: (0, f)),
                  pl.BlockSpec((1, tf), lambda i, f: (0, f)),
                  pl.BlockSpec((n_seq, CONV_W - 1, tf), lambda i, f: (0, 0, f)),
                  pl.BlockSpec((1, d_model), lambda i, f: (0, 0)),
                  pl.BlockSpec((1, d_model), lambda i, f: (0, 0))],
        out_specs=(pl.BlockSpec((tm, d_model), lambda i, f: (i, 0)),
                   pl.BlockSpec((tm // lt, CONV_W - 1, tf), lambda i, f: (i, 0, f))),
        out_shape=(jax.ShapeDtypeStruct((m, d_model), F32),
                   jax.ShapeDtypeStruct((m // lt, CONV_W - 1, d_ff), F32)),
        scratch_shapes=[pltpu.VMEM((tm, d_model), BF16),
                        pltpu.VMEM((tm // lt, 8 + lt, tf), F32),
                        pltpu.VMEM((nf, CONV_W - 1, tf), F32)],
        compiler_params=_cparams(("arbitrary", "arbitrary")),
        name="conv_ffn",
    )(h1, wg, wu, wd, conv_w, conv_b, past, g, b)
    return y, cv_all.reshape(n_seq, seq_len // lt, CONV_W - 1, d_ff)[:, -1]


def _rope_tables(tile_base, tm):
    half = HEAD_DIM // 2
    inv = ROPE_BASE ** (-jnp.arange(half, dtype=F32) / half)
    inv = jnp.concatenate([inv, inv])
    sign = jnp.concatenate([-jnp.ones((half,), F32), jnp.ones((half,), F32)])
    a = tile_base.astype(F32)[:, None] * inv[None, :]
    b = jnp.arange(tm, dtype=F32)[:, None] * inv[None, :]
    base_tab = jnp.stack([jnp.cos(a), jnp.sin(a)], axis=1)
    base_tab = jnp.pad(base_tab, ((0, 0), (0, 6), (0, 0)))
    rot_tab = jnp.stack([jnp.cos(b), jnp.sin(b), sign * jnp.cos(b), sign * jnp.sin(b)])
    return base_tab, rot_tab


def _layer(x, seq_start, seq_len, mk, mv, w, *, alpha, tm_proj, tm_mix, tm_ffn, tf, t_ret, s0, conv_past,
           fox_fn):
    tile_base = (seq_start[:, None] + jnp.arange(0, seq_len, tm_proj)[None, :]).reshape(-1)
    base_tab, rot_tab = _rope_tables(tile_base, tm_proj)
    fq, fk, fv, lf, c, rq, rk, rv, rg, mo = _in_proj(
        x, w["wa"], w["wf"], w["bf"], base_tab, rot_tab, mk, mv, tm=tm_proj, seq_len=seq_len,
        d_fox=w["d_fox"], d_ret=w["d_ret"], d_memq=w["d_memq"])
    fo = fox_fn(fq, fk, fv, lf, c)
    ro, s_new = _retention(rq, rk, rv, rg, s0, w["gn_g"], t_blk=t_ret, seq_len=seq_len,
                           n_heads=w["d_ret"] // HEAD_DIM)
    h1 = _mix_out(x, fo, ro, mo, w["wo"], w["ln1_g"], w["ln1_b"], tm=tm_mix, alpha=alpha)
    y, cv = _conv_ffn(h1, w["wg"], w["wu"], w["wd"], w["conv_w"], w["conv_b"], conv_past,
                      w["ln2_g"], w["ln2_b"], tm=tm_ffn, tf=tf, seq_len=seq_len, alpha=alpha)
    return y, fk, fv, lf, s_new, cv


def kernel(x_prompt, x_sample, cache_fox_k, cache_fox_v, cache_fox_logf, state_ret, cache_mem_k,
           cache_mem_v, state_conv, mem_prompt, w_in, b_f, w_mem_kv, ret_gn_g, w_o, ln1_g, ln1_b,
           w_gate, w_up, conv_w, conv_b, w_down, ln2_g, ln2_b):
    batch, seq, d_model = x_prompt.shape
    dec_batch, dec_seq, _ = x_sample.shape
    depth = w_in.shape[0]
    past_len = cache_fox_k.shape[2]
    n_fox = cache_fox_k.shape[3]
    n_ret = state_ret.shape[2]
    n_mem = cache_mem_k.shape[3]
    n_mem_tok = cache_mem_k.shape[2]
    d_fox, d_ret, d_memq = n_fox * HEAD_DIM, n_ret * HEAD_DIM, n_mem * HEAD_DIM
    d_ff = w_gate.shape[2]
    assert batch == 1 and depth == 1 and n_fox <= 8
    alpha = (2.0 * depth) ** 0.25
    l = 0

    o_ff = 3 * d_fox
    w_fm = jnp.transpose(w_in, (2, 0, 1))
    wa, wf = _w_prep(w_fm, o_ff=o_ff, n_fox=n_fox, tn=2 * LANES)
    bfp = jnp.pad(b_f[l], (0, LANES - n_fox)).reshape(1, LANES)
    w = dict(wa=wa, wf=wf, bf=bfp, d_fox=d_fox, d_ret=d_ret, d_memq=d_memq,
             gn_g=ret_gn_g[l].reshape(1, d_ret), wo=w_o[l].astype(BF16),
             ln1_g=ln1_g[l].reshape(1, d_model), ln1_b=ln1_b[l].reshape(1, d_model),
             wg=w_gate[l].astype(BF16), wu=w_up[l].astype(BF16), wd=w_down[l].astype(BF16),
             conv_w=conv_w[l], conv_b=conv_b[l].reshape(1, d_ff),
             ln2_g=ln2_g[l].reshape(1, d_model), ln2_b=ln2_b[l].reshape(1, d_model))

    to_head_major = lambda t: jnp.swapaxes(t, -3, -2)
    mkv = _mem_kv(mem_prompt.reshape(n_mem_tok, d_model), w_mem_kv[l].astype(BF16))
    mk_p, mv_p = mkv[None, :n_mem], mkv[None, n_mem:]

    def fox_prompt_fn(fq, fk, fv, lf, c):
        return _fox_prompt(fq, fk, fv, c, tb=512, nsub=4, n_heads=n_fox)

    y_p, fk_p, fv_p, lf_p, s_p, cv_p = _layer(
        x_prompt.reshape(seq, d_model), jnp.zeros((batch,), jnp.int32), seq, mk_p, mv_p, w,
        alpha=alpha, tm_proj=256, tm_mix=512, tm_ffn=1024, tf=512, t_ret=256,
        s0=jnp.zeros((1, n_ret, HEAD_DIM, HEAD_DIM), F32),
        conv_past=jnp.zeros((1, CONV_W - 1, d_ff), F32), fox_fn=fox_prompt_fn)

    m_s = dec_batch * dec_seq
    kc = to_head_major(cache_fox_k[l])
    vc = to_head_major(cache_fox_v[l])
    lfc_row = jnp.pad(jnp.swapaxes(cache_fox_logf[l], 1, 2), ((0, 0), (0, HEAD_ROWS - n_fox), (0, 0)))

    def fox_sample_fn(fq, fk, fv, lf, c):
        lfn_row = jnp.swapaxes(lf.reshape(dec_batch, dec_seq, LANES)[:, :, :HEAD_ROWS], 1, 2)
        return _fox_sample(fq, kc, vc, fk, fv, lfc_row, lfn_row, c, n_heads=n_fox, seq_len=dec_seq)

    y_s, fk_s, fv_s, lf_s, s_s, cv_s = _layer(
        x_sample.reshape(m_s, d_model), jnp.full((dec_batch,), past_len, jnp.int32), dec_seq,
        to_head_major(cache_mem_k[l]), to_head_major(cache_mem_v[l]), w,
        alpha=alpha, tm_proj=dec_seq, tm_mix=m_s, tm_ffn=m_s, tf=512, t_ret=dec_seq,
        s0=state_ret[l], conv_past=state_conv[l], fox_fn=fox_sample_fn)

    return (y_p.reshape(batch, seq, d_model), y_s.reshape(dec_batch, dec_seq, d_model),
            to_head_major(fk_p)[None], to_head_major(fv_p)[None],
            lf_p[:, :n_fox].reshape(1, batch, seq, n_fox),
            s_p.reshape(1, batch, n_ret, HEAD_DIM, HEAD_DIM),
            to_head_major(mk_p)[None], to_head_major(mv_p)[None],
            cv_p.reshape(1, batch, CONV_W - 1, d_ff),
            to_head_major(fk_s)[None], to_head_major(fv_s)[None],
            lf_s[:, :n_fox].reshape(1, dec_batch, dec_seq, n_fox),
            s_s.reshape(1, dec_batch, n_ret, HEAD_DIM, HEAD_DIM),
            cv_s.reshape(1, dec_batch, CONV_W - 1, d_ff))
```

```python
import functools
import math

import numpy as np
import jax
import jax.numpy as jnp
from jax import lax
from jax.experimental import pallas as pl
from jax.experimental.pallas import tpu as pltpu

F32 = jnp.float32
BF16 = jnp.bfloat16

HEAD_DIM = 128
CHUNK = 64
CONV_W = 3
ROPE_BASE = 10000.0
LN_EPS = 1e-5
GN_EPS = 1e-5
LANES = 128
V7X_VMEM_BYTES = 64 * 1024 * 1024
VMEM_LIMIT = V7X_VMEM_BYTES - 4 * 1024 * 1024
SCALE = HEAD_DIM ** -0.5
LOG2E = math.log2(math.e)
NEG_INF = float("-inf")
HEAD_ROWS = 16


def _cparams(sem):
    return pltpu.CompilerParams(dimension_semantics=sem, vmem_limit_bytes=VMEM_LIMIT)


def _resident(shape):
    nd = len(shape)
    return pl.BlockSpec(shape, lambda *_: (0,) * nd, pipeline_mode=pl.Buffered(1))


def _dot(a, b):
    return jnp.dot(a, b, preferred_element_type=F32)


def _dot_nt(a, b):
    return lax.dot_general(a, b, (((1,), (1,)), ((), ())), preferred_element_type=F32)


def _dot_tn(a, b):
    return lax.dot_general(a, b, (((0,), (0,)), ((), ())), preferred_element_type=F32)


def _split3_dot(a, tri, *, tri_first):
    hi = a.astype(BF16)
    r1 = a - hi.astype(F32)
    mid = r1.astype(BF16)
    lo = (r1 - mid.astype(F32)).astype(BF16)
    if tri_first:
        return _dot(tri, hi) + _dot(tri, mid) + _dot(tri, lo)
    return _dot(hi, tri) + _dot(mid, tri) + _dot(lo, tri)


def _layernorm(z, g, b):
    mu = jnp.mean(z, axis=-1, keepdims=True)
    zc = z - mu
    var = jnp.mean(zc * zc, axis=-1, keepdims=True)
    return zc * lax.rsqrt(var + LN_EPS) * g + b


def _memkv_kernel(m_ref, w_ref, o_ref):
    kv = _dot(m_ref[...].astype(BF16), w_ref[...])
    for j in range(o_ref.shape[0]):
        o_ref[j] = kv[:, j * HEAD_DIM:(j + 1) * HEAD_DIM]


def _mem_kv(mem, w_bf):
    m, _ = mem.shape
    return pl.pallas_call(
        _memkv_kernel,
        out_shape=jax.ShapeDtypeStruct((w_bf.shape[1] // HEAD_DIM, m, HEAD_DIM), F32),
        compiler_params=pltpu.CompilerParams(vmem_limit_bytes=VMEM_LIMIT),
        name="mem_kv",
    )(mem, w_bf)


def _w_prep_kernel(w_hbm, wa_ref, wf_ref, buf, sem, *, o_ff, n_fox, tn):
    j = pl.program_id(0)
    n = pl.num_programs(0)

    def copy(blk, slot):
        r0 = jnp.where(blk * tn < o_ff, blk * tn, blk * tn + n_fox)
        return pltpu.make_async_copy(w_hbm.at[pl.ds(r0, tn)], buf.at[slot], sem.at[slot])

    @pl.when(j == 0)
    def _():
        gate = pltpu.make_async_copy(w_hbm.at[pl.ds(o_ff, tn)], buf.at[1], sem.at[1])
        gate.start()
        copy(0, 0).start()
        gate.wait()
        rows = buf[1, 0:LANES, 0, :]
        keep = lax.broadcasted_iota(jnp.int32, rows.shape, 0) < n_fox
        wf_ref[...] = jnp.where(keep, rows, 0.0).T.astype(BF16)

    @pl.when(j + 1 < n)
    def _():
        copy(j + 1, (j + 1) % 2).start()

    copy(j, j % 2).wait()
    wa_ref[...] = buf[j % 2, :, 0, :].T.astype(BF16)


def _w_prep(w_fm, *, o_ff, n_fox, tn):
    n, _, d_model = w_fm.shape
    assert o_ff % tn == 0 and (n - n_fox) % tn == 0 and tn >= LANES and o_ff + tn <= n
    return pl.pallas_call(
        functools.partial(_w_prep_kernel, o_ff=o_ff, n_fox=n_fox, tn=tn),
        grid=((n - n_fox) // tn,),
        in_specs=[pl.BlockSpec(memory_space=pl.ANY)],
        out_specs=(pl.BlockSpec((d_model, tn), lambda j: (0, j)),
                   pl.BlockSpec((d_model, LANES), lambda j: (0, 0))),
        out_shape=(jax.ShapeDtypeStruct((d_model, n - n_fox), BF16),
                   jax.ShapeDtypeStruct((d_model, LANES), BF16)),
        scratch_shapes=[pltpu.VMEM((2, tn, 1, d_model), F32), pltpu.SemaphoreType.DMA((2,))],
        compiler_params=_cparams(("arbitrary",)),
        name="w_prep",
    )(w_fm)


def _in_proj_kernel(x_ref, wa_ref, wf_ref, bf_ref, base_ref, rot_ref, mk_ref, mv_ref, tri_ref,
                    fq_ref, fk_ref, fv_ref, lf_ref, c_ref, rq_ref, rk_ref, rv_ref, rg_ref, mo_ref,
                    qa_ref, ka_ref, va_ref, carry_ref, start_ref, *, d_fox, d_ret, d_memq, tiles_per_seq,
                    blk_tiles):
    i = pl.program_id(0)
    xb = x_ref[...].astype(BF16)

    def proj(lo, width):
        return _dot(xb, wa_ref[:, lo:lo + width])

    o_fk = d_fox
    o_fv = 2 * d_fox
    o_rq = 3 * d_fox
    o_rk = o_rq + d_ret
    o_rv = o_rk + d_ret
    o_rg = o_rv + d_ret
    o_mq = o_rg + d_ret

    zfq = proj(0, d_fox) * (SCALE * LOG2E)
    fq_ref[...] = zfq.astype(BF16)
    zfk = proj(o_fk, d_fox)
    zfv = proj(o_fv, d_fox)
    for h in range(d_fox // HEAD_DIM):
        sl = slice(h * HEAD_DIM, (h + 1) * HEAD_DIM)
        fk_ref[0, h] = zfk[:, sl]
        fv_ref[0, h] = zfv[:, sl]

    z = _dot(xb, wf_ref[...]) + bf_ref[...]
    lf = jnp.minimum(z, 0.0) - jnp.log1p(jnp.exp(-jnp.abs(z)))
    lf_ref[...] = lf

    @pl.when(i % tiles_per_seq == 0)
    def _():
        carry_ref[...] = jnp.zeros_like(carry_ref)

    c = _split3_dot(lf, tri_ref[...], tri_first=True) + carry_ref[0:1, :]
    c_ref[...] = c
    tm = c.shape[0]
    carry_ref[...] = jnp.broadcast_to(c[tm - 1:tm, :], carry_ref.shape)

    @pl.when(i % blk_tiles == 0)
    def _():
        start_ref[...] = jnp.broadcast_to(c[0:1, :], start_ref.shape)

    rel = (c - start_ref[0:1, :]) * LOG2E
    for h in range(d_fox // HEAD_DIM):
        sl = slice(h * HEAD_DIM, (h + 1) * HEAD_DIM)
        qa_ref[0, h], ka_ref[0, h], va_ref[0, h] = _fox_operands(zfq[:, sl], zfk[:, sl], zfv[:, sl], rel, h)

    ca = base_ref[0, 0:1, :]
    sa = base_ref[0, 1:2, :]
    cos = ca * rot_ref[0] - sa * rot_ref[1]
    sin = sa * rot_ref[2] + ca * rot_ref[3]

    def rope(t):
        return t * cos + pltpu.roll(t, HEAD_DIM // 2, axis=1) * sin

    zq = proj(o_rq, d_ret)
    zk = proj(o_rk, d_ret)
    for h in range(d_ret // HEAD_DIM):
        sl = slice(h * HEAD_DIM, (h + 1) * HEAD_DIM)
        rq_ref[:, sl] = rope(zq[:, sl]).astype(BF16)
        rk_ref[:, sl] = (rope(zk[:, sl]) * SCALE).astype(BF16)
    rv_ref[...] = proj(o_rv, d_ret).astype(BF16)
    rg_ref[...] = proj(o_rg, d_ret).astype(BF16)

    zm = proj(o_mq, d_memq)
    for h in range(d_memq // HEAD_DIM):
        sl = slice(h * HEAD_DIM, (h + 1) * HEAD_DIM)
        q = (zm[:, sl] * SCALE).astype(BF16)
        s = _dot_nt(q, mk_ref[0, h].astype(BF16))
        p = jnp.exp(s - jnp.max(s, axis=-1, keepdims=True))
        l = jnp.sum(p, axis=-1, keepdims=True)
        o = _dot(p.astype(BF16), mv_ref[0, h].astype(BF16))
        mo_ref[:, sl] = (o / l).astype(BF16)


def _in_proj(x, wa, wf, bfp, base_tab, rot_tab, mk, mv, *, tm, seq_len, d_fox, d_ret, d_memq, attn_block):
    m, d_model = x.shape
    assert m % tm == 0 and seq_len % tm == 0
    assert base_tab.shape == (m // tm, 8, HEAD_DIM) and rot_tab.shape == (4, tm, HEAD_DIM)
    tiles_per_seq = seq_len // tm
    n_fox = d_fox // HEAD_DIM
    tri = (np.arange(tm)[:, None] >= np.arange(tm)[None, :])
    tri = jnp.asarray(tri, dtype=BF16)
    row = lambda w: pl.BlockSpec((tm, w), lambda i: (i, 0))
    memspec = pl.BlockSpec((1,) + mk.shape[1:], lambda i: (i // tiles_per_seq, 0, 0, 0))
    kvspec = pl.BlockSpec((1, n_fox, tm, HEAD_DIM), lambda i: (i // tiles_per_seq, 0, i % tiles_per_seq, 0))
    augspec = pl.BlockSpec((1, n_fox, tm, 2 * HEAD_DIM),
                           lambda i: (i // tiles_per_seq, 0, i % tiles_per_seq, 0))
    aug = jax.ShapeDtypeStruct((m // seq_len, n_fox, seq_len, 2 * HEAD_DIM), BF16)
    assert attn_block % tm == 0 and seq_len % attn_block == 0
    kern = functools.partial(_in_proj_kernel, d_fox=d_fox, d_ret=d_ret, d_memq=d_memq,
                             tiles_per_seq=tiles_per_seq, blk_tiles=attn_block // tm)
    out_shape = (
        jax.ShapeDtypeStruct((m, d_fox), BF16),
        jax.ShapeDtypeStruct((m // seq_len, n_fox, seq_len, HEAD_DIM), F32),
        jax.ShapeDtypeStruct((m // seq_len, n_fox, seq_len, HEAD_DIM), F32),
        jax.ShapeDtypeStruct((m, LANES), F32),
        jax.ShapeDtypeStruct((m, LANES), F32),
        jax.ShapeDtypeStruct((m, d_ret), BF16),
        jax.ShapeDtypeStruct((m, d_ret), BF16),
        jax.ShapeDtypeStruct((m, d_ret), BF16),
        jax.ShapeDtypeStruct((m, d_ret), BF16),
        jax.ShapeDtypeStruct((m, d_memq), BF16),
        aug, aug, aug,
    )
    return pl.pallas_call(
        kern,
        grid=(m // tm,),
        in_specs=[row(d_model), _resident(wa.shape), _resident(wf.shape), _resident(bfp.shape),
                  pl.BlockSpec((1, 8, HEAD_DIM), lambda i: (i, 0, 0)), _resident(rot_tab.shape),
                  memspec, memspec, _resident(tri.shape)],
        out_specs=(row(d_fox), kvspec, kvspec, row(LANES), row(LANES),
                   row(d_ret), row(d_ret), row(d_ret), row(d_ret), row(d_memq),
                   augspec, augspec, augspec),
        out_shape=out_shape,
        scratch_shapes=[pltpu.VMEM((8, LANES), F32), pltpu.VMEM((8, LANES), F32)],
        compiler_params=_cparams(("arbitrary",)),
        name="in_proj",
    )(x, wa, wf, bfp, base_tab, rot_tab, mk, mv, tri)


def _split3_f32(a):
    hi = a.astype(BF16).astype(F32)
    r1 = a - hi
    mid = r1.astype(BF16).astype(F32)
    lo = (r1 - mid).astype(BF16).astype(F32)
    return hi, mid, lo


def _fox_operands(q, k, v, rel, h):
    lane = lax.broadcasted_iota(jnp.int32, q.shape, 1)
    hi, mid, lo = (t[:, h:h + 1] for t in _split3_f32(rel))
    qbias = jnp.where(lane == 0, hi, jnp.where(lane == 1, mid, jnp.where(lane == 2, lo,
                      jnp.where(lane < 6, 1.0, 0.0))))
    kbias = jnp.where(lane < 3, 1.0, jnp.where(lane == 3, -hi, jnp.where(lane == 4, -mid,
                      jnp.where(lane == 5, -lo, 0.0))))
    ones_col = jnp.where(lane == 0, 1.0, 0.0)
    cat = lambda a, b: jnp.concatenate([a.astype(BF16), b.astype(BF16)], axis=1)
    return cat(q, qbias), cat(k, kbias), cat(v, ones_col)


def _fox_prompt_kernel(qa, k_ref, v_ref, ccol_ref, o_ref, kbuf, vbuf, cstart, acc, sbuf, *, tb, nsub,
                       kv_per_trip):
    h = pl.program_id(0)
    qi = pl.program_id(1)
    tq = tb * nsub
    row0 = pl.multiple_of(qi * tq, tq)
    kbuf[pl.ds(row0, tq), :] = k_ref[...]
    vbuf[pl.ds(row0, tq), :] = v_ref[...]
    lane = lax.broadcasted_iota(jnp.int32, (8, LANES), 1)
    for s in range(nsub):
        start = jnp.sum(jnp.where(lane == h, ccol_ref[s * tb:s * tb + 8, :], 0.0), axis=1, keepdims=True)
        cstart[nsub * qi + s] = jnp.broadcast_to(start[0:1], (8, LANES))
    acc[...] = jnp.zeros_like(acc)

    r_id = lax.broadcasted_iota(jnp.int32, (tb, LANES), 0)
    c_id = lax.broadcasted_iota(jnp.int32, (tb, LANES), 1)
    n_ct = tb // LANES
    subs = range(nsub)

    def scores(s, kj):
        col0 = pl.multiple_of(kj * tb, tb)
        return _dot_nt(qa[s * tb:(s + 1) * tb, :], kbuf[pl.ds(col0, tb), :])

    def softmax_pv(s, kj, s2, m, masked):
        col0 = pl.multiple_of(kj * tb, tb)
        tiles = [s2[:, t * LANES:(t + 1) * LANES] for t in range(n_ct)]
        if masked:
            tiles = [jnp.where(c_id + t * LANES <= r_id, tiles[t], NEG_INF) for t in range(n_ct)]
        delta = (cstart[nsub * qi + s][0:1, :] - cstart[kj][0:1, :]) * LOG2E
        m_rel = m - delta
        tmax = functools.reduce(jnp.maximum, tiles)
        m_new = jnp.maximum(m_rel, jnp.max(tmax, axis=-1, keepdims=True))
        alpha = jnp.exp2(m_rel - m_new)
        p = jnp.concatenate([jnp.exp2(t - m_new).astype(BF16) for t in tiles], axis=1)
        pv = _dot(p, vbuf[pl.ds(col0, tb), :])
        a = acc[s]
        acc[s] = jnp.concatenate([alpha * a[:, 0:LANES], alpha * a[:, LANES:]], axis=1) + pv
        return m_new + delta

    for s in subs:
        sbuf[s] = scores(s, 0)

    def body(t, ms):
        kj = kv_per_trip * t
        cur = [sbuf[s] for s in subs]
        for u in range(kv_per_trip):
            nxt = [scores(s, kj + u + 1) for s in subs]
            ms = [softmax_pv(s, kj + u, cur[s], ms[s], False) for s in subs]
            cur = nxt
        for s in subs:
            sbuf[s] = cur[s]
        return tuple(ms)

    assert nsub % kv_per_trip == 0
    ms = lax.fori_loop(0, (nsub // kv_per_trip) * qi, body,
                       tuple(jnp.full((tb, LANES), NEG_INF, F32) for _ in subs))
    for s in subs:
        m = ms[s]
        for t in range(s + 1):
            s2 = sbuf[s] if t == 0 else scores(s, nsub * qi + t)
            m = softmax_pv(s, nsub * qi + t, s2, m, t == s)
        a = acc[s]
        o_ref[s * tb:(s + 1) * tb, :] = (a[:, 0:HEAD_DIM] / a[:, HEAD_DIM:HEAD_DIM + 1]).astype(BF16)


def _fox_prompt(qa, ka, va, c_col, *, tb, nsub):
    _, n_heads, s, width = qa.shape
    tq = tb * nsub
    assert s % tq == 0
    op = lambda: pl.BlockSpec((None, None, tq, width), lambda h, qi: (0, h, qi, 0))
    return pl.pallas_call(
        functools.partial(_fox_prompt_kernel, tb=tb, nsub=nsub, kv_per_trip=4),
        grid=(n_heads, s // tq),
        in_specs=[op(), op(), op(), pl.BlockSpec((tq, LANES), lambda h, qi: (qi, 0))],
        out_specs=pl.BlockSpec((tq, HEAD_DIM), lambda h, qi: (qi, h)),
        out_shape=jax.ShapeDtypeStruct((s, n_heads * HEAD_DIM), BF16),
        scratch_shapes=[pltpu.VMEM((s, width), BF16),
                        pltpu.VMEM((s, width), BF16),
                        pltpu.VMEM((s // tb, 8, LANES), F32),
                        pltpu.VMEM((nsub, tb, width), F32),
                        pltpu.VMEM((nsub, tb, tb), F32)],
        compiler_params=_cparams(("arbitrary", "arbitrary")),
        name="fox_prompt",
    )(qa, ka, va, c_col)


def _fox_sample_kernel(q_ref, kc_ref, vc_ref, kn_ref, vn_ref, lfc_ref, lfn_ref, ccol_ref,
                       tric_ref, trin_ref, o_ref, *, n_heads):
    c_cache = _split3_dot(lfc_ref[0], tric_ref[...], tri_first=False)
    p_len = c_cache.shape[1]
    ck_cache = c_cache - c_cache[:, p_len - 1:p_len]
    ck_new = _split3_dot(lfn_ref[0], trin_ref[...], tri_first=False)
    ccol = ccol_ref[...]
    n_new = ck_new.shape[1]
    r = lax.broadcasted_iota(jnp.int32, (n_new, n_new), 0)
    c = lax.broadcasted_iota(jnp.int32, (n_new, n_new), 1)
    for h in range(n_heads):
        sl = slice(h * HEAD_DIM, (h + 1) * HEAD_DIM)
        q = q_ref[:, sl]
        cq = ccol[:, h:h + 1]
        s_c = _dot_nt(q, kc_ref[0, h].astype(BF16)) + (cq - ck_cache[h:h + 1, :]) * LOG2E
        s_n = _dot_nt(q, kn_ref[0, h].astype(BF16)) + (cq - ck_new[h:h + 1, :]) * LOG2E
        s_n = jnp.where(c <= r, s_n, NEG_INF)
        m = jnp.maximum(jnp.max(s_c, axis=-1, keepdims=True), jnp.max(s_n, axis=-1, keepdims=True))
        p_c = jnp.exp2(s_c - m)
        p_n = jnp.exp2(s_n - m)
        l = jnp.sum(p_c, axis=-1, keepdims=True) + jnp.sum(p_n, axis=-1, keepdims=True)
        o = _dot(p_c.astype(BF16), vc_ref[0, h].astype(BF16)) + \
            _dot(p_n.astype(BF16), vn_ref[0, h].astype(BF16))
        o_ref[:, sl] = (o / l).astype(BF16)


def _fox_sample(fq, kc, vc, kn, vn, lfc_row, lfn_row, c_col, *, n_heads, seq_len):
    m, d_fox = fq.shape
    nb, _, p_len, _ = kc.shape
    tric = jnp.asarray(np.arange(p_len)[:, None] <= np.arange(p_len)[None, :], dtype=BF16)
    trin = jnp.asarray(np.arange(seq_len)[:, None] <= np.arange(seq_len)[None, :], dtype=BF16)
    row = lambda w: pl.BlockSpec((seq_len, w), lambda b: (b, 0))
    cache = pl.BlockSpec((1, n_heads, p_len, HEAD_DIM), lambda b: (b, 0, 0, 0))
    new = pl.BlockSpec((1, n_heads, seq_len, HEAD_DIM), lambda b: (b, 0, 0, 0))
    return pl.pallas_call(
        functools.partial(_fox_sample_kernel, n_heads=n_heads),
        grid=(nb,),
        in_specs=[row(d_fox), cache, cache, new, new,
                  pl.BlockSpec((1, HEAD_ROWS, p_len), lambda b: (b, 0, 0)),
                  pl.BlockSpec((1, HEAD_ROWS, seq_len), lambda b: (b, 0, 0)),
                  row(LANES), _resident(tric.shape), _resident(trin.shape)],
        out_specs=row(d_fox),
        out_shape=jax.ShapeDtypeStruct((m, d_fox), BF16),
        compiler_params=_cparams(("arbitrary",)),
        name="fox_sample",
    )(fq, kc, vc, kn, vn, lfc_row, lfn_row, c_col, tric, trin)


def _log_gamma(h):
    return float(np.log1p(-np.exp2(np.float32(-5.0 - h)), dtype=np.float32))


def _retention_kernel(q_ref, k_ref, v_ref, g_ref, s0_ref, gn_ref, o_ref, s_ref, decay_ref, *, n_heads,
                      t_blk):
    step = pl.program_id(1)

    @pl.when(step == 0)
    def _():
        s_ref[...] = s0_ref[...]

    @pl.when(jnp.logical_and(pl.program_id(0) == 0, step == 0))
    def _():
        ti = lax.broadcasted_iota(jnp.int32, (t_blk, t_blk), 0)
        si = lax.broadcasted_iota(jnp.int32, (t_blk, t_blk), 1)
        dist = jnp.abs(ti - si).astype(F32)
        shift = CHUNK.bit_length() - 1
        visible = jnp.right_shift(si, shift) <= jnp.right_shift(ti, shift)
        for h in range(n_heads):
            decay_ref[h] = jnp.where(visible, jnp.exp(_log_gamma(h) * dist), 0.0)

    tcol = lax.broadcasted_iota(jnp.int32, (t_blk, 1), 0).astype(F32)
    for h in range(n_heads):
        lg = _log_gamma(h)
        sl = slice(h * HEAD_DIM, (h + 1) * HEAD_DIM)
        q = q_ref[:, sl]
        k = k_ref[:, sl]
        v = v_ref[:, sl]
        a = _dot_nt(q, k) * decay_ref[h]
        state = s_ref[0, h]
        o = _dot(a.astype(BF16), v) + _dot(q, state.astype(BF16)) * jnp.exp(lg * (tcol + 1.0))
        kd = (k.astype(F32) * jnp.exp(lg * (t_blk - 1.0 - tcol))).astype(BF16)
        s_ref[0, h] = math.exp(lg * t_blk) * state + _dot_tn(kd, v)
        mu = jnp.mean(o, axis=-1, keepdims=True)
        oc = o - mu
        var = jnp.mean(oc * oc, axis=-1, keepdims=True)
        rn = oc * lax.rsqrt(var + GN_EPS) * gn_ref[:, sl]
        gate = g_ref[:, sl].astype(F32)
        o_ref[:, sl] = (gate / (1.0 + jnp.exp(-gate)) * rn).astype(BF16)


def _retention(rq, rk, rv, rg, s0, gn_g, *, t_blk, seq_len, n_heads):
    m, d_ret = rq.shape
    nb = m // seq_len
    steps = seq_len // t_blk
    row = lambda: pl.BlockSpec((t_blk, d_ret), lambda b, s: (b * steps + s, 0))
    st = lambda: pl.BlockSpec((1, n_heads, HEAD_DIM, HEAD_DIM), lambda b, s: (b, 0, 0, 0))
    return pl.pallas_call(
        functools.partial(_retention_kernel, n_heads=n_heads, t_blk=t_blk),
        grid=(nb, steps),
        in_specs=[row(), row(), row(), row(), st(), pl.BlockSpec((1, d_ret), lambda b, s: (0, 0))],
        out_specs=(row(), st()),
        out_shape=(jax.ShapeDtypeStruct((m, d_ret), BF16),
                   jax.ShapeDtypeStruct((nb, n_heads, HEAD_DIM, HEAD_DIM), F32)),
        scratch_shapes=[pltpu.VMEM((n_heads, t_blk, t_blk), F32)],
        compiler_params=_cparams(("arbitrary", "arbitrary")),
        name="retention",
    )(rq, rk, rv, rg, s0, gn_g)


def _mix_out_kernel(x_ref, fo_ref, ro_ref, mo_ref, wo_ref, g_ref, b_ref, o_ref, cat_ref, *, alpha):
    d_fox = fo_ref.shape[1]
    d_ret = ro_ref.shape[1]
    cat_ref[:, 0:d_fox] = fo_ref[...]
    cat_ref[:, d_fox:d_fox + d_ret] = ro_ref[...]
    cat_ref[:, d_fox + d_ret:] = mo_ref[...]
    tm = x_ref.shape[0]
    pieces = [slice(0, tm // 2), slice(tm // 2, tm)] if tm % 32 == 0 else [slice(0, tm)]
    mix = [_dot(cat_ref[r, :], wo_ref[...]) for r in pieces]
    for r, mx in zip(pieces, mix):
        o_ref[r, :] = _layernorm(alpha * x_ref[r, :] + mx, g_ref[...], b_ref[...])


def _mix_out(x, fo, ro, mo, wo, g, b, *, tm, alpha):
    m, d_model = x.shape
    row = lambda w: pl.BlockSpec((tm, w), lambda i: (i, 0))
    return pl.pallas_call(
        functools.partial(_mix_out_kernel, alpha=alpha),
        grid=(m // tm,),
        in_specs=[row(d_model), row(fo.shape[1]), row(ro.shape[1]), row(mo.shape[1]),
                  _resident(wo.shape), _resident(g.shape), _resident(b.shape)],
        out_specs=row(d_model),
        out_shape=jax.ShapeDtypeStruct((m, d_model), F32),
        scratch_shapes=[pltpu.VMEM((tm, wo.shape[0]), BF16)],
        compiler_params=_cparams(("arbitrary",)),
        name="mix_out",
    )(x, fo, ro, mo, wo, g, b)


def _gelu_tanh(x):
    c = math.sqrt(2.0 / math.pi)
    return x * (0.5 * (1.0 + jnp.tanh(c * (x + 0.044715 * (x * x * x)))))


def _conv_ffn_kernel(h_ref, wg_ref, wu_ref, wd_ref, cw_ref, cb_ref, past_ref, g_ref, b_ref,
                     y_ref, cv_ref, hb_ref, gbuf_ref, carry_ref, *, alpha, seq_len, tm, tc):
    i = pl.program_id(0)
    f = pl.program_id(1)
    nf = pl.num_programs(1)
    lt = min(seq_len, tm)
    ns = tm // lt
    halo = 8
    carried = seq_len > tm

    @pl.when(f == 0)
    def _():
        h = h_ref[...]
        hb_ref[...] = h.astype(BF16)
        y_ref[...] = alpha * h

    if carried:
        @pl.when(jnp.logical_and(i == 0, f == 0))
        def _():
            carry_ref[...] = jnp.zeros_like(carry_ref)

        tiles_per_seq = seq_len // tm
        gbuf_ref[0, halo - 2:halo, :] = jnp.where(i % tiles_per_seq == 0, past_ref[i // tiles_per_seq],
                                                  carry_ref[f])
    else:
        for s in range(ns):
            gbuf_ref[s, halo - 2:halo, :] = past_ref[i * ns + s]

    hb = hb_ref[...]
    hid = []
    for c0 in range(0, wg_ref.shape[1], tc):
        cs = slice(c0, c0 + tc)
        gate = _dot(hb, wg_ref[:, cs])
        up = _dot(hb, wu_ref[:, cs])
        w0 = cw_ref[0:1, cs]
        w1 = cw_ref[1:2, cs]
        w2 = cw_ref[2:3, cs]
        cb = cb_ref[:, cs]
        parts = []
        for s in range(ns):
            gs = gate[s * lt:(s + 1) * lt]
            gbuf_ref[s, halo:halo + lt, cs] = gs
            g1 = gbuf_ref[s, halo - 1:halo - 1 + lt, cs]
            g2 = gbuf_ref[s, halo - 2:halo - 2 + lt, cs]
            gc = cb + w0 * g2 + w1 * g1 + w2 * gs
            parts.append((_gelu_tanh(gc) * up[s * lt:(s + 1) * lt]).astype(BF16))
        hid.append(parts[0] if ns == 1 else jnp.concatenate(parts, axis=0))
    hid = hid[0] if len(hid) == 1 else jnp.concatenate(hid, axis=1)
    y_ref[...] += _dot(hid, wd_ref[...])

    for s in range(ns):
        last2 = gbuf_ref[s, halo + lt - 2:halo + lt, :]
        cv_ref[s] = last2
        if carried:
            carry_ref[f] = last2

    @pl.when(f == nf - 1)
    def _():
        y_ref[...] = _layernorm(y_ref[...], g_ref[...], b_ref[...])


def _conv_ffn(h1, wg, wu, wd, conv_w, conv_b, past, g, b, *, tm, tf, seq_len, alpha):
    tc = 2 * LANES
    assert tf % tc == 0
    m, d_model = h1.shape
    d_ff = wg.shape[1]
    n_seq = past.shape[0]
    assert m % tm == 0 and d_ff % tf == 0 and (seq_len % tm == 0 or tm % seq_len == 0)
    nf = d_ff // tf
    lt = min(seq_len, tm)
    y, cv_all = pl.pallas_call(
        functools.partial(_conv_ffn_kernel, alpha=alpha, seq_len=seq_len, tm=tm, tc=tc),
        grid=(m // tm, nf),
        in_specs=[pl.BlockSpec((tm, d_model), lambda i, f: (i, 0), pipeline_mode=pl.Buffered(1)),
                  pl.BlockSpec((d_model, tf), lambda i, f: (0, f)),
                  pl.BlockSpec((d_model, tf), lambda i, f: (0, f)),
                  pl.BlockSpec((tf, d_model), lambda i, f: (f, 0)),
                  pl.BlockSpec((CONV_W, tf), lambda i, f: (0, f)),
                  pl.BlockSpec((1, tf), lambda i, f: (0, f)),
                  pl.BlockSpec((n_seq, CONV_W - 1, tf), lambda i, f: (0, 0, f)),
                  pl.BlockSpec((1, d_model), lambda i, f: (0, 0)),
                  pl.BlockSpec((1, d_model), lambda i, f: (0, 0))],
        out_specs=(pl.BlockSpec((tm, d_model), lambda i, f: (i, 0)),
                   pl.BlockSpec((tm // lt, CONV_W - 1, tf), lambda i, f: (i, 0, f))),
        out_shape=(jax.ShapeDtypeStruct((m, d_model), F32),
                   jax.ShapeDtypeStruct((m // lt, CONV_W - 1, d_ff), F32)),
        scratch_shapes=[pltpu.VMEM((tm, d_model), BF16),
                        pltpu.VMEM((tm // lt, 8 + lt, tf), F32),
                        pltpu.VMEM((nf, CONV_W - 1, tf), F32)],
        compiler_params=_cparams(("arbitrary", "arbitrary")),
        name="conv_ffn",
    )(h1, wg, wu, wd, conv_w, conv_b, past, g, b)
    return y, cv_all.reshape(n_seq, seq_len // lt, CONV_W - 1, d_ff)[:, -1]


def _rope_tables(tile_base, tm):
    half = HEAD_DIM // 2
    inv = ROPE_BASE ** (-jnp.arange(half, dtype=F32) / half)
    inv = jnp.concatenate([inv, inv])
    sign = jnp.concatenate([-jnp.ones((half,), F32), jnp.ones((half,), F32)])
    a = tile_base.astype(F32)[:, None] * inv[None, :]
    b = jnp.arange(tm, dtype=F32)[:, None] * inv[None, :]
    base_tab = jnp.stack([jnp.cos(a), jnp.sin(a)], axis=1)
    base_tab = jnp.pad(base_tab, ((0, 0), (0, 6), (0, 0)))
    rot_tab = jnp.stack([jnp.cos(b), jnp.sin(b), sign * jnp.cos(b), sign * jnp.sin(b)])
    return base_tab, rot_tab


def _layer(x, seq_start, seq_len, mk, mv, w, *, alpha, tm_proj, tm_mix, tm_ffn, tf, t_ret, s0, conv_past,
           fox_fn, attn_block):
    tile_base = (seq_start[:, None] + jnp.arange(0, seq_len, tm_proj)[None, :]).reshape(-1)
    base_tab, rot_tab = _rope_tables(tile_base, tm_proj)
    fq, fk, fv, lf, c, rq, rk, rv, rg, mo, qa, ka, va = _in_proj(
        x, w["wa"], w["wf"], w["bf"], base_tab, rot_tab, mk, mv, tm=tm_proj, seq_len=seq_len,
        d_fox=w["d_fox"], d_ret=w["d_ret"], d_memq=w["d_memq"], attn_block=attn_block)
    fo = fox_fn(fq, fk, fv, lf, c, qa, ka, va)
    ro, s_new = _retention(rq, rk, rv, rg, s0, w["gn_g"], t_blk=t_ret, seq_len=seq_len,
                           n_heads=w["d_ret"] // HEAD_DIM)
    h1 = _mix_out(x, fo, ro, mo, w["wo"], w["ln1_g"], w["ln1_b"], tm=tm_mix, alpha=alpha)
    y, cv = _conv_ffn(h1, w["wg"], w["wu"], w["wd"], w["conv_w"], w["conv_b"], conv_past,
                      w["ln2_g"], w["ln2_b"], tm=tm_ffn, tf=tf, seq_len=seq_len, alpha=alpha)
    return y, fk, fv, lf, s_new, cv


def kernel(x_prompt, x_sample, cache_fox_k, cache_fox_v, cache_fox_logf, state_ret, cache_mem_k,
           cache_mem_v, state_conv, mem_prompt, w_in, b_f, w_mem_kv, ret_gn_g, w_o, ln1_g, ln1_b,
           w_gate, w_up, conv_w, conv_b, w_down, ln2_g, ln2_b):
    batch, seq, d_model = x_prompt.shape
    dec_batch, dec_seq, _ = x_sample.shape
    depth = w_in.shape[0]
    past_len = cache_fox_k.shape[2]
    n_fox = cache_fox_k.shape[3]
    n_ret = state_ret.shape[2]
    n_mem = cache_mem_k.shape[3]
    n_mem_tok = cache_mem_k.shape[2]
    d_fox, d_ret, d_memq = n_fox * HEAD_DIM, n_ret * HEAD_DIM, n_mem * HEAD_DIM
    d_ff = w_gate.shape[2]
    assert batch == 1 and depth == 1 and n_fox <= 8
    alpha = (2.0 * depth) ** 0.25
    l = 0

    o_ff = 3 * d_fox
    w_fm = jnp.transpose(w_in, (2, 0, 1))
    wa, wf = _w_prep(w_fm, o_ff=o_ff, n_fox=n_fox, tn=2 * LANES)
    bfp = jnp.pad(b_f[l], (0, LANES - n_fox)).reshape(1, LANES)
    w = dict(wa=wa, wf=wf, bf=bfp, d_fox=d_fox, d_ret=d_ret, d_memq=d_memq,
             gn_g=ret_gn_g[l].reshape(1, d_ret), wo=w_o[l].astype(BF16),
             ln1_g=ln1_g[l].reshape(1, d_model), ln1_b=ln1_b[l].reshape(1, d_model),
             wg=w_gate[l].astype(BF16), wu=w_up[l].astype(BF16), wd=w_down[l].astype(BF16),
             conv_w=conv_w[l], conv_b=conv_b[l].reshape(1, d_ff),
             ln2_g=ln2_g[l].reshape(1, d_model), ln2_b=ln2_b[l].reshape(1, d_model))

    to_head_major = lambda t: jnp.swapaxes(t, -3, -2)
    mkv = _mem_kv(mem_prompt.reshape(n_mem_tok, d_model), w_mem_kv[l].astype(BF16))
    mk_p, mv_p = mkv[None, :n_mem], mkv[None, n_mem:]

    fox_tb = 512

    def fox_prompt_fn(fq, fk, fv, lf, c, qa, ka, va):
        return _fox_prompt(qa, ka, va, c, tb=fox_tb, nsub=4)

    y_p, fk_p, fv_p, lf_p, s_p, cv_p = _layer(
        x_prompt.reshape(seq, d_model), jnp.zeros((batch,), jnp.int32), seq, mk_p, mv_p, w,
        alpha=alpha, tm_proj=256, tm_mix=512, tm_ffn=1024, tf=512, t_ret=256,
        s0=jnp.zeros((1, n_ret, HEAD_DIM, HEAD_DIM), F32),
        conv_past=jnp.zeros((1, CONV_W - 1, d_ff), F32), fox_fn=fox_prompt_fn, attn_block=fox_tb)

    m_s = dec_batch * dec_seq
    kc = to_head_major(cache_fox_k[l])
    vc = to_head_major(cache_fox_v[l])
    lfc_row = jnp.pad(jnp.swapaxes(cache_fox_logf[l], 1, 2), ((0, 0), (0, HEAD_ROWS - n_fox), (0, 0)))

    def fox_sample_fn(fq, fk, fv, lf, c, qa, ka, va):
        lfn_row = jnp.swapaxes(lf.reshape(dec_batch, dec_seq, LANES)[:, :, :HEAD_ROWS], 1, 2)
        return _fox_sample(fq, kc, vc, fk, fv, lfc_row, lfn_row, c, n_heads=n_fox, seq_len=dec_seq)

    y_s, fk_s, fv_s, lf_s, s_s, cv_s = _layer(
        x_sample.reshape(m_s, d_model), jnp.full((dec_batch,), past_len, jnp.int32), dec_seq,
        to_head_major(cache_mem_k[l]), to_head_major(cache_mem_v[l]), w,
        alpha=alpha, tm_proj=dec_seq, tm_mix=m_s, tm_ffn=m_s, tf=512, t_ret=dec_seq,
        s0=state_ret[l], conv_past=state_conv[l], fox_fn=fox_sample_fn, attn_block=dec_seq)

    return (y_p.reshape(batch, seq, d_model), y_s.reshape(dec_batch, dec_seq, d_model),
            to_head_major(fk_p)[None], to_head_major(fv_p)[None],
            lf_p[:, :n_fox].reshape(1, batch, seq, n_fox),
            s_p.reshape(1, batch, n_ret, HEAD_DIM, HEAD_DIM),
            to_head_major(mk_p)[None], to_head_major(mv_p)[None],
            cv_p.reshape(1, batch, CONV_W - 1, d_ff),
            to_head_major(fk_s)[None], to_head_major(fv_s)[None],
            lf_s[:, :n_fox].reshape(1, dec_batch, dec_seq, n_fox),
            s_s.reshape(1, dec_batch, n_ret, HEAD_DIM, HEAD_DIM),
            cv_s.reshape(1, dec_batch, CONV_W - 1, d_ff))
```

```python
import functools
import math

import numpy as np
import jax
import jax.numpy as jnp
from jax import lax
from jax.experimental import pallas as pl
from jax.experimental.pallas import tpu as pltpu

F32 = jnp.float32
BF16 = jnp.bfloat16

HEAD_DIM = 128
CHUNK = 64
CONV_W = 3
ROPE_BASE = 10000.0
LN_EPS = 1e-5
GN_EPS = 1e-5
LANES = 128
V7X_VMEM_BYTES = 64 * 1024 * 1024
VMEM_LIMIT = V7X_VMEM_BYTES - 4 * 1024 * 1024
SCALE = HEAD_DIM ** -0.5
LOG2E = math.log2(math.e)
NEG_INF = float("-inf")
HEAD_ROWS = 16


def _cparams(sem):
    return pltpu.CompilerParams(dimension_semantics=sem, vmem_limit_bytes=VMEM_LIMIT)


def _resident(shape):
    nd = len(shape)
    return pl.BlockSpec(shape, lambda *_: (0,) * nd, pipeline_mode=pl.Buffered(1))


def _dot(a, b):
    return jnp.dot(a, b, preferred_element_type=F32)


def _dot_nt(a, b):
    return lax.dot_general(a, b, (((1,), (1,)), ((), ())), preferred_element_type=F32)


def _dot_tn(a, b):
    return lax.dot_general(a, b, (((0,), (0,)), ((), ())), preferred_element_type=F32)


def _split3_dot(a, tri, *, tri_first):
    hi = a.astype(BF16)
    r1 = a - hi.astype(F32)
    mid = r1.astype(BF16)
    lo = (r1 - mid.astype(F32)).astype(BF16)
    if tri_first:
        return _dot(tri, hi) + _dot(tri, mid) + _dot(tri, lo)
    return _dot(hi, tri) + _dot(mid, tri) + _dot(lo, tri)


def _layernorm(z, g, b):
    mu = jnp.mean(z, axis=-1, keepdims=True)
    zc = z - mu
    var = jnp.mean(zc * zc, axis=-1, keepdims=True)
    return zc * lax.rsqrt(var + LN_EPS) * g + b


def _memkv_kernel(m_ref, w_ref, o_ref):
    kv = _dot(m_ref[...].astype(BF16), w_ref[...])
    for j in range(o_ref.shape[0]):
        o_ref[j] = kv[:, j * HEAD_DIM:(j + 1) * HEAD_DIM]


def _mem_kv(mem, w_bf):
    m, _ = mem.shape
    return pl.pallas_call(
        _memkv_kernel,
        out_shape=jax.ShapeDtypeStruct((w_bf.shape[1] // HEAD_DIM, m, HEAD_DIM), F32),
        compiler_params=pltpu.CompilerParams(vmem_limit_bytes=VMEM_LIMIT),
        name="mem_kv",
    )(mem, w_bf)


def _w_prep_kernel(w_hbm, wa_ref, wf_ref, buf, sem, *, o_ff, n_fox, tn):
    j = pl.program_id(0)
    n = pl.num_programs(0)

    def copy(blk, slot):
        r0 = jnp.where(blk * tn < o_ff, blk * tn, blk * tn + n_fox)
        return pltpu.make_async_copy(w_hbm.at[pl.ds(r0, tn)], buf.at[slot], sem.at[slot])

    @pl.when(j == 0)
    def _():
        gate = pltpu.make_async_copy(w_hbm.at[pl.ds(o_ff, tn)], buf.at[1], sem.at[1])
        gate.start()
        copy(0, 0).start()
        gate.wait()
        rows = buf[1, 0:LANES, 0, :]
        keep = lax.broadcasted_iota(jnp.int32, rows.shape, 0) < n_fox
        wf_ref[...] = jnp.where(keep, rows, 0.0).T.astype(BF16)

    @pl.when(j + 1 < n)
    def _():
        copy(j + 1, (j + 1) % 2).start()

    copy(j, j % 2).wait()
    wa_ref[...] = buf[j % 2, :, 0, :].T.astype(BF16)


def _w_prep(w_fm, *, o_ff, n_fox, tn):
    n, _, d_model = w_fm.shape
    assert o_ff % tn == 0 and (n - n_fox) % tn == 0 and tn >= LANES and o_ff + tn <= n
    return pl.pallas_call(
        functools.partial(_w_prep_kernel, o_ff=o_ff, n_fox=n_fox, tn=tn),
        grid=((n - n_fox) // tn,),
        in_specs=[pl.BlockSpec(memory_space=pl.ANY)],
        out_specs=(pl.BlockSpec((d_model, tn), lambda j: (0, j)),
                   pl.BlockSpec((d_model, LANES), lambda j: (0, 0))),
        out_shape=(jax.ShapeDtypeStruct((d_model, n - n_fox), BF16),
                   jax.ShapeDtypeStruct((d_model, LANES), BF16)),
        scratch_shapes=[pltpu.VMEM((2, tn, 1, d_model), F32), pltpu.SemaphoreType.DMA((2,))],
        compiler_params=_cparams(("arbitrary",)),
        name="w_prep",
    )(w_fm)


def _in_proj_kernel(x_ref, wa_ref, wf_ref, bf_ref, base_ref, rot_ref, mk_ref, mv_ref, tri_ref,
                    fq_ref, fk_ref, fv_ref, lf_ref, c_ref, rq_ref, rk_ref, rv_ref, rg_ref, mo_ref,
                    qa_ref, ka_ref, va_ref, carry_ref, start_ref, *, d_fox, d_ret, d_memq, tiles_per_seq,
                    blk_tiles):
    i = pl.program_id(0)
    xb = x_ref[...].astype(BF16)

    def proj(lo, width):
        return _dot(xb, wa_ref[:, lo:lo + width])

    o_fk = d_fox
    o_fv = 2 * d_fox
    o_rq = 3 * d_fox
    o_rk = o_rq + d_ret
    o_rv = o_rk + d_ret
    o_rg = o_rv + d_ret
    o_mq = o_rg + d_ret

    zfq = proj(0, d_fox) * (SCALE * LOG2E)
    fq_ref[...] = zfq.astype(BF16)
    zfk = proj(o_fk, d_fox)
    zfv = proj(o_fv, d_fox)
    for h in range(d_fox // HEAD_DIM):
        sl = slice(h * HEAD_DIM, (h + 1) * HEAD_DIM)
        fk_ref[0, h] = zfk[:, sl]
        fv_ref[0, h] = zfv[:, sl]

    z = _dot(xb, wf_ref[...]) + bf_ref[...]
    lf = jnp.minimum(z, 0.0) - jnp.log1p(jnp.exp(-jnp.abs(z)))
    lf_ref[...] = lf

    @pl.when(i % tiles_per_seq == 0)
    def _():
        carry_ref[...] = jnp.zeros_like(carry_ref)

    c = _split3_dot(lf, tri_ref[...], tri_first=True) + carry_ref[0:1, :]
    c_ref[...] = c
    tm = c.shape[0]
    carry_ref[...] = jnp.broadcast_to(c[tm - 1:tm, :], carry_ref.shape)

    @pl.when(i % blk_tiles == 0)
    def _():
        start_ref[...] = jnp.broadcast_to(c[0:1, :], start_ref.shape)

    rel3 = _split3_f32((c - start_ref[0:1, :]) * LOG2E)
    for h in range(d_fox // HEAD_DIM):
        sl = slice(h * HEAD_DIM, (h + 1) * HEAD_DIM)
        qa_ref[0, h], ka_ref[0, h], va_ref[0, h] = _fox_operands(zfq[:, sl], zfk[:, sl], zfv[:, sl], rel3, h)

    ca = base_ref[0, 0:1, :]
    sa = base_ref[0, 1:2, :]
    cos = ca * rot_ref[0] - sa * rot_ref[1]
    sin = sa * rot_ref[2] + ca * rot_ref[3]

    def rope(t):
        return t * cos + pltpu.roll(t, HEAD_DIM // 2, axis=1) * sin

    zq = proj(o_rq, d_ret)
    zk = proj(o_rk, d_ret)
    for h in range(d_ret // HEAD_DIM):
        sl = slice(h * HEAD_DIM, (h + 1) * HEAD_DIM)
        rq_ref[:, sl] = rope(zq[:, sl]).astype(BF16)
        rk_ref[:, sl] = (rope(zk[:, sl]) * SCALE).astype(BF16)
    rv_ref[...] = proj(o_rv, d_ret).astype(BF16)
    rg_ref[...] = proj(o_rg, d_ret).astype(BF16)

    zm = proj(o_mq, d_memq)
    for h in range(d_memq // HEAD_DIM):
        sl = slice(h * HEAD_DIM, (h + 1) * HEAD_DIM)
        q = (zm[:, sl] * SCALE).astype(BF16)
        s = _dot_nt(q, mk_ref[0, h].astype(BF16))
        p = jnp.exp(s - jnp.max(s, axis=-1, keepdims=True))
        l = jnp.sum(p, axis=-1, keepdims=True)
        o = _dot(p.astype(BF16), mv_ref[0, h].astype(BF16))
        mo_ref[:, sl] = (o / l).astype(BF16)


def _in_proj(x, wa, wf, bfp, base_tab, rot_tab, mk, mv, *, tm, seq_len, d_fox, d_ret, d_memq, attn_block):
    m, d_model = x.shape
    assert m % tm == 0 and seq_len % tm == 0
    assert base_tab.shape == (m // tm, 8, HEAD_DIM) and rot_tab.shape == (4, tm, HEAD_DIM)
    tiles_per_seq = seq_len // tm
    n_fox = d_fox // HEAD_DIM
    tri = (np.arange(tm)[:, None] >= np.arange(tm)[None, :])
    tri = jnp.asarray(tri, dtype=BF16)
    row = lambda w: pl.BlockSpec((tm, w), lambda i: (i, 0))
    memspec = pl.BlockSpec((1,) + mk.shape[1:], lambda i: (i // tiles_per_seq, 0, 0, 0))
    kvspec = pl.BlockSpec((1, n_fox, tm, HEAD_DIM), lambda i: (i // tiles_per_seq, 0, i % tiles_per_seq, 0))
    augspec = pl.BlockSpec((1, n_fox, tm, 2 * HEAD_DIM),
                           lambda i: (i // tiles_per_seq, 0, i % tiles_per_seq, 0))
    aug = jax.ShapeDtypeStruct((m // seq_len, n_fox, seq_len, 2 * HEAD_DIM), BF16)
    assert attn_block % tm == 0 and seq_len % attn_block == 0
    kern = functools.partial(_in_proj_kernel, d_fox=d_fox, d_ret=d_ret, d_memq=d_memq,
                             tiles_per_seq=tiles_per_seq, blk_tiles=attn_block // tm)
    out_shape = (
        jax.ShapeDtypeStruct((m, d_fox), BF16),
        jax.ShapeDtypeStruct((m // seq_len, n_fox, seq_len, HEAD_DIM), F32),
        jax.ShapeDtypeStruct((m // seq_len, n_fox, seq_len, HEAD_DIM), F32),
        jax.ShapeDtypeStruct((m, LANES), F32),
        jax.ShapeDtypeStruct((m, LANES), F32),
        jax.ShapeDtypeStruct((m, d_ret), BF16),
        jax.ShapeDtypeStruct((m, d_ret), BF16),
        jax.ShapeDtypeStruct((m, d_ret), BF16),
        jax.ShapeDtypeStruct((m, d_ret), BF16),
        jax.ShapeDtypeStruct((m, d_memq), BF16),
        aug, aug, aug,
    )
    return pl.pallas_call(
        kern,
        grid=(m // tm,),
        in_specs=[row(d_model), _resident(wa.shape), _resident(wf.shape), _resident(bfp.shape),
                  pl.BlockSpec((1, 8, HEAD_DIM), lambda i: (i, 0, 0)), _resident(rot_tab.shape),
                  memspec, memspec, _resident(tri.shape)],
        out_specs=(row(d_fox), kvspec, kvspec, row(LANES), row(LANES),
                   row(d_ret), row(d_ret), row(d_ret), row(d_ret), row(d_memq),
                   augspec, augspec, augspec),
        out_shape=out_shape,
        scratch_shapes=[pltpu.VMEM((8, LANES), F32), pltpu.VMEM((8, LANES), F32)],
        compiler_params=_cparams(("arbitrary",)),
        name="in_proj",
    )(x, wa, wf, bfp, base_tab, rot_tab, mk, mv, tri)


def _split3_f32(a):
    hi = a.astype(BF16).astype(F32)
    r1 = a - hi
    mid = r1.astype(BF16).astype(F32)
    lo = (r1 - mid).astype(BF16).astype(F32)
    return hi, mid, lo


def _fox_operands(q, k, v, rel3, h):
    lane = lax.broadcasted_iota(jnp.int32, q.shape, 1)
    hi, mid, lo = (t[:, h:h + 1] for t in rel3)
    qbias = jnp.where(lane == 0, hi, jnp.where(lane == 1, mid, jnp.where(lane == 2, lo,
                      jnp.where(lane < 6, 1.0, 0.0))))
    kbias = jnp.where(lane < 3, 1.0, jnp.where(lane == 3, -hi, jnp.where(lane == 4, -mid,
                      jnp.where(lane == 5, -lo, 0.0))))
    ones_col = jnp.where(lane == 0, 1.0, 0.0)
    cat = lambda a, b: jnp.concatenate([a.astype(BF16), b.astype(BF16)], axis=1)
    return cat(q, qbias), cat(k, kbias), cat(v, ones_col)


def _fox_prompt_kernel(qa, k_ref, v_ref, ccol_ref, o_ref, kbuf, vbuf, cstart, acc, sbuf, tbuf, *, tb, nsub,
                       kv_per_trip):
    h = pl.program_id(0)
    qi = pl.program_id(1)
    tq = tb * nsub
    row0 = pl.multiple_of(qi * tq, tq)
    kbuf[pl.ds(row0, tq), :] = k_ref[...]
    vbuf[pl.ds(row0, tq), :] = v_ref[...]
    lane = lax.broadcasted_iota(jnp.int32, (8, LANES), 1)
    for s in range(nsub):
        start = jnp.sum(jnp.where(lane == h, ccol_ref[s * tb:s * tb + 8, :], 0.0), axis=1, keepdims=True)
        cstart[nsub * qi + s] = jnp.broadcast_to(start[0:1], (8, LANES))
    acc[...] = jnp.zeros_like(acc)

    r_id = lax.broadcasted_iota(jnp.int32, (tb, LANES), 0)
    c_id = lax.broadcasted_iota(jnp.int32, (tb, LANES), 1)
    n_ct = tb // LANES
    subs = range(nsub)

    def lane_tiles(s2):
        return [s2[:, t * LANES:(t + 1) * LANES] for t in range(n_ct)]

    def scores(s, kj):
        col0 = pl.multiple_of(kj * tb, tb)
        s2 = _dot_nt(qa[s * tb:(s + 1) * tb, :], kbuf[pl.ds(col0, tb), :])
        return s2, functools.reduce(jnp.maximum, lane_tiles(s2))

    def softmax_pv(s, kj, sc, m, masked):
        col0 = pl.multiple_of(kj * tb, tb)
        s2, tmax = sc
        tiles = lane_tiles(s2)
        if masked:
            tiles = [jnp.where(c_id + t * LANES <= r_id, tiles[t], NEG_INF) for t in range(n_ct)]
            tmax = functools.reduce(jnp.maximum, tiles)
        delta = (cstart[nsub * qi + s][0:1, :] - cstart[kj][0:1, :]) * LOG2E
        m_rel = m - delta
        m_new = jnp.maximum(m_rel, jnp.max(tmax, axis=-1, keepdims=True))
        alpha = jnp.exp2(m_rel - m_new)
        p = jnp.concatenate([jnp.exp2(t - m_new).astype(BF16) for t in tiles], axis=1)
        pv = _dot(p, vbuf[pl.ds(col0, tb), :])
        a = acc[s]
        acc[s] = jnp.concatenate([alpha * a[:, 0:LANES], alpha * a[:, LANES:]], axis=1) + pv
        return m_new + delta

    for s in subs:
        sbuf[s], tbuf[s] = scores(s, 0)

    def body(t, ms):
        kj = kv_per_trip * t
        cur = [(sbuf[s], tbuf[s]) for s in subs]
        for u in range(kv_per_trip):
            nxt = [scores(s, kj + u + 1) for s in subs]
            ms = [softmax_pv(s, kj + u, cur[s], ms[s], False) for s in subs]
            cur = nxt
        for s in subs:
            sbuf[s], tbuf[s] = cur[s]
        return tuple(ms)

    assert nsub % kv_per_trip == 0
    ms = lax.fori_loop(0, (nsub // kv_per_trip) * qi, body,
                       tuple(jnp.full((tb, LANES), NEG_INF, F32) for _ in subs))
    for s in subs:
        m = ms[s]
        for t in range(s + 1):
            sc = (sbuf[s], tbuf[s]) if t == 0 else scores(s, nsub * qi + t)
            m = softmax_pv(s, nsub * qi + t, sc, m, t == s)
        a = acc[s]
        o_ref[s * tb:(s + 1) * tb, :] = (a[:, 0:HEAD_DIM] / a[:, HEAD_DIM:HEAD_DIM + 1]).astype(BF16)


def _fox_prompt(qa, ka, va, c_col, *, tb, nsub):
    _, n_heads, s, width = qa.shape
    tq = tb * nsub
    assert s % tq == 0
    op = lambda: pl.BlockSpec((None, None, tq, width), lambda h, qi: (0, h, qi, 0))
    return pl.pallas_call(
        functools.partial(_fox_prompt_kernel, tb=tb, nsub=nsub, kv_per_trip=4),
        grid=(n_heads, s // tq),
        in_specs=[op(), op(), op(), pl.BlockSpec((tq, LANES), lambda h, qi: (qi, 0))],
        out_specs=pl.BlockSpec((tq, HEAD_DIM), lambda h, qi: (qi, h)),
        out_shape=jax.ShapeDtypeStruct((s, n_heads * HEAD_DIM), BF16),
        scratch_shapes=[pltpu.VMEM((s, width), BF16),
                        pltpu.VMEM((s, width), BF16),
                        pltpu.VMEM((s // tb, 8, LANES), F32),
                        pltpu.VMEM((nsub, tb, width), F32),
                        pltpu.VMEM((nsub, tb, tb), F32),
                        pltpu.VMEM((nsub, tb, LANES), F32)],
        compiler_params=_cparams(("arbitrary", "arbitrary")),
        name="fox_prompt",
    )(qa, ka, va, c_col)


def _fox_sample_kernel(q_ref, kc_ref, vc_ref, kn_ref, vn_ref, lfc_ref, lfn_ref, ccol_ref,
                       tric_ref, trin_ref, o_ref, *, n_heads):
    c_cache = _split3_dot(lfc_ref[0], tric_ref[...], tri_first=False)
    p_len = c_cache.shape[1]
    ck_cache = c_cache - c_cache[:, p_len - 1:p_len]
    ck_new = _split3_dot(lfn_ref[0], trin_ref[...], tri_first=False)
    ccol = ccol_ref[...]
    n_new = ck_new.shape[1]
    r = lax.broadcasted_iota(jnp.int32, (n_new, n_new), 0)
    c = lax.broadcasted_iota(jnp.int32, (n_new, n_new), 1)
    for h in range(n_heads):
        sl = slice(h * HEAD_DIM, (h + 1) * HEAD_DIM)
        q = q_ref[:, sl]
        cq = ccol[:, h:h + 1]
        s_c = _dot_nt(q, kc_ref[0, h].astype(BF16)) + (cq - ck_cache[h:h + 1, :]) * LOG2E
        s_n = _dot_nt(q, kn_ref[0, h].astype(BF16)) + (cq - ck_new[h:h + 1, :]) * LOG2E
        s_n = jnp.where(c <= r, s_n, NEG_INF)
        m = jnp.maximum(jnp.max(s_c, axis=-1, keepdims=True), jnp.max(s_n, axis=-1, keepdims=True))
        p_c = jnp.exp2(s_c - m)
        p_n = jnp.exp2(s_n - m)
        l = jnp.sum(p_c, axis=-1, keepdims=True) + jnp.sum(p_n, axis=-1, keepdims=True)
        o = _dot(p_c.astype(BF16), vc_ref[0, h].astype(BF16)) + \
            _dot(p_n.astype(BF16), vn_ref[0, h].astype(BF16))
        o_ref[:, sl] = (o / l).astype(BF16)


def _fox_sample(fq, kc, vc, kn, vn, lfc_row, lfn_row, c_col, *, n_heads, seq_len):
    m, d_fox = fq.shape
    nb, _, p_len, _ = kc.shape
    tric = jnp.asarray(np.arange(p_len)[:, None] <= np.arange(p_len)[None, :], dtype=BF16)
    trin = jnp.asarray(np.arange(seq_len)[:, None] <= np.arange(seq_len)[None, :], dtype=BF16)
    row = lambda w: pl.BlockSpec((seq_len, w), lambda b: (b, 0))
    cache = pl.BlockSpec((1, n_heads, p_len, HEAD_DIM), lambda b: (b, 0, 0, 0))
    new = pl.BlockSpec((1, n_heads, seq_len, HEAD_DIM), lambda b: (b, 0, 0, 0))
    return pl.pallas_call(
        functools.partial(_fox_sample_kernel, n_heads=n_heads),
        grid=(nb,),
        in_specs=[row(d_fox), cache, cache, new, new,
                  pl.BlockSpec((1, HEAD_ROWS, p_len), lambda b: (b, 0, 0)),
                  pl.BlockSpec((1, HEAD_ROWS, seq_len), lambda b: (b, 0, 0)),
                  row(LANES), _resident(tric.shape), _resident(trin.shape)],
        out_specs=row(d_fox),
        out_shape=jax.ShapeDtypeStruct((m, d_fox), BF16),
        compiler_params=_cparams(("arbitrary",)),
        name="fox_sample",
    )(fq, kc, vc, kn, vn, lfc_row, lfn_row, c_col, tric, trin)


def _log_gamma(h):
    return float(np.log1p(-np.exp2(np.float32(-5.0 - h)), dtype=np.float32))


def _retention_kernel(q_ref, k_ref, v_ref, g_ref, s0_ref, gn_ref, o_ref, s_ref, decay_ref, *, n_heads,
                      t_blk):
    step = pl.program_id(1)

    @pl.when(step == 0)
    def _():
        s_ref[...] = s0_ref[...]

    @pl.when(jnp.logical_and(pl.program_id(0) == 0, step == 0))
    def _():
        ti = lax.broadcasted_iota(jnp.int32, (t_blk, t_blk), 0)
        si = lax.broadcasted_iota(jnp.int32, (t_blk, t_blk), 1)
        dist = jnp.abs(ti - si).astype(F32)
        shift = CHUNK.bit_length() - 1
        visible = jnp.right_shift(si, shift) <= jnp.right_shift(ti, shift)
        for h in range(n_heads):
            decay_ref[h] = jnp.where(visible, jnp.exp(_log_gamma(h) * dist), 0.0)

    tcol = lax.broadcasted_iota(jnp.int32, (t_blk, 1), 0).astype(F32)
    for h in range(n_heads):
        lg = _log_gamma(h)
        sl = slice(h * HEAD_DIM, (h + 1) * HEAD_DIM)
        q = q_ref[:, sl]
        k = k_ref[:, sl]
        v = v_ref[:, sl]
        a = _dot_nt(q, k) * decay_ref[h]
        state = s_ref[0, h]
        o = _dot(a.astype(BF16), v) + _dot(q, state.astype(BF16)) * jnp.exp(lg * (tcol + 1.0))
        kd = (k.astype(F32) * jnp.exp(lg * (t_blk - 1.0 - tcol))).astype(BF16)
        s_ref[0, h] = math.exp(lg * t_blk) * state + _dot_tn(kd, v)
        mu = jnp.mean(o, axis=-1, keepdims=True)
        oc = o - mu
        var = jnp.mean(oc * oc, axis=-1, keepdims=True)
        rn = oc * lax.rsqrt(var + GN_EPS) * gn_ref[:, sl]
        gate = g_ref[:, sl].astype(F32)
        o_ref[:, sl] = (gate / (1.0 + jnp.exp(-gate)) * rn).astype(BF16)


def _retention(rq, rk, rv, rg, s0, gn_g, *, t_blk, seq_len, n_heads):
    m, d_ret = rq.shape
    nb = m // seq_len
    steps = seq_len // t_blk
    row = lambda: pl.BlockSpec((t_blk, d_ret), lambda b, s: (b * steps + s, 0))
    st = lambda: pl.BlockSpec((1, n_heads, HEAD_DIM, HEAD_DIM), lambda b, s: (b, 0, 0, 0))
    return pl.pallas_call(
        functools.partial(_retention_kernel, n_heads=n_heads, t_blk=t_blk),
        grid=(nb, steps),
        in_specs=[row(), row(), row(), row(), st(), pl.BlockSpec((1, d_ret), lambda b, s: (0, 0))],
        out_specs=(row(), st()),
        out_shape=(jax.ShapeDtypeStruct((m, d_ret), BF16),
                   jax.ShapeDtypeStruct((nb, n_heads, HEAD_DIM, HEAD_DIM), F32)),
        scratch_shapes=[pltpu.VMEM((n_heads, t_blk, t_blk), F32)],
        compiler_params=_cparams(("arbitrary", "arbitrary")),
        name="retention",
    )(rq, rk, rv, rg, s0, gn_g)


def _mix_out_kernel(x_ref, fo_ref, ro_ref, mo_ref, wo_ref, g_ref, b_ref, o_ref, *rest, alpha, emit_weight):
    wo_out, cat_ref = rest if emit_weight else (None,) + rest
    d_fox = fo_ref.shape[1]
    d_ret = ro_ref.shape[1]
    cat_ref[:, 0:d_fox] = fo_ref[...]
    cat_ref[:, d_fox:d_fox + d_ret] = ro_ref[...]
    cat_ref[:, d_fox + d_ret:] = mo_ref[...]
    wo = wo_ref[...]
    if emit_weight:
        wo = wo.astype(BF16)
        wo_out[...] = wo
    tm = x_ref.shape[0]
    pieces = [slice(0, tm // 2), slice(tm // 2, tm)] if tm % 32 == 0 else [slice(0, tm)]
    mix = [_dot(cat_ref[r, :], wo) for r in pieces]
    for r, mx in zip(pieces, mix):
        o_ref[r, :] = _layernorm(alpha * x_ref[r, :] + mx, g_ref[...], b_ref[...])


def _mix_out(x, fo, ro, mo, wo, g, b, *, tm, alpha):
    m, d_model = x.shape
    emit = wo.dtype != BF16
    assert not emit or m == tm
    row = lambda w: pl.BlockSpec((tm, w), lambda i: (i, 0))
    out_specs, out_shape = row(d_model), jax.ShapeDtypeStruct((m, d_model), F32)
    if emit:
        out_specs = (out_specs, pl.BlockSpec(wo.shape, lambda i: (0, 0)))
        out_shape = (out_shape, jax.ShapeDtypeStruct(wo.shape, BF16))
    return pl.pallas_call(
        functools.partial(_mix_out_kernel, alpha=alpha, emit_weight=emit),
        grid=(m // tm,),
        in_specs=[row(d_model), row(fo.shape[1]), row(ro.shape[1]), row(mo.shape[1]),
                  _resident(wo.shape), _resident(g.shape), _resident(b.shape)],
        out_specs=out_specs,
        out_shape=out_shape,
        scratch_shapes=[pltpu.VMEM((tm, wo.shape[0]), BF16)],
        compiler_params=_cparams(("arbitrary",)),
        name="mix_out",
    )(x, fo, ro, mo, wo, g, b)


def _gelu_tanh(x):
    c = math.sqrt(2.0 / math.pi)
    return x * (0.5 * (1.0 + jnp.tanh(c * (x + 0.044715 * (x * x * x)))))


def _conv_ffn_kernel(h_ref, wg_ref, wu_ref, wd_ref, cw_ref, cb_ref, past_ref, g_ref, b_ref,
                     y_ref, cv_ref, *rest, alpha, seq_len, tm, tc, emit_weights):
    wg_out, wu_out, wd_out = rest[:3] if emit_weights else (None, None, None)
    hb_ref, gbuf_ref, carry_ref = rest[3:] if emit_weights else rest

    def weight(ref, out, idx):
        w = ref[idx]
        if out is not None:
            w = w.astype(BF16)
            out[idx] = w
        return w

    i = pl.program_id(0)
    f = pl.program_id(1)
    nf = pl.num_programs(1)
    lt = min(seq_len, tm)
    ns = tm // lt
    halo = 8
    carried = seq_len > tm

    @pl.when(f == 0)
    def _():
        h = h_ref[...]
        hb_ref[...] = h.astype(BF16)
        y_ref[...] = alpha * h

    if carried:
        @pl.when(jnp.logical_and(i == 0, f == 0))
        def _():
            carry_ref[...] = jnp.zeros_like(carry_ref)

        tiles_per_seq = seq_len // tm
        gbuf_ref[0, halo - 2:halo, :] = jnp.where(i % tiles_per_seq == 0, past_ref[i // tiles_per_seq],
                                                  carry_ref[f])
    else:
        for s in range(ns):
            gbuf_ref[s, halo - 2:halo, :] = past_ref[i * ns + s]

    hb = hb_ref[...]
    hid = []
    for c0 in range(0, wg_ref.shape[1], tc):
        cs = slice(c0, c0 + tc)
        gate = _dot(hb, weight(wg_ref, wg_out, (slice(None), cs)))
        up = _dot(hb, weight(wu_ref, wu_out, (slice(None), cs)))
        w0 = cw_ref[0:1, cs]
        w1 = cw_ref[1:2, cs]
        w2 = cw_ref[2:3, cs]
        cb = cb_ref[:, cs]
        parts = []
        for s in range(ns):
            gs = gate[s * lt:(s + 1) * lt]
            gbuf_ref[s, halo:halo + lt, cs] = gs
            g1 = gbuf_ref[s, halo - 1:halo - 1 + lt, cs]
            g2 = gbuf_ref[s, halo - 2:halo - 2 + lt, cs]
            gc = cb + w0 * g2 + w1 * g1 + w2 * gs
            parts.append((_gelu_tanh(gc) * up[s * lt:(s + 1) * lt]).astype(BF16))
        hid.append(parts[0] if ns == 1 else jnp.concatenate(parts, axis=0))
    hid = hid[0] if len(hid) == 1 else jnp.concatenate(hid, axis=1)
    y_ref[...] += _dot(hid, weight(wd_ref, wd_out, (slice(None), slice(None))))

    for s in range(ns):
        last2 = gbuf_ref[s, halo + lt - 2:halo + lt, :]
        cv_ref[s] = last2
        if carried:
            carry_ref[f] = last2

    @pl.when(f == nf - 1)
    def _():
        y_ref[...] = _layernorm(y_ref[...], g_ref[...], b_ref[...])


def _conv_ffn(h1, wg, wu, wd, conv_w, conv_b, past, g, b, *, tm, tf, seq_len, alpha):
    tc = 2 * LANES
    assert tf % tc == 0
    m, d_model = h1.shape
    d_ff = wg.shape[1]
    n_seq = past.shape[0]
    assert m % tm == 0 and d_ff % tf == 0 and (seq_len % tm == 0 or tm % seq_len == 0)
    nf = d_ff // tf
    lt = min(seq_len, tm)
    emit = wg.dtype != BF16
    assert not emit or m == tm
    w_specs = [pl.BlockSpec((d_model, tf), lambda i, f: (0, f)),
               pl.BlockSpec((d_model, tf), lambda i, f: (0, f)),
               pl.BlockSpec((tf, d_model), lambda i, f: (f, 0))]
    w_shapes = [jax.ShapeDtypeStruct(t.shape, BF16) for t in (wg, wu, wd)] if emit else []
    y, cv_all, *w_bf = pl.pallas_call(
        functools.partial(_conv_ffn_kernel, alpha=alpha, seq_len=seq_len, tm=tm, tc=tc, emit_weights=emit),
        grid=(m // tm, nf),
        in_specs=[pl.BlockSpec((tm, d_model), lambda i, f: (i, 0), pipeline_mode=pl.Buffered(1)),
                  *w_specs,
                  pl.BlockSpec((CONV_W, tf), lambda i, f: (0, f)),
                  pl.BlockSpec((1, tf), lambda i, f: (0, f)),
                  pl.BlockSpec((n_seq, CONV_W - 1, tf), lambda i, f: (0, 0, f)),
                  pl.BlockSpec((1, d_model), lambda i, f: (0, 0)),
                  pl.BlockSpec((1, d_model), lambda i, f: (0, 0))],
        out_specs=(pl.BlockSpec((tm, d_model), lambda i, f: (i, 0)),
                   pl.BlockSpec((tm // lt, CONV_W - 1, tf), lambda i, f: (i, 0, f)),
                   *(w_specs if emit else [])),
        out_shape=(jax.ShapeDtypeStruct((m, d_model), F32),
                   jax.ShapeDtypeStruct((m // lt, CONV_W - 1, d_ff), F32), *w_shapes),
        scratch_shapes=[pltpu.VMEM((tm, d_model), BF16),
                        pltpu.VMEM((tm // lt, 8 + lt, tf), F32),
                        pltpu.VMEM((nf, CONV_W - 1, tf), F32)],
        compiler_params=_cparams(("arbitrary", "arbitrary")),
        name="conv_ffn",
    )(h1, wg, wu, wd, conv_w, conv_b, past, g, b)
    return y, cv_all.reshape(n_seq, seq_len // lt, CONV_W - 1, d_ff)[:, -1], w_bf


def _rope_tables(tile_base, tm):
    half = HEAD_DIM // 2
    inv = ROPE_BASE ** (-jnp.arange(half, dtype=F32) / half)
    inv = jnp.concatenate([inv, inv])
    sign = jnp.concatenate([-jnp.ones((half,), F32), jnp.ones((half,), F32)])
    a = tile_base.astype(F32)[:, None] * inv[None, :]
    b = jnp.arange(tm, dtype=F32)[:, None] * inv[None, :]
    base_tab = jnp.stack([jnp.cos(a), jnp.sin(a)], axis=1)
    base_tab = jnp.pad(base_tab, ((0, 0), (0, 6), (0, 0)))
    rot_tab = jnp.stack([jnp.cos(b), jnp.sin(b), sign * jnp.cos(b), sign * jnp.sin(b)])
    return base_tab, rot_tab


def _layer(x, seq_start, seq_len, mk, mv, w, *, alpha, tm_proj, tm_mix, tm_ffn, tf, t_ret, s0, conv_past,
           fox_fn, attn_block):
    tile_base = (seq_start[:, None] + jnp.arange(0, seq_len, tm_proj)[None, :]).reshape(-1)
    base_tab, rot_tab = _rope_tables(tile_base, tm_proj)
    fq, fk, fv, lf, c, rq, rk, rv, rg, mo, qa, ka, va = _in_proj(
        x, w["wa"], w["wf"], w["bf"], base_tab, rot_tab, mk, mv, tm=tm_proj, seq_len=seq_len,
        d_fox=w["d_fox"], d_ret=w["d_ret"], d_memq=w["d_memq"], attn_block=attn_block)
    fo = fox_fn(fq, fk, fv, lf, c, qa, ka, va)
    ro, s_new = _retention(rq, rk, rv, rg, s0, w["gn_g"], t_blk=t_ret, seq_len=seq_len,
                           n_heads=w["d_ret"] // HEAD_DIM)
    w_bf = {}
    h1 = _mix_out(x, fo, ro, mo, w["wo"], w["ln1_g"], w["ln1_b"], tm=tm_mix, alpha=alpha)
    if w["wo"].dtype != BF16:
        h1, w_bf["wo"] = h1
    y, cv, ffn_bf = _conv_ffn(h1, w["wg"], w["wu"], w["wd"], w["conv_w"], w["conv_b"], conv_past,
                              w["ln2_g"], w["ln2_b"], tm=tm_ffn, tf=tf, seq_len=seq_len, alpha=alpha)
    w_bf.update(zip(("wg", "wu", "wd"), ffn_bf))
    return y, fk, fv, lf, s_new, cv, w_bf


def kernel(x_prompt, x_sample, cache_fox_k, cache_fox_v, cache_fox_logf, state_ret, cache_mem_k,
           cache_mem_v, state_conv, mem_prompt, w_in, b_f, w_mem_kv, ret_gn_g, w_o, ln1_g, ln1_b,
           w_gate, w_up, conv_w, conv_b, w_down, ln2_g, ln2_b):
    batch, seq, d_model = x_prompt.shape
    dec_batch, dec_seq, _ = x_sample.shape
    depth = w_in.shape[0]
    past_len = cache_fox_k.shape[2]
    n_fox = cache_fox_k.shape[3]
    n_ret = state_ret.shape[2]
    n_mem = cache_mem_k.shape[3]
    n_mem_tok = cache_mem_k.shape[2]
    d_fox, d_ret, d_memq = n_fox * HEAD_DIM, n_ret * HEAD_DIM, n_mem * HEAD_DIM
    d_ff = w_gate.shape[2]
    assert batch == 1 and depth == 1 and n_fox <= 8
    alpha = (2.0 * depth) ** 0.25
    l = 0

    o_ff = 3 * d_fox
    w_fm = jnp.transpose(w_in, (2, 0, 1))
    wa, wf = _w_prep(w_fm, o_ff=o_ff, n_fox=n_fox, tn=2 * LANES)
    bfp = jnp.pad(b_f[l], (0, LANES - n_fox)).reshape(1, LANES)
    w = dict(wa=wa, wf=wf, bf=bfp, d_fox=d_fox, d_ret=d_ret, d_memq=d_memq,
             gn_g=ret_gn_g[l].reshape(1, d_ret), wo=w_o[l],
             ln1_g=ln1_g[l].reshape(1, d_model), ln1_b=ln1_b[l].reshape(1, d_model),
             wg=w_gate[l], wu=w_up[l], wd=w_down[l],
             conv_w=conv_w[l], conv_b=conv_b[l].reshape(1, d_ff),
             ln2_g=ln2_g[l].reshape(1, d_model), ln2_b=ln2_b[l].reshape(1, d_model))

    to_head_major = lambda t: jnp.swapaxes(t, -3, -2)

    m_s = dec_batch * dec_seq
    kc = to_head_major(cache_fox_k[l])
    vc = to_head_major(cache_fox_v[l])
    lfc_row = jnp.pad(jnp.swapaxes(cache_fox_logf[l], 1, 2), ((0, 0), (0, HEAD_ROWS - n_fox), (0, 0)))

    def fox_sample_fn(fq, fk, fv, lf, c, qa, ka, va):
        lfn_row = jnp.swapaxes(lf.reshape(dec_batch, dec_seq, LANES)[:, :, :HEAD_ROWS], 1, 2)
        return _fox_sample(fq, kc, vc, fk, fv, lfc_row, lfn_row, c, n_heads=n_fox, seq_len=dec_seq)

    y_s, fk_s, fv_s, lf_s, s_s, cv_s, w_bf = _layer(
        x_sample.reshape(m_s, d_model), jnp.full((dec_batch,), past_len, jnp.int32), dec_seq,
        to_head_major(cache_mem_k[l]), to_head_major(cache_mem_v[l]), w,
        alpha=alpha, tm_proj=dec_seq, tm_mix=m_s, tm_ffn=m_s, tf=2 * LANES, t_ret=dec_seq,
        s0=state_ret[l], conv_past=state_conv[l], fox_fn=fox_sample_fn, attn_block=dec_seq)

    mkv = _mem_kv(mem_prompt.reshape(n_mem_tok, d_model), w_mem_kv[l].astype(BF16))
    mk_p, mv_p = mkv[None, :n_mem], mkv[None, n_mem:]

    fox_tb = 512

    def fox_prompt_fn(fq, fk, fv, lf, c, qa, ka, va):
        return _fox_prompt(qa, ka, va, c, tb=fox_tb, nsub=4)

    y_p, fk_p, fv_p, lf_p, s_p, cv_p, _ = _layer(
        x_prompt.reshape(seq, d_model), jnp.zeros((batch,), jnp.int32), seq, mk_p, mv_p, {**w, **w_bf},
        alpha=alpha, tm_proj=256, tm_mix=512, tm_ffn=1024, tf=512, t_ret=256,
        s0=jnp.zeros((1, n_ret, HEAD_DIM, HEAD_DIM), F32),
        conv_past=jnp.zeros((1, CONV_W - 1, d_ff), F32), fox_fn=fox_prompt_fn, attn_block=fox_tb)

    return (y_p.reshape(batch, seq, d_model), y_s.reshape(dec_batch, dec_seq, d_model),
            to_head_major(fk_p)[None], to_head_major(fv_p)[None],
            lf_p[:, :n_fox].reshape(1, batch, seq, n_fox),
            s_p.reshape(1, batch, n_ret, HEAD_DIM, HEAD_DIM),
            to_head_major(mk_p)[None], to_head_major(mv_p)[None],
            cv_p.reshape(1, batch, CONV_W - 1, d_ff),
            to_head_major(fk_s)[None], to_head_major(fv_s)[None],
            lf_s[:, :n_fox].reshape(1, dec_batch, dec_seq, n_fox),
            s_s.reshape(1, dec_batch, n_ret, HEAD_DIM, HEAD_DIM),
            cv_s.reshape(1, dec_batch, CONV_W - 1, d_ff))
```

```python
import functools
import math

import numpy as np
import jax
import jax.numpy as jnp
from jax import lax
from jax.experimental import pallas as pl
from jax.experimental.pallas import tpu as pltpu

F32 = jnp.float32
BF16 = jnp.bfloat16

HEAD_DIM = 128
CHUNK = 64
CONV_W = 3
ROPE_BASE = 10000.0
LN_EPS = 1e-5
GN_EPS = 1e-5
LANES = 128
V7X_VMEM_BYTES = 64 * 1024 * 1024
VMEM_LIMIT = V7X_VMEM_BYTES - 2 * 1024 * 1024
SCALE = HEAD_DIM ** -0.5
LOG2E = math.log2(math.e)
NEG_INF = float("-inf")
HEAD_ROWS = 16


def _cparams(sem):
    return pltpu.CompilerParams(dimension_semantics=sem, vmem_limit_bytes=VMEM_LIMIT)


def _resident(shape):
    nd = len(shape)
    return pl.BlockSpec(shape, lambda *_: (0,) * nd, pipeline_mode=pl.Buffered(1))


def _dot(a, b):
    return jnp.dot(a, b, preferred_element_type=F32)


def _dot_nt(a, b):
    return lax.dot_general(a, b, (((1,), (1,)), ((), ())), preferred_element_type=F32)


def _dot_tn(a, b):
    return lax.dot_general(a, b, (((0,), (0,)), ((), ())), preferred_element_type=F32)


def _split3_dot(a, tri):
    hi = a.astype(BF16)
    r1 = a - hi.astype(F32)
    mid = r1.astype(BF16)
    lo = (r1 - mid.astype(F32)).astype(BF16)
    return _dot(hi, tri) + _dot(mid, tri) + _dot(lo, tri)


def _layernorm(z, g, b):
    mu = jnp.mean(z, axis=-1, keepdims=True)
    zc = z - mu
    var = jnp.mean(zc * zc, axis=-1, keepdims=True)
    return zc * lax.rsqrt(var + LN_EPS) * g + b


def _memkv_kernel(m_ref, w_ref, o_ref):
    kv = _dot(m_ref[...].astype(BF16), w_ref[...])
    for j in range(o_ref.shape[0]):
        o_ref[j] = kv[:, j * HEAD_DIM:(j + 1) * HEAD_DIM]


def _mem_kv(mem, w_bf):
    m, _ = mem.shape
    return pl.pallas_call(
        _memkv_kernel,
        out_shape=jax.ShapeDtypeStruct((w_bf.shape[1] // HEAD_DIM, m, HEAD_DIM), F32),
        compiler_params=pltpu.CompilerParams(vmem_limit_bytes=VMEM_LIMIT),
        name="mem_kv",
    )(mem, w_bf)


def _w_prep_kernel(w_hbm, wa_ref, wf_ref, buf, sem, *, o_ff, n_fox, tn):
    j = pl.program_id(0)
    n = pl.num_programs(0)

    def copy(blk, slot):
        r0 = jnp.where(blk * tn < o_ff, blk * tn, blk * tn + n_fox)
        return pltpu.make_async_copy(w_hbm.at[pl.ds(r0, tn)], buf.at[slot], sem.at[slot])

    @pl.when(j == 0)
    def _():
        gate = pltpu.make_async_copy(w_hbm.at[pl.ds(o_ff, tn)], buf.at[1], sem.at[1])
        gate.start()
        copy(0, 0).start()
        gate.wait()
        rows = buf[1, 0:LANES, 0, :]
        keep = lax.broadcasted_iota(jnp.int32, rows.shape, 0) < n_fox
        wf_ref[...] = jnp.where(keep, rows, 0.0).T.astype(BF16)

    @pl.when(j + 1 < n)
    def _():
        copy(j + 1, (j + 1) % 2).start()

    copy(j, j % 2).wait()
    wa_ref[...] = buf[j % 2, :, 0, :].T.astype(BF16)


def _w_prep(w_fm, *, o_ff, n_fox, tn):
    n, _, d_model = w_fm.shape
    assert o_ff % tn == 0 and (n - n_fox) % tn == 0 and tn >= LANES and o_ff + tn <= n
    return pl.pallas_call(
        functools.partial(_w_prep_kernel, o_ff=o_ff, n_fox=n_fox, tn=tn),
        grid=((n - n_fox) // tn,),
        in_specs=[pl.BlockSpec(memory_space=pl.ANY)],
        out_specs=(pl.BlockSpec((d_model, tn), lambda j: (0, j)),
                   pl.BlockSpec((d_model, LANES), lambda j: (0, 0))),
        out_shape=(jax.ShapeDtypeStruct((d_model, n - n_fox), BF16),
                   jax.ShapeDtypeStruct((d_model, LANES), BF16)),
        scratch_shapes=[pltpu.VMEM((2, tn, 1, d_model), F32), pltpu.SemaphoreType.DMA((2,))],
        compiler_params=_cparams(("arbitrary",)),
        name="w_prep",
    )(w_fm)


def _in_proj_kernel(x_ref, wa_ref, wf_ref, bf_ref, base_ref, rot_ref, mk_ref, mv_ref, tri_ref,
                    fq_ref, fk_ref, fv_ref, lf_ref, c_ref, rq_ref, rk_ref, rv_ref, rg_ref, mo_ref,
                    qa_ref, ka_ref, va_ref, carry_ref, start_ref, *, d_fox, d_ret, d_memq, tiles_per_seq,
                    blk_tiles):
    i = pl.program_id(0)
    xb = x_ref[...].astype(BF16)

    def proj(lo, width):
        return _dot(xb, wa_ref[:, lo:lo + width])

    o_fk = d_fox
    o_fv = 2 * d_fox
    o_rq = 3 * d_fox
    o_rk = o_rq + d_ret
    o_rv = o_rk + d_ret
    o_rg = o_rv + d_ret
    o_mq = o_rg + d_ret

    heads = lambda d: [slice(h * HEAD_DIM, (h + 1) * HEAD_DIM) for h in range(d // HEAD_DIM)]

    z = _dot(xb, wf_ref[...]) + bf_ref[...]
    fq_ref[...] = (proj(0, d_fox) * (SCALE * LOG2E)).astype(BF16)
    lf = jnp.minimum(z, 0.0) - jnp.log1p(jnp.exp(-jnp.abs(z)))
    lf_ref[...] = lf

    @pl.when(i % tiles_per_seq == 0)
    def _():
        carry_ref[...] = jnp.zeros_like(carry_ref)

    parts = _dot(tri_ref[...], jnp.concatenate([t.astype(BF16) for t in _split3_f32(lf)], axis=1))
    c = parts[:, 0:LANES] + parts[:, LANES:2 * LANES] + parts[:, 2 * LANES:] + carry_ref[0:1, :]
    c_ref[...] = c
    tm = c.shape[0]
    carry_ref[...] = jnp.broadcast_to(c[tm - 1:tm, :], carry_ref.shape)

    @pl.when(i % blk_tiles == 0)
    def _():
        start_ref[...] = jnp.broadcast_to(c[0:1, :], start_ref.shape)

    rel3 = _split3_f32((c - start_ref[0:1, :]) * LOG2E)
    zfk = proj(o_fk, d_fox)
    zfv = proj(o_fv, d_fox)
    for h, sl in enumerate(heads(d_fox)):
        fk_ref[0, h] = zfk[:, sl]
        fv_ref[0, h] = zfv[:, sl]
        qa_ref[0, h], ka_ref[0, h], va_ref[0, h] = _fox_operands(fq_ref[:, sl], zfk[:, sl], zfv[:, sl], rel3, h)

    ca = base_ref[0, 0:1, :]
    sa = base_ref[0, 1:2, :]
    cos = ca * rot_ref[0] - sa * rot_ref[1]
    sin = sa * rot_ref[2] + ca * rot_ref[3]

    def rope(t):
        return t * cos + pltpu.roll(t, HEAD_DIM // 2, axis=1) * sin

    def ret_q():
        zq = proj(o_rq, d_ret)
        for sl in heads(d_ret):
            rq_ref[:, sl] = rope(zq[:, sl]).astype(BF16)

    def ret_k():
        zk = proj(o_rk, d_ret)
        for sl in heads(d_ret):
            rk_ref[:, sl] = (rope(zk[:, sl]) * SCALE).astype(BF16)

    def ret_v():
        rv_ref[...] = proj(o_rv, d_ret).astype(BF16)

    def ret_g():
        rg_ref[...] = proj(o_rg, d_ret).astype(BF16)

    zm = proj(o_mq, d_memq)
    pending = [ret_q, ret_k, ret_v, ret_g]
    for h, sl in enumerate(heads(d_memq)):
        s = _dot_nt((zm[:, sl] * SCALE).astype(BF16), mk_ref[0, h].astype(BF16))
        p = jnp.exp(s - jnp.max(s, axis=-1, keepdims=True))
        if pending:
            pending.pop(0)()
        o = _dot(p.astype(BF16), mv_ref[0, h].astype(BF16))
        mo_ref[:, sl] = (o / jnp.sum(p, axis=-1, keepdims=True)).astype(BF16)
    for projection in pending:
        projection()


def _in_proj(x, wa, wf, bfp, base_tab, rot_tab, mk, mv, *, tm, seq_len, d_fox, d_ret, d_memq, attn_block):
    m, d_model = x.shape
    assert m % tm == 0 and seq_len % tm == 0
    assert base_tab.shape == (m // tm, 8, HEAD_DIM) and rot_tab.shape == (4, tm, HEAD_DIM)
    tiles_per_seq = seq_len // tm
    n_fox = d_fox // HEAD_DIM
    tri = jnp.asarray(np.arange(tm)[:, None] >= np.arange(tm)[None, :], dtype=BF16)
    row = lambda w: pl.BlockSpec((tm, w), lambda i: (i, 0))
    memspec = pl.BlockSpec((1,) + mk.shape[1:], lambda i: (i // tiles_per_seq, 0, 0, 0))
    kvspec = pl.BlockSpec((1, n_fox, tm, HEAD_DIM), lambda i: (i // tiles_per_seq, 0, i % tiles_per_seq, 0))
    augspec = pl.BlockSpec((1, n_fox, tm, 2 * HEAD_DIM),
                           lambda i: (i // tiles_per_seq, 0, i % tiles_per_seq, 0))
    aug = jax.ShapeDtypeStruct((m // seq_len, n_fox, seq_len, 2 * HEAD_DIM), BF16)
    assert attn_block % tm == 0 and seq_len % attn_block == 0
    kern = functools.partial(_in_proj_kernel, d_fox=d_fox, d_ret=d_ret, d_memq=d_memq,
                             tiles_per_seq=tiles_per_seq, blk_tiles=attn_block // tm)
    out_shape = (
        jax.ShapeDtypeStruct((m, d_fox), BF16),
        jax.ShapeDtypeStruct((m // seq_len, n_fox, seq_len, HEAD_DIM), F32),
        jax.ShapeDtypeStruct((m // seq_len, n_fox, seq_len, HEAD_DIM), F32),
        jax.ShapeDtypeStruct((m, LANES), F32),
        jax.ShapeDtypeStruct((m, LANES), F32),
        jax.ShapeDtypeStruct((m, d_ret), BF16),
        jax.ShapeDtypeStruct((m, d_ret), BF16),
        jax.ShapeDtypeStruct((m, d_ret), BF16),
        jax.ShapeDtypeStruct((m, d_ret), BF16),
        jax.ShapeDtypeStruct((m, d_memq), BF16),
        aug, aug, aug,
    )
    return pl.pallas_call(
        kern,
        grid=(m // tm,),
        in_specs=[row(d_model), _resident(wa.shape), _resident(wf.shape), _resident(bfp.shape),
                  pl.BlockSpec((1, 8, HEAD_DIM), lambda i: (i, 0, 0)), _resident(rot_tab.shape),
                  memspec, memspec, _resident(tri.shape)],
        out_specs=(row(d_fox), kvspec, kvspec, row(LANES), row(LANES),
                   row(d_ret), row(d_ret), row(d_ret), row(d_ret), row(d_memq),
                   augspec, augspec, augspec),
        out_shape=out_shape,
        scratch_shapes=[pltpu.VMEM((8, LANES), F32), pltpu.VMEM((8, LANES), F32)],
        compiler_params=_cparams(("arbitrary",)),
        name="in_proj",
    )(x, wa, wf, bfp, base_tab, rot_tab, mk, mv, tri)


def _split3_f32(a):
    hi = a.astype(BF16).astype(F32)
    r1 = a - hi
    mid = r1.astype(BF16).astype(F32)
    lo = (r1 - mid).astype(BF16).astype(F32)
    return hi, mid, lo


def _fox_operands(q, k, v, rel3, h):
    lane = lax.broadcasted_iota(jnp.int32, q.shape, 1)
    hi, mid, lo = (t[:, h:h + 1] for t in rel3)
    qbias = jnp.where(lane == 0, hi, jnp.where(lane == 1, mid, jnp.where(lane == 2, lo,
                      jnp.where(lane < 6, 1.0, 0.0))))
    kbias = jnp.where(lane < 3, 1.0, jnp.where(lane == 3, -hi, jnp.where(lane == 4, -mid,
                      jnp.where(lane == 5, -lo, 0.0))))
    ones_col = jnp.where(lane == 0, 1.0, 0.0)
    cat = lambda a, b: jnp.concatenate([a.astype(BF16), b.astype(BF16)], axis=1)
    return cat(q, qbias), cat(k, kbias), cat(v, ones_col)


def _fox_prompt_kernel(qa, k_ref, v_ref, ccol_ref, o_ref, kbuf, vbuf, cstart, acc, sbuf, tbuf, *, tb, nsub,
                       kv_per_trip):
    h = pl.program_id(0)
    qi = pl.program_id(1)
    tq = tb * nsub
    row0 = pl.multiple_of(qi * tq, tq)
    kbuf[pl.ds(row0, tq), :] = k_ref[...]
    vbuf[pl.ds(row0, tq), :] = v_ref[...]
    lane = lax.broadcasted_iota(jnp.int32, (8, LANES), 1)
    for s in range(nsub):
        start = jnp.sum(jnp.where(lane == h, ccol_ref[s * tb:s * tb + 8, :], 0.0), axis=1, keepdims=True)
        cstart[nsub * qi + s] = jnp.broadcast_to(start[0:1], (8, LANES))
    acc[...] = jnp.zeros_like(acc)

    r_id = lax.broadcasted_iota(jnp.int32, (tb, LANES), 0)
    c_id = lax.broadcasted_iota(jnp.int32, (tb, LANES), 1)
    n_ct = tb // LANES
    subs = range(nsub)

    def lane_tiles(s2):
        return [s2[:, t * LANES:(t + 1) * LANES] for t in range(n_ct)]

    def scores(s, kj):
        col0 = pl.multiple_of(kj * tb, tb)
        s2 = _dot_nt(qa[s * tb:(s + 1) * tb, :], kbuf[pl.ds(col0, tb), :])
        return s2, functools.reduce(jnp.maximum, lane_tiles(s2))

    def softmax_pv(s, kj, sc, m, masked):
        col0 = pl.multiple_of(kj * tb, tb)
        s2, tmax = sc
        tiles = lane_tiles(s2)
        if masked:
            tiles = [jnp.where(c_id + t * LANES <= r_id, tiles[t], NEG_INF) for t in range(n_ct)]
            tmax = functools.reduce(jnp.maximum, tiles)
        delta = (cstart[nsub * qi + s][0:1, :] - cstart[kj][0:1, :]) * LOG2E
        m_rel = m - delta
        m_new = jnp.maximum(m_rel, jnp.max(tmax, axis=-1, keepdims=True))
        alpha = jnp.exp2(m_rel - m_new)
        p = jnp.concatenate([jnp.exp2(t - m_new).astype(BF16) for t in tiles], axis=1)
        pv = _dot(p, vbuf[pl.ds(col0, tb), :])
        a = acc[s]
        acc[s] = jnp.concatenate([alpha * a[:, 0:LANES], alpha * a[:, LANES:]], axis=1) + pv
        return m_new + delta

    for s in subs:
        sbuf[s], tbuf[s] = scores(s, 0)

    def body(t, ms):
        kj = kv_per_trip * t
        cur = [(sbuf[s], tbuf[s]) for s in subs]
        for u in range(kv_per_trip):
            nxt = [scores(s, kj + u + 1) for s in subs]
            ms = [softmax_pv(s, kj + u, cur[s], ms[s], False) for s in subs]
            cur = nxt
        for s in subs:
            sbuf[s], tbuf[s] = cur[s]
        return tuple(ms)

    assert nsub % kv_per_trip == 0
    ms = lax.fori_loop(0, (nsub // kv_per_trip) * qi, body,
                       tuple(jnp.full((tb, LANES), NEG_INF, F32) for _ in subs))
    for s in subs:
        m = ms[s]
        for t in range(s + 1):
            sc = (sbuf[s], tbuf[s]) if t == 0 else scores(s, nsub * qi + t)
            m = softmax_pv(s, nsub * qi + t, sc, m, t == s)
        a = acc[s]
        o_ref[s * tb:(s + 1) * tb, :] = (a[:, 0:HEAD_DIM] / a[:, HEAD_DIM:HEAD_DIM + 1]).astype(BF16)


def _fox_prompt(qa, ka, va, c_col, *, tb, nsub):
    _, n_heads, s, width = qa.shape
    tq = tb * nsub
    assert s % tq == 0
    op = lambda: pl.BlockSpec((None, None, tq, width), lambda h, qi: (0, h, qi, 0))
    return pl.pallas_call(
        functools.partial(_fox_prompt_kernel, tb=tb, nsub=nsub, kv_per_trip=4),
        grid=(n_heads, s // tq),
        in_specs=[op(), op(), op(), pl.BlockSpec((tq, LANES), lambda h, qi: (qi, 0))],
        out_specs=pl.BlockSpec((tq, HEAD_DIM), lambda h, qi: (qi, h)),
        out_shape=jax.ShapeDtypeStruct((s, n_heads * HEAD_DIM), BF16),
        scratch_shapes=[pltpu.VMEM((s, width), BF16),
                        pltpu.VMEM((s, width), BF16),
                        pltpu.VMEM((s // tb, 8, LANES), F32),
                        pltpu.VMEM((nsub, tb, width), F32),
                        pltpu.VMEM((nsub, tb, tb), F32),
                        pltpu.VMEM((nsub, tb, LANES), F32)],
        compiler_params=_cparams(("arbitrary", "arbitrary")),
        name="fox_prompt",
    )(qa, ka, va, c_col)


def _fox_sample_kernel(q_ref, kc_ref, vc_ref, kn_ref, vn_ref, lfc_ref, lfn_ref, ccol_ref,
                       tric_ref, trin_ref, o_ref, *, n_heads):
    c_cache = _split3_dot(lfc_ref[0], tric_ref[...])
    p_len = c_cache.shape[1]
    ck_cache = c_cache - c_cache[:, p_len - 1:p_len]
    ck_new = _split3_dot(lfn_ref[0], trin_ref[...])
    ccol = ccol_ref[...]
    n_new = ck_new.shape[1]
    r = lax.broadcasted_iota(jnp.int32, (n_new, n_new), 0)
    c = lax.broadcasted_iota(jnp.int32, (n_new, n_new), 1)
    for h in range(n_heads):
        sl = slice(h * HEAD_DIM, (h + 1) * HEAD_DIM)
        q = q_ref[:, sl]
        cq = ccol[:, h:h + 1]
        s_c = _dot_nt(q, kc_ref[0, h].astype(BF16)) + (cq - ck_cache[h:h + 1, :]) * LOG2E
        s_n = _dot_nt(q, kn_ref[0, h].astype(BF16)) + (cq - ck_new[h:h + 1, :]) * LOG2E
        s_n = jnp.where(c <= r, s_n, NEG_INF)
        m = jnp.maximum(jnp.max(s_c, axis=-1, keepdims=True), jnp.max(s_n, axis=-1, keepdims=True))
        p_c = jnp.exp2(s_c - m)
        p_n = jnp.exp2(s_n - m)
        l = jnp.sum(p_c, axis=-1, keepdims=True) + jnp.sum(p_n, axis=-1, keepdims=True)
        o = _dot(p_c.astype(BF16), vc_ref[0, h].astype(BF16)) + \
            _dot(p_n.astype(BF16), vn_ref[0, h].astype(BF16))
        o_ref[:, sl] = (o / l).astype(BF16)


def _fox_sample(fq, kc, vc, kn, vn, lfc_row, lfn_row, c_col, *, n_heads, seq_len):
    m, d_fox = fq.shape
    nb, _, p_len, _ = kc.shape
    tric = jnp.asarray(np.arange(p_len)[:, None] <= np.arange(p_len)[None, :], dtype=BF16)
    trin = jnp.asarray(np.arange(seq_len)[:, None] <= np.arange(seq_len)[None, :], dtype=BF16)
    row = lambda w: pl.BlockSpec((seq_len, w), lambda b: (b, 0))
    cache = pl.BlockSpec((1, n_heads, p_len, HEAD_DIM), lambda b: (b, 0, 0, 0))
    new = pl.BlockSpec((1, n_heads, seq_len, HEAD_DIM), lambda b: (b, 0, 0, 0))
    return pl.pallas_call(
        functools.partial(_fox_sample_kernel, n_heads=n_heads),
        grid=(nb,),
        in_specs=[row(d_fox), cache, cache, new, new,
                  pl.BlockSpec((1, HEAD_ROWS, p_len), lambda b: (b, 0, 0)),
                  pl.BlockSpec((1, HEAD_ROWS, seq_len), lambda b: (b, 0, 0)),
                  row(LANES), _resident(tric.shape), _resident(trin.shape)],
        out_specs=row(d_fox),
        out_shape=jax.ShapeDtypeStruct((m, d_fox), BF16),
        compiler_params=_cparams(("arbitrary",)),
        name="fox_sample",
    )(fq, kc, vc, kn, vn, lfc_row, lfn_row, c_col, tric, trin)


def _log_gamma(h):
    return float(np.log1p(-np.exp2(np.float32(-5.0 - h)), dtype=np.float32))


def _retention_kernel(q_ref, k_ref, v_ref, g_ref, s0_ref, gn_ref, o_ref, s_ref, decay_ref, *, n_heads,
                      t_blk):
    step = pl.program_id(1)

    @pl.when(step == 0)
    def _():
        s_ref[...] = s0_ref[...]

    @pl.when(jnp.logical_and(pl.program_id(0) == 0, step == 0))
    def _():
        ti = lax.broadcasted_iota(jnp.int32, (t_blk, t_blk), 0)
        si = lax.broadcasted_iota(jnp.int32, (t_blk, t_blk), 1)
        dist = jnp.abs(ti - si).astype(F32)
        shift = CHUNK.bit_length() - 1
        visible = jnp.right_shift(si, shift) <= jnp.right_shift(ti, shift)
        for h in range(n_heads):
            decay_ref[h] = jnp.where(visible, jnp.exp(_log_gamma(h) * dist), 0.0)

    tcol = lax.broadcasted_iota(jnp.int32, (t_blk, 1), 0).astype(F32)
    for h in range(n_heads):
        lg = _log_gamma(h)
        sl = slice(h * HEAD_DIM, (h + 1) * HEAD_DIM)
        q = q_ref[:, sl]
        k = k_ref[:, sl]
        v = v_ref[:, sl]
        a = _dot_nt(q, k) * decay_ref[h]
        state = s_ref[0, h]
        o = _dot(a.astype(BF16), v) + _dot(q, state.astype(BF16)) * jnp.exp(lg * (tcol + 1.0))
        kd = (k.astype(F32) * jnp.exp(lg * (t_blk - 1.0 - tcol))).astype(BF16)
        s_ref[0, h] = math.exp(lg * t_blk) * state + _dot_tn(kd, v)
        mu = jnp.mean(o, axis=-1, keepdims=True)
        oc = o - mu
        var = jnp.mean(oc * oc, axis=-1, keepdims=True)
        rn = oc * lax.rsqrt(var + GN_EPS) * gn_ref[:, sl]
        gate = g_ref[:, sl].astype(F32)
        o_ref[:, sl] = (gate / (1.0 + jnp.exp(-gate)) * rn).astype(BF16)


def _retention(rq, rk, rv, rg, s0, gn_g, *, t_blk, seq_len, n_heads):
    m, d_ret = rq.shape
    nb = m // seq_len
    steps = seq_len // t_blk
    row = lambda: pl.BlockSpec((t_blk, d_ret), lambda b, s: (b * steps + s, 0))
    st = lambda: pl.BlockSpec((1, n_heads, HEAD_DIM, HEAD_DIM), lambda b, s: (b, 0, 0, 0))
    return pl.pallas_call(
        functools.partial(_retention_kernel, n_heads=n_heads, t_blk=t_blk),
        grid=(nb, steps),
        in_specs=[row(), row(), row(), row(), st(), pl.BlockSpec((1, d_ret), lambda b, s: (0, 0))],
        out_specs=(row(), st()),
        out_shape=(jax.ShapeDtypeStruct((m, d_ret), BF16),
                   jax.ShapeDtypeStruct((nb, n_heads, HEAD_DIM, HEAD_DIM), F32)),
        scratch_shapes=[pltpu.VMEM((n_heads, t_blk, t_blk), F32)],
        compiler_params=_cparams(("arbitrary", "arbitrary")),
        name="retention",
    )(rq, rk, rv, rg, s0, gn_g)


def _mix_out_kernel(x_ref, fo_ref, ro_ref, mo_ref, wo_ref, g_ref, b_ref, o_ref, *rest, alpha, emit_weight):
    wo_out, cat_ref = rest if emit_weight else (None,) + rest
    d_fox = fo_ref.shape[1]
    d_ret = ro_ref.shape[1]
    cat_ref[:, 0:d_fox] = fo_ref[...]
    cat_ref[:, d_fox:d_fox + d_ret] = ro_ref[...]
    cat_ref[:, d_fox + d_ret:] = mo_ref[...]
    wo = wo_ref[...]
    if emit_weight:
        wo = wo.astype(BF16)
        wo_out[...] = wo
    tm = x_ref.shape[0]
    pieces = [slice(0, tm // 2), slice(tm // 2, tm)] if tm % 32 == 0 else [slice(0, tm)]
    mix = [_dot(cat_ref[r, :], wo) for r in pieces]
    for r, mx in zip(pieces, mix):
        o_ref[r, :] = _layernorm(alpha * x_ref[r, :] + mx, g_ref[...], b_ref[...])


def _mix_out(x, fo, ro, mo, wo, g, b, *, tm, alpha):
    m, d_model = x.shape
    emit = wo.dtype != BF16
    assert not emit or m == tm
    row = lambda w: pl.BlockSpec((tm, w), lambda i: (i, 0))
    out_specs, out_shape = row(d_model), jax.ShapeDtypeStruct((m, d_model), F32)
    if emit:
        out_specs = (out_specs, pl.BlockSpec(wo.shape, lambda i: (0, 0)))
        out_shape = (out_shape, jax.ShapeDtypeStruct(wo.shape, BF16))
    return pl.pallas_call(
        functools.partial(_mix_out_kernel, alpha=alpha, emit_weight=emit),
        grid=(m // tm,),
        in_specs=[row(d_model), row(fo.shape[1]), row(ro.shape[1]), row(mo.shape[1]),
                  _resident(wo.shape), _resident(g.shape), _resident(b.shape)],
        out_specs=out_specs,
        out_shape=out_shape,
        scratch_shapes=[pltpu.VMEM((tm, wo.shape[0]), BF16)],
        compiler_params=_cparams(("arbitrary",)),
        name="mix_out",
    )(x, fo, ro, mo, wo, g, b)


def _gelu_tanh(x):
    c = math.sqrt(2.0 / math.pi)
    return x * (0.5 * (1.0 + jnp.tanh(c * (x + 0.044715 * (x * x * x)))))


def _conv_ffn_kernel(h_ref, wg_ref, wu_ref, wd_ref, cw_ref, cb_ref, past_ref, g_ref, b_ref,
                     y_ref, cv_ref, *rest, alpha, seq_len, tm, tc, emit_weights):
    wg_out, wu_out, wd_out = rest[:3] if emit_weights else (None, None, None)
    hb_ref, gbuf_ref, carry_ref = rest[3:] if emit_weights else rest

    def weight(ref, out, idx):
        w = ref[idx]
        if out is not None:
            w = w.astype(BF16)
            out[idx] = w
        return w

    i = pl.program_id(0)
    f = pl.program_id(1)
    nf = pl.num_programs(1)
    lt = min(seq_len, tm)
    ns = tm // lt
    halo = 8
    carried = seq_len > tm

    @pl.when(f == 0)
    def _():
        h = h_ref[...]
        hb_ref[...] = h.astype(BF16)
        y_ref[...] = alpha * h

    if carried:
        @pl.when(jnp.logical_and(i == 0, f == 0))
        def _():
            carry_ref[...] = jnp.zeros_like(carry_ref)

        tiles_per_seq = seq_len // tm
        gbuf_ref[0, halo - 2:halo, :] = jnp.where(i % tiles_per_seq == 0, past_ref[i // tiles_per_seq],
                                                  carry_ref[f])
    else:
        for s in range(ns):
            gbuf_ref[s, halo - 2:halo, :] = past_ref[i * ns + s]

    hb = hb_ref[...]
    hid = []
    for c0 in range(0, wg_ref.shape[1], tc):
        cs = slice(c0, c0 + tc)
        gate = _dot(hb, weight(wg_ref, wg_out, (slice(None), cs)))
        up = _dot(hb, weight(wu_ref, wu_out, (slice(None), cs)))
        w0 = cw_ref[0:1, cs]
        w1 = cw_ref[1:2, cs]
        w2 = cw_ref[2:3, cs]
        cb = cb_ref[:, cs]
        parts = []
        for s in range(ns):
            gs = gate[s * lt:(s + 1) * lt]
            gbuf_ref[s, halo:halo + lt, cs] = gs
            g1 = gbuf_ref[s, halo - 1:halo - 1 + lt, cs]
            g2 = gbuf_ref[s, halo - 2:halo - 2 + lt, cs]
            gc = cb + w0 * g2 + w1 * g1 + w2 * gs
            parts.append((_gelu_tanh(gc) * up[s * lt:(s + 1) * lt]).astype(BF16))
        hid.append(parts[0] if ns == 1 else jnp.concatenate(parts, axis=0))
    hid = hid[0] if len(hid) == 1 else jnp.concatenate(hid, axis=1)
    y_ref[...] += _dot(hid, weight(wd_ref, wd_out, (slice(None), slice(None))))

    for s in range(ns):
        last2 = gbuf_ref[s, halo + lt - 2:halo + lt, :]
        cv_ref[s] = last2
        if carried:
            carry_ref[f] = last2

    @pl.when(f == nf - 1)
    def _():
        y_ref[...] = _layernorm(y_ref[...], g_ref[...], b_ref[...])


def _conv_ffn(h1, wg, wu, wd, conv_w, conv_b, past, g, b, *, tm, tf, seq_len, alpha):
    tc = 2 * LANES
    assert tf % tc == 0
    m, d_model = h1.shape
    d_ff = wg.shape[1]
    n_seq = past.shape[0]
    assert m % tm == 0 and d_ff % tf == 0 and (seq_len % tm == 0 or tm % seq_len == 0)
    nf = d_ff // tf
    lt = min(seq_len, tm)
    emit = wg.dtype != BF16
    assert not emit or m == tm
    w_specs = [pl.BlockSpec((d_model, tf), lambda i, f: (0, f)),
               pl.BlockSpec((d_model, tf), lambda i, f: (0, f)),
               pl.BlockSpec((tf, d_model), lambda i, f: (f, 0))]
    w_shapes = [jax.ShapeDtypeStruct(t.shape, BF16) for t in (wg, wu, wd)] if emit else []
    y, cv_all, *w_bf = pl.pallas_call(
        functools.partial(_conv_ffn_kernel, alpha=alpha, seq_len=seq_len, tm=tm, tc=tc, emit_weights=emit),
        grid=(m // tm, nf),
        in_specs=[pl.BlockSpec((tm, d_model), lambda i, f: (i, 0)),
                  *w_specs,
                  pl.BlockSpec((CONV_W, tf), lambda i, f: (0, f)),
                  pl.BlockSpec((1, tf), lambda i, f: (0, f)),
                  pl.BlockSpec((n_seq, CONV_W - 1, tf), lambda i, f: (0, 0, f)),
                  pl.BlockSpec((1, d_model), lambda i, f: (0, 0)),
                  pl.BlockSpec((1, d_model), lambda i, f: (0, 0))],
        out_specs=(pl.BlockSpec((tm, d_model), lambda i, f: (i, 0)),
                   pl.BlockSpec((tm // lt, CONV_W - 1, tf), lambda i, f: (i, 0, f)),
                   *(w_specs if emit else [])),
        out_shape=(jax.ShapeDtypeStruct((m, d_model), F32),
                   jax.ShapeDtypeStruct((m // lt, CONV_W - 1, d_ff), F32), *w_shapes),
        scratch_shapes=[pltpu.VMEM((tm, d_model), BF16),
                        pltpu.VMEM((tm // lt, 8 + lt, tf), F32),
                        pltpu.VMEM((nf, CONV_W - 1, tf), F32)],
        compiler_params=_cparams(("arbitrary", "arbitrary")),
        name="conv_ffn",
    )(h1, wg, wu, wd, conv_w, conv_b, past, g, b)
    return y, cv_all.reshape(n_seq, seq_len // lt, CONV_W - 1, d_ff)[:, -1], w_bf


def _rope_tables(tile_base, tm):
    half = HEAD_DIM // 2
    inv = ROPE_BASE ** (-jnp.arange(half, dtype=F32) / half)
    inv = jnp.concatenate([inv, inv])
    sign = jnp.concatenate([-jnp.ones((half,), F32), jnp.ones((half,), F32)])
    a = tile_base.astype(F32)[:, None] * inv[None, :]
    b = jnp.arange(tm, dtype=F32)[:, None] * inv[None, :]
    base_tab = jnp.stack([jnp.cos(a), jnp.sin(a)], axis=1)
    base_tab = jnp.pad(base_tab, ((0, 0), (0, 6), (0, 0)))
    rot_tab = jnp.stack([jnp.cos(b), jnp.sin(b), sign * jnp.cos(b), sign * jnp.sin(b)])
    return base_tab, rot_tab


def _layer(x, seq_start, seq_len, mk, mv, w, *, alpha, tm_proj, tm_mix, tm_ffn, tf, t_ret, s0, conv_past,
           fox_fn, attn_block):
    tile_base = (seq_start[:, None] + jnp.arange(0, seq_len, tm_proj)[None, :]).reshape(-1)
    base_tab, rot_tab = _rope_tables(tile_base, tm_proj)
    fq, fk, fv, lf, c, rq, rk, rv, rg, mo, qa, ka, va = _in_proj(
        x, w["wa"], w["wf"], w["bf"], base_tab, rot_tab, mk, mv, tm=tm_proj, seq_len=seq_len,
        d_fox=w["d_fox"], d_ret=w["d_ret"], d_memq=w["d_memq"], attn_block=attn_block)
    fo = fox_fn(fq, fk, fv, lf, c, qa, ka, va)
    ro, s_new = _retention(rq, rk, rv, rg, s0, w["gn_g"], t_blk=t_ret, seq_len=seq_len,
                           n_heads=w["d_ret"] // HEAD_DIM)
    w_bf = {}
    h1 = _mix_out(x, fo, ro, mo, w["wo"], w["ln1_g"], w["ln1_b"], tm=tm_mix, alpha=alpha)
    if w["wo"].dtype != BF16:
        h1, w_bf["wo"] = h1
    y, cv, ffn_bf = _conv_ffn(h1, w["wg"], w["wu"], w["wd"], w["conv_w"], w["conv_b"], conv_past,
                              w["ln2_g"], w["ln2_b"], tm=tm_ffn, tf=tf, seq_len=seq_len, alpha=alpha)
    w_bf.update(zip(("wg", "wu", "wd"), ffn_bf))
    return y, fk, fv, lf, s_new, cv, w_bf


def kernel(x_prompt, x_sample, cache_fox_k, cache_fox_v, cache_fox_logf, state_ret, cache_mem_k,
           cache_mem_v, state_conv, mem_prompt, w_in, b_f, w_mem_kv, ret_gn_g, w_o, ln1_g, ln1_b,
           w_gate, w_up, conv_w, conv_b, w_down, ln2_g, ln2_b):
    batch, seq, d_model = x_prompt.shape
    dec_batch, dec_seq, _ = x_sample.shape
    depth = w_in.shape[0]
    past_len = cache_fox_k.shape[2]
    n_fox = cache_fox_k.shape[3]
    n_ret = state_ret.shape[2]
    n_mem = cache_mem_k.shape[3]
    n_mem_tok = cache_mem_k.shape[2]
    d_fox, d_ret, d_memq = n_fox * HEAD_DIM, n_ret * HEAD_DIM, n_mem * HEAD_DIM
    d_ff = w_gate.shape[2]
    assert batch == 1 and depth == 1 and n_fox <= 8
    alpha = (2.0 * depth) ** 0.25
    l = 0

    o_ff = 3 * d_fox
    w_fm = jnp.transpose(w_in, (2, 0, 1))
    wa, wf = _w_prep(w_fm, o_ff=o_ff, n_fox=n_fox, tn=2 * LANES)
    bfp = jnp.pad(b_f[l], (0, LANES - n_fox)).reshape(1, LANES)
    w = dict(wa=wa, wf=wf, bf=bfp, d_fox=d_fox, d_ret=d_ret, d_memq=d_memq,
             gn_g=ret_gn_g[l].reshape(1, d_ret), wo=w_o[l],
             ln1_g=ln1_g[l].reshape(1, d_model), ln1_b=ln1_b[l].reshape(1, d_model),
             wg=w_gate[l], wu=w_up[l], wd=w_down[l],
             conv_w=conv_w[l], conv_b=conv_b[l].reshape(1, d_ff),
             ln2_g=ln2_g[l].reshape(1, d_model), ln2_b=ln2_b[l].reshape(1, d_model))

    to_head_major = lambda t: jnp.swapaxes(t, -3, -2)

    m_s = dec_batch * dec_seq
    kc = to_head_major(cache_fox_k[l])
    vc = to_head_major(cache_fox_v[l])
    lfc_row = jnp.pad(jnp.swapaxes(cache_fox_logf[l], 1, 2), ((0, 0), (0, HEAD_ROWS - n_fox), (0, 0)))

    def fox_sample_fn(fq, fk, fv, lf, c, qa, ka, va):
        lfn_row = jnp.swapaxes(lf.reshape(dec_batch, dec_seq, LANES)[:, :, :HEAD_ROWS], 1, 2)
        return _fox_sample(fq, kc, vc, fk, fv, lfc_row, lfn_row, c, n_heads=n_fox, seq_len=dec_seq)

    y_s, fk_s, fv_s, lf_s, s_s, cv_s, w_bf = _layer(
        x_sample.reshape(m_s, d_model), jnp.full((dec_batch,), past_len, jnp.int32), dec_seq,
        to_head_major(cache_mem_k[l]), to_head_major(cache_mem_v[l]), w,
        alpha=alpha, tm_proj=dec_seq, tm_mix=m_s, tm_ffn=m_s, tf=2 * LANES, t_ret=dec_seq,
        s0=state_ret[l], conv_past=state_conv[l], fox_fn=fox_sample_fn, attn_block=dec_seq)

    mkv = _mem_kv(mem_prompt.reshape(n_mem_tok, d_model), w_mem_kv[l].astype(BF16))
    mk_p, mv_p = mkv[None, :n_mem], mkv[None, n_mem:]

    fox_tb = 512

    def fox_prompt_fn(fq, fk, fv, lf, c, qa, ka, va):
        return _fox_prompt(qa, ka, va, c, tb=fox_tb, nsub=4)

    y_p, fk_p, fv_p, lf_p, s_p, cv_p, _ = _layer(
        x_prompt.reshape(seq, d_model), jnp.zeros((batch,), jnp.int32), seq, mk_p, mv_p, {**w, **w_bf},
        alpha=alpha, tm_proj=256, tm_mix=512, tm_ffn=1024, tf=512, t_ret=256,
        s0=jnp.zeros((1, n_ret, HEAD_DIM, HEAD_DIM), F32),
        conv_past=jnp.zeros((1, CONV_W - 1, d_ff), F32), fox_fn=fox_prompt_fn, attn_block=fox_tb)

    return (y_p.reshape(batch, seq, d_model), y_s.reshape(dec_batch, dec_seq, d_model),
            to_head_major(fk_p)[None], to_head_major(fv_p)[None],
            lf_p[:, :n_fox].reshape(1, batch, seq, n_fox),
            s_p.reshape(1, batch, n_ret, HEAD_DIM, HEAD_DIM),
            to_head_major(mk_p)[None], to_head_major(mv_p)[None],
            cv_p.reshape(1, batch, CONV_W - 1, d_ff),
            to_head_major(fk_s)[None], to_head_major(fv_s)[None],
            lf_s[:, :n_fox].reshape(1, dec_batch, dec_seq, n_fox),
            s_s.reshape(1, dec_batch, n_ret, HEAD_DIM, HEAD_DIM),
            cv_s.reshape(1, dec_batch, CONV_W - 1, d_ff))
```

```python
import functools
import math

import numpy as np
import jax
import jax.numpy as jnp
from jax import lax
from jax.experimental import pallas as pl
from jax.experimental.pallas import tpu as pltpu

F32 = jnp.float32
BF16 = jnp.bfloat16

HEAD_DIM = 128
CHUNK = 64
CONV_W = 3
ROPE_BASE = 10000.0
LN_EPS = 1e-5
GN_EPS = 1e-5
LANES = 128
V7X_VMEM_BYTES = 64 * 1024 * 1024
VMEM_LIMIT = V7X_VMEM_BYTES - 2 * 1024 * 1024
SCALE = HEAD_DIM ** -0.5
LOG2E = math.log2(math.e)
NEG_INF = float("-inf")
HEAD_ROWS = 16


def _cparams(sem):
    return pltpu.CompilerParams(dimension_semantics=sem, vmem_limit_bytes=VMEM_LIMIT)


def _resident(shape):
    nd = len(shape)
    return pl.BlockSpec(shape, lambda *_: (0,) * nd, pipeline_mode=pl.Buffered(1))


def _dot(a, b):
    return jnp.dot(a, b, preferred_element_type=F32)


def _dot_nt(a, b):
    return lax.dot_general(a, b, (((1,), (1,)), ((), ())), preferred_element_type=F32)


def _dot_tn(a, b):
    return lax.dot_general(a, b, (((0,), (0,)), ((), ())), preferred_element_type=F32)


def _split3_dot(a, tri):
    hi = a.astype(BF16)
    r1 = a - hi.astype(F32)
    mid = r1.astype(BF16)
    lo = (r1 - mid.astype(F32)).astype(BF16)
    return _dot(hi, tri) + _dot(mid, tri) + _dot(lo, tri)


def _layernorm(z, g, b):
    mu = jnp.mean(z, axis=-1, keepdims=True)
    zc = z - mu
    var = jnp.mean(zc * zc, axis=-1, keepdims=True)
    return zc * lax.rsqrt(var + LN_EPS) * g + b


def _memkv_kernel(m_ref, w_ref, o_ref):
    kv = _dot(m_ref[...].astype(BF16), w_ref[...])
    for j in range(o_ref.shape[0]):
        o_ref[j] = kv[:, j * HEAD_DIM:(j + 1) * HEAD_DIM]


def _mem_kv(mem, w_bf):
    m, _ = mem.shape
    return pl.pallas_call(
        _memkv_kernel,
        out_shape=jax.ShapeDtypeStruct((w_bf.shape[1] // HEAD_DIM, m, HEAD_DIM), F32),
        compiler_params=pltpu.CompilerParams(vmem_limit_bytes=VMEM_LIMIT),
        name="mem_kv",
    )(mem, w_bf)


def _w_prep_kernel(w_hbm, wa_ref, wf_ref, buf, sem, *, o_ff, n_fox, tn):
    j = pl.program_id(0)
    n = pl.num_programs(0)

    def copy(blk, slot):
        r0 = jnp.where(blk * tn < o_ff, blk * tn, blk * tn + n_fox)
        return pltpu.make_async_copy(w_hbm.at[pl.ds(r0, tn)], buf.at[slot], sem.at[slot])

    @pl.when(j == 0)
    def _():
        gate = pltpu.make_async_copy(w_hbm.at[pl.ds(o_ff, tn)], buf.at[1], sem.at[1])
        gate.start()
        copy(0, 0).start()
        gate.wait()
        rows = buf[1, 0:LANES, 0, :]
        keep = lax.broadcasted_iota(jnp.int32, rows.shape, 0) < n_fox
        wf_ref[...] = jnp.where(keep, rows, 0.0).T.astype(BF16)

    @pl.when(j + 1 < n)
    def _():
        copy(j + 1, (j + 1) % 2).start()

    copy(j, j % 2).wait()
    wa_ref[...] = buf[j % 2, :, 0, :].T.astype(BF16)


def _w_prep(w_fm, *, o_ff, n_fox, tn):
    n, _, d_model = w_fm.shape
    assert o_ff % tn == 0 and (n - n_fox) % tn == 0 and tn >= LANES and o_ff + tn <= n
    return pl.pallas_call(
        functools.partial(_w_prep_kernel, o_ff=o_ff, n_fox=n_fox, tn=tn),
        grid=((n - n_fox) // tn,),
        in_specs=[pl.BlockSpec(memory_space=pl.ANY)],
        out_specs=(pl.BlockSpec((d_model, tn), lambda j: (0, j)),
                   pl.BlockSpec((d_model, LANES), lambda j: (0, 0))),
        out_shape=(jax.ShapeDtypeStruct((d_model, n - n_fox), BF16),
                   jax.ShapeDtypeStruct((d_model, LANES), BF16)),
        scratch_shapes=[pltpu.VMEM((2, tn, 1, d_model), F32), pltpu.SemaphoreType.DMA((2,))],
        compiler_params=_cparams(("arbitrary",)),
        name="w_prep",
    )(w_fm)


def _in_proj_kernel(x_ref, wa_ref, wf_ref, bf_ref, base_ref, rot_ref, mk_ref, mv_ref, tri_ref,
                    fq_ref, fk_ref, fv_ref, lf_ref, c_ref, rq_ref, rk_ref, rv_ref, rg_ref, mo_ref,
                    qa_ref, ka_ref, va_ref, carry_ref, start_ref, *, d_fox, d_ret, d_memq, tiles_per_seq,
                    blk_tiles):
    i = pl.program_id(0)
    xb = x_ref[...].astype(BF16)

    def proj(lo, width):
        return _dot(xb, wa_ref[:, lo:lo + width])

    o_fk = d_fox
    o_fv = 2 * d_fox
    o_rq = 3 * d_fox
    o_rk = o_rq + d_ret
    o_rv = o_rk + d_ret
    o_rg = o_rv + d_ret
    o_mq = o_rg + d_ret

    heads = lambda d: [slice(h * HEAD_DIM, (h + 1) * HEAD_DIM) for h in range(d // HEAD_DIM)]

    z = _dot(xb, wf_ref[...]) + bf_ref[...]
    fq_ref[...] = (proj(0, d_fox) * (SCALE * LOG2E)).astype(BF16)
    lf = jnp.minimum(z, 0.0) - jnp.log1p(jnp.exp(-jnp.abs(z)))
    lf_ref[...] = lf

    @pl.when(i % tiles_per_seq == 0)
    def _():
        carry_ref[...] = jnp.zeros_like(carry_ref)

    parts = _dot(tri_ref[...], jnp.concatenate([t.astype(BF16) for t in _split3_f32(lf)], axis=1))
    c = parts[:, 0:LANES] + parts[:, LANES:2 * LANES] + parts[:, 2 * LANES:] + carry_ref[0:1, :]
    c_ref[...] = c
    tm = c.shape[0]
    carry_ref[...] = jnp.broadcast_to(c[tm - 1:tm, :], carry_ref.shape)

    @pl.when(i % blk_tiles == 0)
    def _():
        start_ref[...] = jnp.broadcast_to(c[0:1, :], start_ref.shape)

    rel3 = _split3_f32((c - start_ref[0:1, :]) * LOG2E)
    zfk = proj(o_fk, d_fox)
    zfv = proj(o_fv, d_fox)
    for h, sl in enumerate(heads(d_fox)):
        fk_ref[0, h] = zfk[:, sl]
        fv_ref[0, h] = zfv[:, sl]
        qa_ref[0, h], ka_ref[0, h], va_ref[0, h] = _fox_operands(fq_ref[:, sl], zfk[:, sl], zfv[:, sl], rel3, h)

    ca = base_ref[0, 0:1, :]
    sa = base_ref[0, 1:2, :]
    cos = ca * rot_ref[0] - sa * rot_ref[1]
    sin = sa * rot_ref[2] + ca * rot_ref[3]

    def rope(t):
        return t * cos + pltpu.roll(t, HEAD_DIM // 2, axis=1) * sin

    def ret_q():
        zq = proj(o_rq, d_ret)
        for sl in heads(d_ret):
            rq_ref[:, sl] = rope(zq[:, sl]).astype(BF16)

    def ret_k():
        zk = proj(o_rk, d_ret)
        for sl in heads(d_ret):
            rk_ref[:, sl] = (rope(zk[:, sl]) * SCALE).astype(BF16)

    def ret_v():
        rv_ref[...] = proj(o_rv, d_ret).astype(BF16)

    def ret_g():
        rg_ref[...] = proj(o_rg, d_ret).astype(BF16)

    zm = proj(o_mq, d_memq)
    pending = [ret_q, ret_k, ret_v, ret_g]
    for h, sl in enumerate(heads(d_memq)):
        s = _dot_nt((zm[:, sl] * SCALE).astype(BF16), mk_ref[0, h].astype(BF16))
        p = jnp.exp(s - jnp.max(s, axis=-1, keepdims=True))
        if pending:
            pending.pop(0)()
        o = _dot(p.astype(BF16), mv_ref[0, h].astype(BF16))
        mo_ref[:, sl] = (o / jnp.sum(p, axis=-1, keepdims=True)).astype(BF16)
    for projection in pending:
        projection()


def _in_proj(x, wa, wf, bfp, base_tab, rot_tab, mk, mv, *, tm, seq_len, d_fox, d_ret, d_memq, attn_block):
    m, d_model = x.shape
    assert m % tm == 0 and seq_len % tm == 0
    assert base_tab.shape == (m // tm, 8, HEAD_DIM) and rot_tab.shape == (4, tm, HEAD_DIM)
    tiles_per_seq = seq_len // tm
    n_fox = d_fox // HEAD_DIM
    tri = jnp.asarray(np.arange(tm)[:, None] >= np.arange(tm)[None, :], dtype=BF16)
    row = lambda w: pl.BlockSpec((tm, w), lambda i: (i, 0))
    memspec = pl.BlockSpec((1,) + mk.shape[1:], lambda i: (i // tiles_per_seq, 0, 0, 0))
    kvspec = pl.BlockSpec((1, n_fox, tm, HEAD_DIM), lambda i: (i // tiles_per_seq, 0, i % tiles_per_seq, 0))
    augspec = pl.BlockSpec((1, n_fox, tm, 2 * HEAD_DIM),
                           lambda i: (i // tiles_per_seq, 0, i % tiles_per_seq, 0))
    aug = jax.ShapeDtypeStruct((m // seq_len, n_fox, seq_len, 2 * HEAD_DIM), BF16)
    assert attn_block % tm == 0 and seq_len % attn_block == 0
    kern = functools.partial(_in_proj_kernel, d_fox=d_fox, d_ret=d_ret, d_memq=d_memq,
                             tiles_per_seq=tiles_per_seq, blk_tiles=attn_block // tm)
    out_shape = (
        jax.ShapeDtypeStruct((m, d_fox), BF16),
        jax.ShapeDtypeStruct((m // seq_len, n_fox, seq_len, HEAD_DIM), F32),
        jax.ShapeDtypeStruct((m // seq_len, n_fox, seq_len, HEAD_DIM), F32),
        jax.ShapeDtypeStruct((m, LANES), F32),
        jax.ShapeDtypeStruct((m, LANES), F32),
        jax.ShapeDtypeStruct((m, d_ret), BF16),
        jax.ShapeDtypeStruct((m, d_ret), BF16),
        jax.ShapeDtypeStruct((m, d_ret), BF16),
        jax.ShapeDtypeStruct((m, d_ret), BF16),
        jax.ShapeDtypeStruct((m, d_memq), BF16),
        aug, aug, aug,
    )
    return pl.pallas_call(
        kern,
        grid=(m // tm,),
        in_specs=[row(d_model), _resident(wa.shape), _resident(wf.shape), _resident(bfp.shape),
                  pl.BlockSpec((1, 8, HEAD_DIM), lambda i: (i, 0, 0)), _resident(rot_tab.shape),
                  memspec, memspec, _resident(tri.shape)],
        out_specs=(row(d_fox), kvspec, kvspec, row(LANES), row(LANES),
                   row(d_ret), row(d_ret), row(d_ret), row(d_ret), row(d_memq),
                   augspec, augspec, augspec),
        out_shape=out_shape,
        scratch_shapes=[pltpu.VMEM((8, LANES), F32), pltpu.VMEM((8, LANES), F32)],
        compiler_params=_cparams(("arbitrary",)),
        name="in_proj",
    )(x, wa, wf, bfp, base_tab, rot_tab, mk, mv, tri)


def _split3_f32(a):
    hi = a.astype(BF16).astype(F32)
    r1 = a - hi
    mid = r1.astype(BF16).astype(F32)
    lo = (r1 - mid).astype(BF16).astype(F32)
    return hi, mid, lo


def _fox_operands(q, k, v, rel3, h):
    lane = lax.broadcasted_iota(jnp.int32, q.shape, 1)
    hi, mid, lo = (t[:, h:h + 1] for t in rel3)
    qbias = jnp.where(lane == 0, hi, jnp.where(lane == 1, mid, jnp.where(lane == 2, lo,
                      jnp.where(lane < 6, 1.0, 0.0))))
    kbias = jnp.where(lane < 3, 1.0, jnp.where(lane == 3, -hi, jnp.where(lane == 4, -mid,
                      jnp.where(lane == 5, -lo, 0.0))))
    ones_col = jnp.where(lane == 0, 1.0, 0.0)
    cat = lambda a, b: jnp.concatenate([a.astype(BF16), b.astype(BF16)], axis=1)
    return cat(q, qbias), cat(k, kbias), cat(v, ones_col)


def _fox_prompt_kernel(qa, k_ref, v_ref, ccol_ref, o_ref, kbuf, vbuf, cstart, acc, sbuf, tbuf, *, tb, nsub,
                       kv_per_trip):
    h = pl.program_id(0)
    qi = pl.program_id(1)
    tq = tb * nsub
    row0 = pl.multiple_of(qi * tq, tq)
    kbuf[pl.ds(row0, tq), :] = k_ref[...]
    vbuf[pl.ds(row0, tq), :] = v_ref[...]
    lane = lax.broadcasted_iota(jnp.int32, (8, LANES), 1)
    for s in range(nsub):
        start = jnp.sum(jnp.where(lane == h, ccol_ref[s * tb:s * tb + 8, :], 0.0), axis=1, keepdims=True)
        cstart[nsub * qi + s] = jnp.broadcast_to(start[0:1], (8, LANES))
    acc[...] = jnp.zeros_like(acc)

    r_id = lax.broadcasted_iota(jnp.int32, (tb, LANES), 0)
    c_id = lax.broadcasted_iota(jnp.int32, (tb, LANES), 1)
    n_ct = tb // LANES
    subs = range(nsub)

    def lane_tiles(s2):
        return [s2[:, t * LANES:(t + 1) * LANES] for t in range(n_ct)]

    def scores(s, kj):
        col0 = pl.multiple_of(kj * tb, tb)
        s2 = _dot_nt(qa[s * tb:(s + 1) * tb, :], kbuf[pl.ds(col0, tb), :])
        return s2, functools.reduce(jnp.maximum, lane_tiles(s2))

    def softmax_pv(s, kj, sc, m, masked):
        col0 = pl.multiple_of(kj * tb, tb)
        s2, tmax = sc
        tiles = lane_tiles(s2)
        if masked:
            tiles = [jnp.where(c_id + t * LANES <= r_id, tiles[t], NEG_INF) for t in range(n_ct)]
            tmax = functools.reduce(jnp.maximum, tiles)
        delta = (cstart[nsub * qi + s][0:1, :] - cstart[kj][0:1, :]) * LOG2E
        m_rel = m - delta
        m_new = jnp.maximum(m_rel, jnp.max(tmax, axis=-1, keepdims=True))
        alpha = jnp.exp2(m_rel - m_new)
        p = jnp.concatenate([jnp.exp2(t - m_new).astype(BF16) for t in tiles], axis=1)
        pv = _dot(p, vbuf[pl.ds(col0, tb), :])
        a = acc[s]
        acc[s] = jnp.concatenate([alpha * a[:, 0:LANES], alpha * a[:, LANES:]], axis=1) + pv
        return m_new + delta

    for s in subs:
        sbuf[s], tbuf[s] = scores(s, 0)

    def body(t, ms):
        kj = kv_per_trip * t
        cur = [(sbuf[s], tbuf[s]) for s in subs]
        for u in range(kv_per_trip):
            nxt = [scores(s, kj + u + 1) for s in subs]
            ms = [softmax_pv(s, kj + u, cur[s], ms[s], False) for s in subs]
            cur = nxt
        for s in subs:
            sbuf[s], tbuf[s] = cur[s]
        return tuple(ms)

    assert nsub % kv_per_trip == 0
    ms = lax.fori_loop(0, (nsub // kv_per_trip) * qi, body,
                       tuple(jnp.full((tb, LANES), NEG_INF, F32) for _ in subs))
    for s in subs:
        m = ms[s]
        for t in range(s + 1):
            sc = (sbuf[s], tbuf[s]) if t == 0 else scores(s, nsub * qi + t)
            m = softmax_pv(s, nsub * qi + t, sc, m, t == s)
        a = acc[s]
        o_ref[s * tb:(s + 1) * tb, :] = (a[:, 0:HEAD_DIM] / a[:, HEAD_DIM:HEAD_DIM + 1]).astype(BF16)


def _fox_prompt(qa, ka, va, c_col, *, tb, nsub):
    _, n_heads, s, width = qa.shape
    tq = tb * nsub
    assert s % tq == 0
    op = lambda: pl.BlockSpec((None, None, tq, width), lambda h, qi: (0, h, qi, 0))
    return pl.pallas_call(
        functools.partial(_fox_prompt_kernel, tb=tb, nsub=nsub, kv_per_trip=min(nsub, 4)),
        grid=(n_heads, s // tq),
        in_specs=[op(), op(), op(), pl.BlockSpec((tq, LANES), lambda h, qi: (qi, 0))],
        out_specs=pl.BlockSpec((tq, HEAD_DIM), lambda h, qi: (qi, h)),
        out_shape=jax.ShapeDtypeStruct((s, n_heads * HEAD_DIM), BF16),
        scratch_shapes=[pltpu.VMEM((s, width), BF16),
                        pltpu.VMEM((s, width), BF16),
                        pltpu.VMEM((s // tb, 8, LANES), F32),
                        pltpu.VMEM((nsub, tb, width), F32),
                        pltpu.VMEM((nsub, tb, tb), F32),
                        pltpu.VMEM((nsub, tb, LANES), F32)],
        compiler_params=_cparams(("arbitrary", "arbitrary")),
        name="fox_prompt",
    )(qa, ka, va, c_col)


def _fox_sample_kernel(q_ref, kc_ref, vc_ref, kn_ref, vn_ref, lfc_ref, lfn_ref, ccol_ref,
                       tric_ref, trin_ref, o_ref, *, n_heads):
    c_cache = _split3_dot(lfc_ref[0], tric_ref[...])
    p_len = c_cache.shape[1]
    ck_cache = c_cache - c_cache[:, p_len - 1:p_len]
    ck_new = _split3_dot(lfn_ref[0], trin_ref[...])
    ccol = ccol_ref[...]
    n_new = ck_new.shape[1]
    r = lax.broadcasted_iota(jnp.int32, (n_new, n_new), 0)
    c = lax.broadcasted_iota(jnp.int32, (n_new, n_new), 1)
    for h in range(n_heads):
        sl = slice(h * HEAD_DIM, (h + 1) * HEAD_DIM)
        q = q_ref[:, sl]
        cq = ccol[:, h:h + 1]
        s_c = _dot_nt(q, kc_ref[0, h].astype(BF16)) + (cq - ck_cache[h:h + 1, :]) * LOG2E
        s_n = _dot_nt(q, kn_ref[0, h].astype(BF16)) + (cq - ck_new[h:h + 1, :]) * LOG2E
        s_n = jnp.where(c <= r, s_n, NEG_INF)
        m = jnp.maximum(jnp.max(s_c, axis=-1, keepdims=True), jnp.max(s_n, axis=-1, keepdims=True))
        p_c = jnp.exp2(s_c - m)
        p_n = jnp.exp2(s_n - m)
        l = jnp.sum(p_c, axis=-1, keepdims=True) + jnp.sum(p_n, axis=-1, keepdims=True)
        o = _dot(p_c.astype(BF16), vc_ref[0, h].astype(BF16)) + \
            _dot(p_n.astype(BF16), vn_ref[0, h].astype(BF16))
        o_ref[:, sl] = (o / l).astype(BF16)


def _fox_sample(fq, kc, vc, kn, vn, lfc_row, lfn_row, c_col, *, n_heads, seq_len):
    m, d_fox = fq.shape
    nb, _, p_len, _ = kc.shape
    tric = jnp.asarray(np.arange(p_len)[:, None] <= np.arange(p_len)[None, :], dtype=BF16)
    trin = jnp.asarray(np.arange(seq_len)[:, None] <= np.arange(seq_len)[None, :], dtype=BF16)
    row = lambda w: pl.BlockSpec((seq_len, w), lambda b: (b, 0))
    cache = pl.BlockSpec((1, n_heads, p_len, HEAD_DIM), lambda b: (b, 0, 0, 0))
    new = pl.BlockSpec((1, n_heads, seq_len, HEAD_DIM), lambda b: (b, 0, 0, 0))
    return pl.pallas_call(
        functools.partial(_fox_sample_kernel, n_heads=n_heads),
        grid=(nb,),
        in_specs=[row(d_fox), cache, cache, new, new,
                  pl.BlockSpec((1, HEAD_ROWS, p_len), lambda b: (b, 0, 0)),
                  pl.BlockSpec((1, HEAD_ROWS, seq_len), lambda b: (b, 0, 0)),
                  row(LANES), _resident(tric.shape), _resident(trin.shape)],
        out_specs=row(d_fox),
        out_shape=jax.ShapeDtypeStruct((m, d_fox), BF16),
        compiler_params=_cparams(("arbitrary",)),
        name="fox_sample",
    )(fq, kc, vc, kn, vn, lfc_row, lfn_row, c_col, tric, trin)


def _log_gamma(h):
    return float(np.log1p(-np.exp2(np.float32(-5.0 - h)), dtype=np.float32))


def _retention_kernel(q_ref, k_ref, v_ref, g_ref, s0_ref, gn_ref, o_ref, s_ref, decay_ref, *, n_heads,
                      t_blk):
    step = pl.program_id(1)

    @pl.when(step == 0)
    def _():
        s_ref[...] = s0_ref[...]

    @pl.when(jnp.logical_and(pl.program_id(0) == 0, step == 0))
    def _():
        ti = lax.broadcasted_iota(jnp.int32, (t_blk, t_blk), 0)
        si = lax.broadcasted_iota(jnp.int32, (t_blk, t_blk), 1)
        dist = jnp.abs(ti - si).astype(F32)
        shift = CHUNK.bit_length() - 1
        visible = jnp.right_shift(si, shift) <= jnp.right_shift(ti, shift)
        for h in range(n_heads):
            decay_ref[h] = jnp.where(visible, jnp.exp(_log_gamma(h) * dist), 0.0)

    tcol = lax.broadcasted_iota(jnp.int32, (t_blk, 1), 0).astype(F32)
    for h in range(n_heads):
        lg = _log_gamma(h)
        sl = slice(h * HEAD_DIM, (h + 1) * HEAD_DIM)
        q = q_ref[:, sl]
        k = k_ref[:, sl]
        v = v_ref[:, sl]
        a = _dot_nt(q, k) * decay_ref[h]
        state = s_ref[0, h]
        o = _dot(a.astype(BF16), v) + _dot(q, state.astype(BF16)) * jnp.exp(lg * (tcol + 1.0))
        kd = (k.astype(F32) * jnp.exp(lg * (t_blk - 1.0 - tcol))).astype(BF16)
        s_ref[0, h] = math.exp(lg * t_blk) * state + _dot_tn(kd, v)
        mu = jnp.mean(o, axis=-1, keepdims=True)
        oc = o - mu
        var = jnp.mean(oc * oc, axis=-1, keepdims=True)
        rn = oc * lax.rsqrt(var + GN_EPS) * gn_ref[:, sl]
        gate = g_ref[:, sl].astype(F32)
        o_ref[:, sl] = (gate / (1.0 + jnp.exp(-gate)) * rn).astype(BF16)


def _retention(rq, rk, rv, rg, s0, gn_g, *, t_blk, seq_len, n_heads):
    m, d_ret = rq.shape
    nb = m // seq_len
    steps = seq_len // t_blk
    row = lambda: pl.BlockSpec((t_blk, d_ret), lambda b, s: (b * steps + s, 0))
    st = lambda: pl.BlockSpec((1, n_heads, HEAD_DIM, HEAD_DIM), lambda b, s: (b, 0, 0, 0))
    return pl.pallas_call(
        functools.partial(_retention_kernel, n_heads=n_heads, t_blk=t_blk),
        grid=(nb, steps),
        in_specs=[row(), row(), row(), row(), st(), pl.BlockSpec((1, d_ret), lambda b, s: (0, 0))],
        out_specs=(row(), st()),
        out_shape=(jax.ShapeDtypeStruct((m, d_ret), BF16),
                   jax.ShapeDtypeStruct((nb, n_heads, HEAD_DIM, HEAD_DIM), F32)),
        scratch_shapes=[pltpu.VMEM((n_heads, t_blk, t_blk), F32)],
        compiler_params=_cparams(("arbitrary", "arbitrary")),
        name="retention",
    )(rq, rk, rv, rg, s0, gn_g)


def _mix_out_kernel(x_ref, fo_ref, ro_ref, mo_ref, wo_ref, g_ref, b_ref, o_ref, *rest, alpha, emit_weight):
    wo_out, cat_ref = rest if emit_weight else (None,) + rest
    d_fox = fo_ref.shape[1]
    d_ret = ro_ref.shape[1]
    cat_ref[:, 0:d_fox] = fo_ref[...]
    cat_ref[:, d_fox:d_fox + d_ret] = ro_ref[...]
    cat_ref[:, d_fox + d_ret:] = mo_ref[...]
    wo = wo_ref[...]
    if emit_weight:
        wo = wo.astype(BF16)
        wo_out[...] = wo
    tm = x_ref.shape[0]
    pieces = [slice(0, tm // 2), slice(tm // 2, tm)] if tm % 32 == 0 else [slice(0, tm)]
    mix = [_dot(cat_ref[r, :], wo) for r in pieces]
    for r, mx in zip(pieces, mix):
        o_ref[r, :] = _layernorm(alpha * x_ref[r, :] + mx, g_ref[...], b_ref[...])


def _mix_out(x, fo, ro, mo, wo, g, b, *, tm, alpha):
    m, d_model = x.shape
    emit = wo.dtype != BF16
    assert not emit or m == tm
    row = lambda w: pl.BlockSpec((tm, w), lambda i: (i, 0))
    out_specs, out_shape = row(d_model), jax.ShapeDtypeStruct((m, d_model), F32)
    if emit:
        out_specs = (out_specs, pl.BlockSpec(wo.shape, lambda i: (0, 0)))
        out_shape = (out_shape, jax.ShapeDtypeStruct(wo.shape, BF16))
    return pl.pallas_call(
        functools.partial(_mix_out_kernel, alpha=alpha, emit_weight=emit),
        grid=(m // tm,),
        in_specs=[row(d_model), row(fo.shape[1]), row(ro.shape[1]), row(mo.shape[1]),
                  _resident(wo.shape), _resident(g.shape), _resident(b.shape)],
        out_specs=out_specs,
        out_shape=out_shape,
        scratch_shapes=[pltpu.VMEM((tm, wo.shape[0]), BF16)],
        compiler_params=_cparams(("arbitrary",)),
        name="mix_out",
    )(x, fo, ro, mo, wo, g, b)


def _gelu_tanh(x):
    c = math.sqrt(2.0 / math.pi)
    return x * (0.5 * (1.0 + jnp.tanh(c * (x + 0.044715 * (x * x * x)))))


def _conv_ffn_kernel(h_ref, wg_ref, wu_ref, wd_ref, cw_ref, cb_ref, past_ref, g_ref, b_ref,
                     y_ref, cv_ref, *rest, alpha, seq_len, tm, tc, emit_weights):
    wg_out, wu_out, wd_out = rest[:3] if emit_weights else (None, None, None)
    hb_ref, gbuf_ref, carry_ref = rest[3:] if emit_weights else rest

    def weight(ref, out, idx):
        w = ref[idx]
        if out is not None:
            w = w.astype(BF16)
            out[idx] = w
        return w

    i = pl.program_id(0)
    f = pl.program_id(1)
    nf = pl.num_programs(1)
    lt = min(seq_len, tm)
    ns = tm // lt
    halo = 8
    carried = seq_len > tm

    @pl.when(f == 0)
    def _():
        h = h_ref[...]
        hb_ref[...] = h.astype(BF16)
        y_ref[...] = alpha * h

    if carried:
        @pl.when(jnp.logical_and(i == 0, f == 0))
        def _():
            carry_ref[...] = jnp.zeros_like(carry_ref)

        tiles_per_seq = seq_len // tm
        gbuf_ref[0, halo - 2:halo, :] = jnp.where(i % tiles_per_seq == 0, past_ref[i // tiles_per_seq],
                                                  carry_ref[f])
    else:
        for s in range(ns):
            gbuf_ref[s, halo - 2:halo, :] = past_ref[i * ns + s]

    hb = hb_ref[...]
    hid = []
    for c0 in range(0, wg_ref.shape[1], tc):
        cs = slice(c0, c0 + tc)
        gate = _dot(hb, weight(wg_ref, wg_out, (slice(None), cs)))
        up = _dot(hb, weight(wu_ref, wu_out, (slice(None), cs)))
        w0 = cw_ref[0:1, cs]
        w1 = cw_ref[1:2, cs]
        w2 = cw_ref[2:3, cs]
        cb = cb_ref[:, cs]
        parts = []
        for s in range(ns):
            gs = gate[s * lt:(s + 1) * lt]
            gbuf_ref[s, halo:halo + lt, cs] = gs
            g1 = gbuf_ref[s, halo - 1:halo - 1 + lt, cs]
            g2 = gbuf_ref[s, halo - 2:halo - 2 + lt, cs]
            gc = cb + w0 * g2 + w1 * g1 + w2 * gs
            parts.append((_gelu_tanh(gc) * up[s * lt:(s + 1) * lt]).astype(BF16))
        hid.append(parts[0] if ns == 1 else jnp.concatenate(parts, axis=0))
    hid = hid[0] if len(hid) == 1 else jnp.concatenate(hid, axis=1)
    y_ref[...] += _dot(hid, weight(wd_ref, wd_out, (slice(None), slice(None))))

    for s in range(ns):
        last2 = gbuf_ref[s, halo + lt - 2:halo + lt, :]
        cv_ref[s] = last2
        if carried:
            carry_ref[f] = last2

    @pl.when(f == nf - 1)
    def _():
        y_ref[...] = _layernorm(y_ref[...], g_ref[...], b_ref[...])


def _conv_ffn(h1, wg, wu, wd, conv_w, conv_b, past, g, b, *, tm, tf, seq_len, alpha):
    tc = 2 * LANES
    assert tf % tc == 0
    m, d_model = h1.shape
    d_ff = wg.shape[1]
    n_seq = past.shape[0]
    assert m % tm == 0 and d_ff % tf == 0 and (seq_len % tm == 0 or tm % seq_len == 0)
    nf = d_ff // tf
    lt = min(seq_len, tm)
    emit = wg.dtype != BF16
    assert not emit or m == tm
    w_specs = [pl.BlockSpec((d_model, tf), lambda i, f: (0, f)),
               pl.BlockSpec((d_model, tf), lambda i, f: (0, f)),
               pl.BlockSpec((tf, d_model), lambda i, f: (f, 0))]
    w_shapes = [jax.ShapeDtypeStruct(t.shape, BF16) for t in (wg, wu, wd)] if emit else []
    y, cv_all, *w_bf = pl.pallas_call(
        functools.partial(_conv_ffn_kernel, alpha=alpha, seq_len=seq_len, tm=tm, tc=tc, emit_weights=emit),
        grid=(m // tm, nf),
        in_specs=[pl.BlockSpec((tm, d_model), lambda i, f: (i, 0)),
                  *w_specs,
                  pl.BlockSpec((CONV_W, tf), lambda i, f: (0, f)),
                  pl.BlockSpec((1, tf), lambda i, f: (0, f)),
                  pl.BlockSpec((n_seq, CONV_W - 1, tf), lambda i, f: (0, 0, f)),
                  pl.BlockSpec((1, d_model), lambda i, f: (0, 0)),
                  pl.BlockSpec((1, d_model), lambda i, f: (0, 0))],
        out_specs=(pl.BlockSpec((tm, d_model), lambda i, f: (i, 0)),
                   pl.BlockSpec((tm // lt, CONV_W - 1, tf), lambda i, f: (i, 0, f)),
                   *(w_specs if emit else [])),
        out_shape=(jax.ShapeDtypeStruct((m, d_model), F32),
                   jax.ShapeDtypeStruct((m // lt, CONV_W - 1, d_ff), F32), *w_shapes),
        scratch_shapes=[pltpu.VMEM((tm, d_model), BF16),
                        pltpu.VMEM((tm // lt, 8 + lt, tf), F32),
                        pltpu.VMEM((nf, CONV_W - 1, tf), F32)],
        compiler_params=_cparams(("arbitrary", "arbitrary")),
        name="conv_ffn",
    )(h1, wg, wu, wd, conv_w, conv_b, past, g, b)
    return y, cv_all.reshape(n_seq, seq_len // lt, CONV_W - 1, d_ff)[:, -1], w_bf


def _rope_tables(tile_base, tm):
    half = HEAD_DIM // 2
    inv = ROPE_BASE ** (-jnp.arange(half, dtype=F32) / half)
    inv = jnp.concatenate([inv, inv])
    sign = jnp.concatenate([-jnp.ones((half,), F32), jnp.ones((half,), F32)])
    a = tile_base.astype(F32)[:, None] * inv[None, :]
    b = jnp.arange(tm, dtype=F32)[:, None] * inv[None, :]
    base_tab = jnp.stack([jnp.cos(a), jnp.sin(a)], axis=1)
    base_tab = jnp.pad(base_tab, ((0, 0), (0, 6), (0, 0)))
    rot_tab = jnp.stack([jnp.cos(b), jnp.sin(b), sign * jnp.cos(b), sign * jnp.sin(b)])
    return base_tab, rot_tab


def _layer(x, seq_start, seq_len, mk, mv, w, *, alpha, tm_proj, tm_mix, tm_ffn, tf, t_ret, s0, conv_past,
           fox_fn, attn_block):
    tile_base = (seq_start[:, None] + jnp.arange(0, seq_len, tm_proj)[None, :]).reshape(-1)
    base_tab, rot_tab = _rope_tables(tile_base, tm_proj)
    fq, fk, fv, lf, c, rq, rk, rv, rg, mo, qa, ka, va = _in_proj(
        x, w["wa"], w["wf"], w["bf"], base_tab, rot_tab, mk, mv, tm=tm_proj, seq_len=seq_len,
        d_fox=w["d_fox"], d_ret=w["d_ret"], d_memq=w["d_memq"], attn_block=attn_block)
    fo = fox_fn(fq, fk, fv, lf, c, qa, ka, va)
    ro, s_new = _retention(rq, rk, rv, rg, s0, w["gn_g"], t_blk=t_ret, seq_len=seq_len,
                           n_heads=w["d_ret"] // HEAD_DIM)
    w_bf = {}
    h1 = _mix_out(x, fo, ro, mo, w["wo"], w["ln1_g"], w["ln1_b"], tm=tm_mix, alpha=alpha)
    if w["wo"].dtype != BF16:
        h1, w_bf["wo"] = h1
    y, cv, ffn_bf = _conv_ffn(h1, w["wg"], w["wu"], w["wd"], w["conv_w"], w["conv_b"], conv_past,
                              w["ln2_g"], w["ln2_b"], tm=tm_ffn, tf=tf, seq_len=seq_len, alpha=alpha)
    w_bf.update(zip(("wg", "wu", "wd"), ffn_bf))
    return y, fk, fv, lf, s_new, cv, w_bf


def kernel(x_prompt, x_sample, cache_fox_k, cache_fox_v, cache_fox_logf, state_ret, cache_mem_k,
           cache_mem_v, state_conv, mem_prompt, w_in, b_f, w_mem_kv, ret_gn_g, w_o, ln1_g, ln1_b,
           w_gate, w_up, conv_w, conv_b, w_down, ln2_g, ln2_b):
    batch, seq, d_model = x_prompt.shape
    dec_batch, dec_seq, _ = x_sample.shape
    depth = w_in.shape[0]
    past_len = cache_fox_k.shape[2]
    n_fox = cache_fox_k.shape[3]
    n_ret = state_ret.shape[2]
    n_mem = cache_mem_k.shape[3]
    n_mem_tok = cache_mem_k.shape[2]
    d_fox, d_ret, d_memq = n_fox * HEAD_DIM, n_ret * HEAD_DIM, n_mem * HEAD_DIM
    d_ff = w_gate.shape[2]
    assert batch == 1 and depth == 1 and n_fox <= 8
    alpha = (2.0 * depth) ** 0.25
    l = 0

    o_ff = 3 * d_fox
    w_fm = jnp.transpose(w_in, (2, 0, 1))
    wa, wf = _w_prep(w_fm, o_ff=o_ff, n_fox=n_fox, tn=2 * LANES)
    bfp = jnp.pad(b_f[l], (0, LANES - n_fox)).reshape(1, LANES)
    w = dict(wa=wa, wf=wf, bf=bfp, d_fox=d_fox, d_ret=d_ret, d_memq=d_memq,
             gn_g=ret_gn_g[l].reshape(1, d_ret), wo=w_o[l],
             ln1_g=ln1_g[l].reshape(1, d_model), ln1_b=ln1_b[l].reshape(1, d_model),
             wg=w_gate[l], wu=w_up[l], wd=w_down[l],
             conv_w=conv_w[l], conv_b=conv_b[l].reshape(1, d_ff),
             ln2_g=ln2_g[l].reshape(1, d_model), ln2_b=ln2_b[l].reshape(1, d_model))

    to_head_major = lambda t: jnp.swapaxes(t, -3, -2)

    m_s = dec_batch * dec_seq
    kc = to_head_major(cache_fox_k[l])
    vc = to_head_major(cache_fox_v[l])
    lfc_row = jnp.pad(jnp.swapaxes(cache_fox_logf[l], 1, 2), ((0, 0), (0, HEAD_ROWS - n_fox), (0, 0)))

    def fox_sample_fn(fq, fk, fv, lf, c, qa, ka, va):
        lfn_row = jnp.swapaxes(lf.reshape(dec_batch, dec_seq, LANES)[:, :, :HEAD_ROWS], 1, 2)
        return _fox_sample(fq, kc, vc, fk, fv, lfc_row, lfn_row, c, n_heads=n_fox, seq_len=dec_seq)

    y_s, fk_s, fv_s, lf_s, s_s, cv_s, w_bf = _layer(
        x_sample.reshape(m_s, d_model), jnp.full((dec_batch,), past_len, jnp.int32), dec_seq,
        to_head_major(cache_mem_k[l]), to_head_major(cache_mem_v[l]), w,
        alpha=alpha, tm_proj=dec_seq, tm_mix=m_s, tm_ffn=m_s, tf=2 * LANES, t_ret=dec_seq,
        s0=state_ret[l], conv_past=state_conv[l], fox_fn=fox_sample_fn, attn_block=dec_seq)

    mkv = _mem_kv(mem_prompt.reshape(n_mem_tok, d_model), w_mem_kv[l].astype(BF16))
    mk_p, mv_p = mkv[None, :n_mem], mkv[None, n_mem:]

    fox_tb = 1024

    def fox_prompt_fn(fq, fk, fv, lf, c, qa, ka, va):
        return _fox_prompt(qa, ka, va, c, tb=fox_tb, nsub=2)

    y_p, fk_p, fv_p, lf_p, s_p, cv_p, _ = _layer(
        x_prompt.reshape(seq, d_model), jnp.zeros((batch,), jnp.int32), seq, mk_p, mv_p, {**w, **w_bf},
        alpha=alpha, tm_proj=256, tm_mix=512, tm_ffn=1024, tf=512, t_ret=256,
        s0=jnp.zeros((1, n_ret, HEAD_DIM, HEAD_DIM), F32),
        conv_past=jnp.zeros((1, CONV_W - 1, d_ff), F32), fox_fn=fox_prompt_fn, attn_block=fox_tb)

    return (y_p.reshape(batch, seq, d_model), y_s.reshape(dec_batch, dec_seq, d_model),
            to_head_major(fk_p)[None], to_head_major(fv_p)[None],
            lf_p[:, :n_fox].reshape(1, batch, seq, n_fox),
            s_p.reshape(1, batch, n_ret, HEAD_DIM, HEAD_DIM),
            to_head_major(mk_p)[None], to_head_major(mv_p)[None],
            cv_p.reshape(1, batch, CONV_W - 1, d_ff),
            to_head_major(fk_s)[None], to_head_major(fv_s)[None],
            lf_s[:, :n_fox].reshape(1, dec_batch, dec_seq, n_fox),
            s_s.reshape(1, dec_batch, n_ret, HEAD_DIM, HEAD_DIM),
            cv_s.reshape(1, dec_batch, CONV_W - 1, d_ff))
```

```python
import functools
import math

import numpy as np
import jax
import jax.numpy as jnp
from jax import lax
from jax.experimental import pallas as pl
from jax.experimental.pallas import tpu as pltpu

F32 = jnp.float32
BF16 = jnp.bfloat16

HEAD_DIM = 128
CHUNK = 64
CONV_W = 3
ROPE_BASE = 10000.0
LN_EPS = 1e-5
GN_EPS = 1e-5
LANES = 128
V7X_VMEM_BYTES = 64 * 1024 * 1024
VMEM_LIMIT = V7X_VMEM_BYTES - 2 * 1024 * 1024
SCALE = HEAD_DIM ** -0.5
LOG2E = math.log2(math.e)
NEG_INF = float("-inf")
HEAD_ROWS = 16


def _cparams(sem):
    return pltpu.CompilerParams(dimension_semantics=sem, vmem_limit_bytes=VMEM_LIMIT)


def _resident(shape):
    nd = len(shape)
    return pl.BlockSpec(shape, lambda *_: (0,) * nd, pipeline_mode=pl.Buffered(1))


def _dot(a, b):
    return jnp.dot(a, b, preferred_element_type=F32)


def _dot_nt(a, b):
    return lax.dot_general(a, b, (((1,), (1,)), ((), ())), preferred_element_type=F32)


def _dot_tn(a, b):
    return lax.dot_general(a, b, (((0,), (0,)), ((), ())), preferred_element_type=F32)


def _split3_dot(a, tri):
    hi = a.astype(BF16)
    r1 = a - hi.astype(F32)
    mid = r1.astype(BF16)
    lo = (r1 - mid.astype(F32)).astype(BF16)
    return _dot(hi, tri) + _dot(mid, tri) + _dot(lo, tri)


def _layernorm(z, g, b):
    mu = jnp.mean(z, axis=-1, keepdims=True)
    zc = z - mu
    var = jnp.mean(zc * zc, axis=-1, keepdims=True)
    return zc * lax.rsqrt(var + LN_EPS) * g + b


def _memkv_kernel(m_ref, w_ref, o_ref):
    kv = _dot(m_ref[...].astype(BF16), w_ref[...])
    for j in range(o_ref.shape[0]):
        o_ref[j] = kv[:, j * HEAD_DIM:(j + 1) * HEAD_DIM]


def _mem_kv(mem, w_bf):
    m, _ = mem.shape
    return pl.pallas_call(
        _memkv_kernel,
        out_shape=jax.ShapeDtypeStruct((w_bf.shape[1] // HEAD_DIM, m, HEAD_DIM), F32),
        compiler_params=pltpu.CompilerParams(vmem_limit_bytes=VMEM_LIMIT),
        name="mem_kv",
    )(mem, w_bf)


def _w_prep_kernel(w_hbm, wa_ref, wf_ref, buf, sem, *, o_ff, n_fox, tn):
    j = pl.program_id(0)
    n = pl.num_programs(0)

    def copy(blk, slot):
        r0 = jnp.where(blk * tn < o_ff, blk * tn, blk * tn + n_fox)
        return pltpu.make_async_copy(w_hbm.at[pl.ds(r0, tn)], buf.at[slot], sem.at[slot])

    @pl.when(j == 0)
    def _():
        gate = pltpu.make_async_copy(w_hbm.at[pl.ds(o_ff, tn)], buf.at[1], sem.at[1])
        gate.start()
        copy(0, 0).start()
        gate.wait()
        rows = buf[1, 0:LANES, 0, :]
        keep = lax.broadcasted_iota(jnp.int32, rows.shape, 0) < n_fox
        wf_ref[...] = jnp.where(keep, rows, 0.0).T.astype(BF16)

    @pl.when(j + 1 < n)
    def _():
        copy(j + 1, (j + 1) % 2).start()

    copy(j, j % 2).wait()
    wa_ref[...] = buf[j % 2, :, 0, :].T.astype(BF16)


def _w_prep(w_fm, *, o_ff, n_fox, tn):
    n, _, d_model = w_fm.shape
    assert o_ff % tn == 0 and (n - n_fox) % tn == 0 and tn >= LANES and o_ff + tn <= n
    return pl.pallas_call(
        functools.partial(_w_prep_kernel, o_ff=o_ff, n_fox=n_fox, tn=tn),
        grid=((n - n_fox) // tn,),
        in_specs=[pl.BlockSpec(memory_space=pl.ANY)],
        out_specs=(pl.BlockSpec((d_model, tn), lambda j: (0, j)),
                   pl.BlockSpec((d_model, LANES), lambda j: (0, 0))),
        out_shape=(jax.ShapeDtypeStruct((d_model, n - n_fox), BF16),
                   jax.ShapeDtypeStruct((d_model, LANES), BF16)),
        scratch_shapes=[pltpu.VMEM((2, tn, 1, d_model), F32), pltpu.SemaphoreType.DMA((2,))],
        compiler_params=_cparams(("arbitrary",)),
        name="w_prep",
    )(w_fm)


def _in_proj_kernel(x_ref, wa_ref, wf_ref, bf_ref, base_ref, rot_ref, mk_ref, mv_ref, tri_ref,
                    fq_ref, fk_ref, fv_ref, lf_ref, c_ref, rq_ref, rk_ref, rv_ref, rg_ref, mo_ref,
                    qa_ref, ka_ref, va_ref, carry_ref, start_ref, *, d_fox, d_ret, d_memq, tiles_per_seq,
                    blk_tiles):
    i = pl.program_id(0)
    xb = x_ref[...].astype(BF16)

    def proj(lo, width):
        return _dot(xb, wa_ref[:, lo:lo + width])

    o_fk = d_fox
    o_fv = 2 * d_fox
    o_rq = 3 * d_fox
    o_rk = o_rq + d_ret
    o_rv = o_rk + d_ret
    o_rg = o_rv + d_ret
    o_mq = o_rg + d_ret

    heads = lambda d: [slice(h * HEAD_DIM, (h + 1) * HEAD_DIM) for h in range(d // HEAD_DIM)]

    z = _dot(xb, wf_ref[...]) + bf_ref[...]
    fq_ref[...] = (proj(0, d_fox) * (SCALE * LOG2E)).astype(BF16)
    lf = jnp.minimum(z, 0.0) - jnp.log1p(jnp.exp(-jnp.abs(z)))
    lf_ref[...] = lf

    @pl.when(i % tiles_per_seq == 0)
    def _():
        carry_ref[...] = jnp.zeros_like(carry_ref)

    parts = _dot(tri_ref[...], jnp.concatenate([t.astype(BF16) for t in _split3_f32(lf)], axis=1))
    c = parts[:, 0:LANES] + parts[:, LANES:2 * LANES] + parts[:, 2 * LANES:] + carry_ref[0:1, :]
    c_ref[...] = c
    tm = c.shape[0]
    carry_ref[...] = jnp.broadcast_to(c[tm - 1:tm, :], carry_ref.shape)

    @pl.when(i % blk_tiles == 0)
    def _():
        start_ref[...] = jnp.broadcast_to(c[0:1, :], start_ref.shape)

    rel3 = _split3_f32((c - start_ref[0:1, :]) * LOG2E)
    zfk = proj(o_fk, d_fox)
    zfv = proj(o_fv, d_fox)
    for h, sl in enumerate(heads(d_fox)):
        fk_ref[0, h] = zfk[:, sl]
        fv_ref[0, h] = zfv[:, sl]
        qa_ref[0, h], ka_ref[0, h], va_ref[0, h] = _fox_operands(fq_ref[:, sl], zfk[:, sl], zfv[:, sl], rel3, h)

    ca = base_ref[0, 0:1, :]
    sa = base_ref[0, 1:2, :]
    cos = ca * rot_ref[0] - sa * rot_ref[1]
    sin = sa * rot_ref[2] + ca * rot_ref[3]

    def rope(t):
        return t * cos + pltpu.roll(t, HEAD_DIM // 2, axis=1) * sin

    def ret_q():
        zq = proj(o_rq, d_ret)
        for sl in heads(d_ret):
            rq_ref[:, sl] = rope(zq[:, sl]).astype(BF16)

    def ret_k():
        zk = proj(o_rk, d_ret)
        for sl in heads(d_ret):
            rk_ref[:, sl] = (rope(zk[:, sl]) * SCALE).astype(BF16)

    def ret_v():
        rv_ref[...] = proj(o_rv, d_ret).astype(BF16)

    def ret_g():
        rg_ref[...] = proj(o_rg, d_ret).astype(BF16)

    zm = proj(o_mq, d_memq)
    pending = [ret_q, ret_k, ret_v, ret_g]
    for h, sl in enumerate(heads(d_memq)):
        s = _dot_nt((zm[:, sl] * SCALE).astype(BF16), mk_ref[0, h].astype(BF16))
        p = jnp.exp(s - jnp.max(s, axis=-1, keepdims=True))
        if pending:
            pending.pop(0)()
        o = _dot(p.astype(BF16), mv_ref[0, h].astype(BF16))
        mo_ref[:, sl] = (o / jnp.sum(p, axis=-1, keepdims=True)).astype(BF16)
    for projection in pending:
        projection()


def _in_proj(x, wa, wf, bfp, base_tab, rot_tab, mk, mv, *, tm, seq_len, d_fox, d_ret, d_memq, attn_block):
    m, d_model = x.shape
    assert m % tm == 0 and seq_len % tm == 0
    assert base_tab.shape == (m // tm, 8, HEAD_DIM) and rot_tab.shape == (4, tm, HEAD_DIM)
    tiles_per_seq = seq_len // tm
    n_fox = d_fox // HEAD_DIM
    tri = jnp.asarray(np.arange(tm)[:, None] >= np.arange(tm)[None, :], dtype=BF16)
    row = lambda w: pl.BlockSpec((tm, w), lambda i: (i, 0))
    memspec = pl.BlockSpec((1,) + mk.shape[1:], lambda i: (i // tiles_per_seq, 0, 0, 0))
    kvspec = pl.BlockSpec((1, n_fox, tm, HEAD_DIM), lambda i: (i // tiles_per_seq, 0, i % tiles_per_seq, 0))
    augspec = pl.BlockSpec((1, n_fox, tm, 2 * HEAD_DIM),
                           lambda i: (i // tiles_per_seq, 0, i % tiles_per_seq, 0))
    aug = jax.ShapeDtypeStruct((m // seq_len, n_fox, seq_len, 2 * HEAD_DIM), BF16)
    assert attn_block % tm == 0 and seq_len % attn_block == 0
    kern = functools.partial(_in_proj_kernel, d_fox=d_fox, d_ret=d_ret, d_memq=d_memq,
                             tiles_per_seq=tiles_per_seq, blk_tiles=attn_block // tm)
    out_shape = (
        jax.ShapeDtypeStruct((m, d_fox), BF16),
        jax.ShapeDtypeStruct((m // seq_len, n_fox, seq_len, HEAD_DIM), F32),
        jax.ShapeDtypeStruct((m // seq_len, n_fox, seq_len, HEAD_DIM), F32),
        jax.ShapeDtypeStruct((m, LANES), F32),
        jax.ShapeDtypeStruct((m, LANES), F32),
        jax.ShapeDtypeStruct((m, d_ret), BF16),
        jax.ShapeDtypeStruct((m, d_ret), BF16),
        jax.ShapeDtypeStruct((m, d_ret), BF16),
        jax.ShapeDtypeStruct((m, d_ret), BF16),
        jax.ShapeDtypeStruct((m, d_memq), BF16),
        aug, aug, aug,
    )
    return pl.pallas_call(
        kern,
        grid=(m // tm,),
        in_specs=[row(d_model), _resident(wa.shape), _resident(wf.shape), _resident(bfp.shape),
                  pl.BlockSpec((1, 8, HEAD_DIM), lambda i: (i, 0, 0)), _resident(rot_tab.shape),
                  memspec, memspec, _resident(tri.shape)],
        out_specs=(row(d_fox), kvspec, kvspec, row(LANES), row(LANES),
                   row(d_ret), row(d_ret), row(d_ret), row(d_ret), row(d_memq),
                   augspec, augspec, augspec),
        out_shape=out_shape,
        scratch_shapes=[pltpu.VMEM((8, LANES), F32), pltpu.VMEM((8, LANES), F32)],
        compiler_params=_cparams(("arbitrary",)),
        name="in_proj",
    )(x, wa, wf, bfp, base_tab, rot_tab, mk, mv, tri)


def _split3_f32(a):
    hi = a.astype(BF16).astype(F32)
    r1 = a - hi
    mid = r1.astype(BF16).astype(F32)
    lo = (r1 - mid).astype(BF16).astype(F32)
    return hi, mid, lo


def _fox_operands(q, k, v, rel3, h):
    lane = lax.broadcasted_iota(jnp.int32, q.shape, 1)
    hi, mid, lo = (t[:, h:h + 1] for t in rel3)
    qbias = jnp.where(lane == 0, hi, jnp.where(lane == 1, mid, jnp.where(lane == 2, lo,
                      jnp.where(lane < 6, 1.0, 0.0))))
    kbias = jnp.where(lane < 3, 1.0, jnp.where(lane == 3, -hi, jnp.where(lane == 4, -mid,
                      jnp.where(lane == 5, -lo, 0.0))))
    ones_col = jnp.where(lane == 0, 1.0, 0.0)
    cat = lambda a, b: jnp.concatenate([a.astype(BF16), b.astype(BF16)], axis=1)
    return cat(q, qbias), cat(k, kbias), cat(v, ones_col)


def _fox_prompt_kernel(qa, k_ref, v_ref, ccol_ref, o_ref, kbuf, vbuf, cstart, acc, sbuf, tbuf, *, tb, nsub,
                       kv_per_trip):
    h = pl.program_id(0)
    qi = pl.program_id(1)
    tq = tb * nsub
    row0 = pl.multiple_of(qi * tq, tq)
    kbuf[pl.ds(row0, tq), :] = k_ref[...]
    vbuf[pl.ds(row0, tq), :] = v_ref[...]
    lane = lax.broadcasted_iota(jnp.int32, (8, LANES), 1)
    for s in range(nsub):
        start = jnp.sum(jnp.where(lane == h, ccol_ref[s * tb:s * tb + 8, :], 0.0), axis=1, keepdims=True)
        cstart[nsub * qi + s] = jnp.broadcast_to(start[0:1], (8, LANES))
    acc[...] = jnp.zeros_like(acc)

    r_id = lax.broadcasted_iota(jnp.int32, (tb, LANES), 0)
    c_id = lax.broadcasted_iota(jnp.int32, (tb, LANES), 1)
    n_ct = tb // LANES
    subs = range(nsub)

    def lane_tiles(s2):
        return [s2[:, t * LANES:(t + 1) * LANES] for t in range(n_ct)]

    def scores(s, kj):
        col0 = pl.multiple_of(kj * tb, tb)
        s2 = _dot_nt(qa[s * tb:(s + 1) * tb, :], kbuf[pl.ds(col0, tb), :])
        return s2, functools.reduce(jnp.maximum, lane_tiles(s2))

    def softmax_pv(s, kj, sc, m, masked):
        col0 = pl.multiple_of(kj * tb, tb)
        s2, tmax = sc
        tiles = lane_tiles(s2)
        if masked:
            tiles = [jnp.where(c_id + t * LANES <= r_id, tiles[t], NEG_INF) for t in range(n_ct)]
            tmax = functools.reduce(jnp.maximum, tiles)
        delta = (cstart[nsub * qi + s][0:1, :] - cstart[kj][0:1, :]) * LOG2E
        m_rel = m - delta
        m_new = jnp.maximum(m_rel, jnp.max(tmax, axis=-1, keepdims=True))
        alpha = jnp.exp2(m_rel - m_new)
        p = jnp.concatenate([jnp.exp2(t - m_new).astype(BF16) for t in tiles], axis=1)
        pv = _dot(p, vbuf[pl.ds(col0, tb), :])
        a = acc[s]
        acc[s] = jnp.concatenate([alpha * a[:, 0:LANES], alpha * a[:, LANES:]], axis=1) + pv
        return m_new + delta

    for s in subs:
        sbuf[s], tbuf[s] = scores(s, 0)

    def body(t, ms):
        kj = kv_per_trip * t
        cur = [(sbuf[s], tbuf[s]) for s in subs]
        for u in range(kv_per_trip):
            nxt = [scores(s, kj + u + 1) for s in subs]
            ms = [softmax_pv(s, kj + u, cur[s], ms[s], False) for s in subs]
            cur = nxt
        for s in subs:
            sbuf[s], tbuf[s] = cur[s]
        return tuple(ms)

    assert nsub % kv_per_trip == 0
    ms = lax.fori_loop(0, (nsub // kv_per_trip) * qi, body,
                       tuple(jnp.full((tb, LANES), NEG_INF, F32) for _ in subs))
    half = tb // 2
    r_half = lax.broadcasted_iota(jnp.int32, (half, LANES), 0)
    c_half = lax.broadcasted_iota(jnp.int32, (half, LANES), 1)

    def diagonal_unit(s, kj, m, s2_full):
        col0 = pl.multiple_of(kj * tb, tb)
        delta = (cstart[nsub * qi + s][0:1, :] - cstart[kj][0:1, :]) * LOG2E
        for ra in range(2):
            r0 = ra * half
            m_r = m[r0:r0 + half] - delta
            a = acc[s, r0:r0 + half, :]
            for ca in range(ra + 1):
                c0 = pl.multiple_of(col0 + ca * half, half)
                if s2_full is None:
                    s2 = _dot_nt(qa[s * tb + r0:s * tb + r0 + half, :], kbuf[pl.ds(c0, half), :])
                else:
                    s2 = s2_full[r0:r0 + half, ca * half:(ca + 1) * half]
                tiles = [s2[:, t * LANES:(t + 1) * LANES] for t in range(half // LANES)]
                if ca == ra:
                    tiles = [jnp.where(c_half + t * LANES <= r_half, tl, NEG_INF) for t, tl in enumerate(tiles)]
                m_new = jnp.maximum(m_r, jnp.max(functools.reduce(jnp.maximum, tiles), axis=-1, keepdims=True))
                alpha = jnp.exp2(m_r - m_new)
                p = jnp.concatenate([jnp.exp2(tl - m_new).astype(BF16) for tl in tiles], axis=1)
                pv = _dot(p, vbuf[pl.ds(c0, half), :])
                a = jnp.concatenate([alpha * a[:, 0:LANES], alpha * a[:, LANES:]], axis=1) + pv
                m_r = m_new
            acc[s, r0:r0 + half, :] = a

    for s in subs:
        m = ms[s]
        for t in range(s):
            sc = (sbuf[s], tbuf[s]) if t == 0 else scores(s, nsub * qi + t)
            m = softmax_pv(s, nsub * qi + t, sc, m, False)
        diagonal_unit(s, nsub * qi + s, m, sbuf[s] if s == 0 else None)
        a = acc[s]
        o_ref[s * tb:(s + 1) * tb, :] = (a[:, 0:HEAD_DIM] / a[:, HEAD_DIM:HEAD_DIM + 1]).astype(BF16)


def _fox_prompt(qa, ka, va, c_col, *, tb, nsub):
    _, n_heads, s, width = qa.shape
    tq = tb * nsub
    assert s % tq == 0
    op = lambda: pl.BlockSpec((None, None, tq, width), lambda h, qi: (0, h, qi, 0))
    return pl.pallas_call(
        functools.partial(_fox_prompt_kernel, tb=tb, nsub=nsub, kv_per_trip=min(nsub, 4)),
        grid=(n_heads, s // tq),
        in_specs=[op(), op(), op(), pl.BlockSpec((tq, LANES), lambda h, qi: (qi, 0))],
        out_specs=pl.BlockSpec((tq, HEAD_DIM), lambda h, qi: (qi, h)),
        out_shape=jax.ShapeDtypeStruct((s, n_heads * HEAD_DIM), BF16),
        scratch_shapes=[pltpu.VMEM((s, width), BF16),
                        pltpu.VMEM((s, width), BF16),
                        pltpu.VMEM((s // tb, 8, LANES), F32),
                        pltpu.VMEM((nsub, tb, width), F32),
                        pltpu.VMEM((nsub, tb, tb), F32),
                        pltpu.VMEM((nsub, tb, LANES), F32)],
        compiler_params=_cparams(("arbitrary", "arbitrary")),
        name="fox_prompt",
    )(qa, ka, va, c_col)


def _fox_sample_kernel(q_ref, kc_ref, vc_ref, kn_ref, vn_ref, lfc_ref, lfn_ref, ccol_ref,
                       tric_ref, trin_ref, o_ref, *, n_heads):
    c_cache = _split3_dot(lfc_ref[0], tric_ref[...])
    p_len = c_cache.shape[1]
    ck_cache = c_cache - c_cache[:, p_len - 1:p_len]
    ck_new = _split3_dot(lfn_ref[0], trin_ref[...])
    ccol = ccol_ref[...]
    n_new = ck_new.shape[1]
    r = lax.broadcasted_iota(jnp.int32, (n_new, n_new), 0)
    c = lax.broadcasted_iota(jnp.int32, (n_new, n_new), 1)
    for h in range(n_heads):
        sl = slice(h * HEAD_DIM, (h + 1) * HEAD_DIM)
        q = q_ref[:, sl]
        cq = ccol[:, h:h + 1]
        s_c = _dot_nt(q, kc_ref[0, h].astype(BF16)) + (cq - ck_cache[h:h + 1, :]) * LOG2E
        s_n = _dot_nt(q, kn_ref[0, h].astype(BF16)) + (cq - ck_new[h:h + 1, :]) * LOG2E
        s_n = jnp.where(c <= r, s_n, NEG_INF)
        m = jnp.maximum(jnp.max(s_c, axis=-1, keepdims=True), jnp.max(s_n, axis=-1, keepdims=True))
        p_c = jnp.exp2(s_c - m)
        p_n = jnp.exp2(s_n - m)
        l = jnp.sum(p_c, axis=-1, keepdims=True) + jnp.sum(p_n, axis=-1, keepdims=True)
        o = _dot(p_c.astype(BF16), vc_ref[0, h].astype(BF16)) + \
            _dot(p_n.astype(BF16), vn_ref[0, h].astype(BF16))
        o_ref[:, sl] = (o / l).astype(BF16)


def _fox_sample(fq, kc, vc, kn, vn, lfc_row, lfn_row, c_col, *, n_heads, seq_len):
    m, d_fox = fq.shape
    nb, _, p_len, _ = kc.shape
    tric = jnp.asarray(np.arange(p_len)[:, None] <= np.arange(p_len)[None, :], dtype=BF16)
    trin = jnp.asarray(np.arange(seq_len)[:, None] <= np.arange(seq_len)[None, :], dtype=BF16)
    row = lambda w: pl.BlockSpec((seq_len, w), lambda b: (b, 0))
    cache = pl.BlockSpec((1, n_heads, p_len, HEAD_DIM), lambda b: (b, 0, 0, 0))
    new = pl.BlockSpec((1, n_heads, seq_len, HEAD_DIM), lambda b: (b, 0, 0, 0))
    return pl.pallas_call(
        functools.partial(_fox_sample_kernel, n_heads=n_heads),
        grid=(nb,),
        in_specs=[row(d_fox), cache, cache, new, new,
                  pl.BlockSpec((1, HEAD_ROWS, p_len), lambda b: (b, 0, 0)),
                  pl.BlockSpec((1, HEAD_ROWS, seq_len), lambda b: (b, 0, 0)),
                  row(LANES), _resident(tric.shape), _resident(trin.shape)],
        out_specs=row(d_fox),
        out_shape=jax.ShapeDtypeStruct((m, d_fox), BF16),
        compiler_params=_cparams(("arbitrary",)),
        name="fox_sample",
    )(fq, kc, vc, kn, vn, lfc_row, lfn_row, c_col, tric, trin)


def _log_gamma(h):
    return float(np.log1p(-np.exp2(np.float32(-5.0 - h)), dtype=np.float32))


def _retention_kernel(q_ref, k_ref, v_ref, g_ref, s0_ref, gn_ref, o_ref, s_ref, decay_ref, *, n_heads,
                      t_blk):
    step = pl.program_id(1)

    @pl.when(step == 0)
    def _():
        s_ref[...] = s0_ref[...]

    @pl.when(jnp.logical_and(pl.program_id(0) == 0, step == 0))
    def _():
        ti = lax.broadcasted_iota(jnp.int32, (t_blk, t_blk), 0)
        si = lax.broadcasted_iota(jnp.int32, (t_blk, t_blk), 1)
        dist = jnp.abs(ti - si).astype(F32)
        shift = CHUNK.bit_length() - 1
        visible = jnp.right_shift(si, shift) <= jnp.right_shift(ti, shift)
        for h in range(n_heads):
            decay_ref[h] = jnp.where(visible, jnp.exp(_log_gamma(h) * dist), 0.0)

    tcol = lax.broadcasted_iota(jnp.int32, (t_blk, 1), 0).astype(F32)
    for h in range(n_heads):
        lg = _log_gamma(h)
        sl = slice(h * HEAD_DIM, (h + 1) * HEAD_DIM)
        q = q_ref[:, sl]
        k = k_ref[:, sl]
        v = v_ref[:, sl]
        a = _dot_nt(q, k) * decay_ref[h]
        state = s_ref[0, h]
        o = _dot(a.astype(BF16), v) + _dot(q, state.astype(BF16)) * jnp.exp(lg * (tcol + 1.0))
        kd = (k.astype(F32) * jnp.exp(lg * (t_blk - 1.0 - tcol))).astype(BF16)
        s_ref[0, h] = math.exp(lg * t_blk) * state + _dot_tn(kd, v)
        mu = jnp.mean(o, axis=-1, keepdims=True)
        oc = o - mu
        var = jnp.mean(oc * oc, axis=-1, keepdims=True)
        rn = oc * lax.rsqrt(var + GN_EPS) * gn_ref[:, sl]
        gate = g_ref[:, sl].astype(F32)
        o_ref[:, sl] = (gate / (1.0 + jnp.exp(-gate)) * rn).astype(BF16)


def _retention(rq, rk, rv, rg, s0, gn_g, *, t_blk, seq_len, n_heads):
    m, d_ret = rq.shape
    nb = m // seq_len
    steps = seq_len // t_blk
    row = lambda: pl.BlockSpec((t_blk, d_ret), lambda b, s: (b * steps + s, 0))
    st = lambda: pl.BlockSpec((1, n_heads, HEAD_DIM, HEAD_DIM), lambda b, s: (b, 0, 0, 0))
    return pl.pallas_call(
        functools.partial(_retention_kernel, n_heads=n_heads, t_blk=t_blk),
        grid=(nb, steps),
        in_specs=[row(), row(), row(), row(), st(), pl.BlockSpec((1, d_ret), lambda b, s: (0, 0))],
        out_specs=(row(), st()),
        out_shape=(jax.ShapeDtypeStruct((m, d_ret), BF16),
                   jax.ShapeDtypeStruct((nb, n_heads, HEAD_DIM, HEAD_DIM), F32)),
        scratch_shapes=[pltpu.VMEM((n_heads, t_blk, t_blk), F32)],
        compiler_params=_cparams(("arbitrary", "arbitrary")),
        name="retention",
    )(rq, rk, rv, rg, s0, gn_g)


def _mix_out_kernel(x_ref, fo_ref, ro_ref, mo_ref, wo_ref, g_ref, b_ref, o_ref, *rest, alpha, emit_weight):
    wo_out, cat_ref = rest if emit_weight else (None,) + rest
    d_fox = fo_ref.shape[1]
    d_ret = ro_ref.shape[1]
    cat_ref[:, 0:d_fox] = fo_ref[...]
    cat_ref[:, d_fox:d_fox + d_ret] = ro_ref[...]
    cat_ref[:, d_fox + d_ret:] = mo_ref[...]
    wo = wo_ref[...]
    if emit_weight:
        wo = wo.astype(BF16)
        wo_out[...] = wo
    tm = x_ref.shape[0]
    pieces = [slice(0, tm // 2), slice(tm // 2, tm)] if tm % 32 == 0 else [slice(0, tm)]
    mix = [_dot(cat_ref[r, :], wo) for r in pieces]
    for r, mx in zip(pieces, mix):
        o_ref[r, :] = _layernorm(alpha * x_ref[r, :] + mx, g_ref[...], b_ref[...])


def _mix_out(x, fo, ro, mo, wo, g, b, *, tm, alpha):
    m, d_model = x.shape
    emit = wo.dtype != BF16
    assert not emit or m == tm
    row = lambda w: pl.BlockSpec((tm, w), lambda i: (i, 0))
    out_specs, out_shape = row(d_model), jax.ShapeDtypeStruct((m, d_model), F32)
    if emit:
        out_specs = (out_specs, pl.BlockSpec(wo.shape, lambda i: (0, 0)))
        out_shape = (out_shape, jax.ShapeDtypeStruct(wo.shape, BF16))
    return pl.pallas_call(
        functools.partial(_mix_out_kernel, alpha=alpha, emit_weight=emit),
        grid=(m // tm,),
        in_specs=[row(d_model), row(fo.shape[1]), row(ro.shape[1]), row(mo.shape[1]),
                  _resident(wo.shape), _resident(g.shape), _resident(b.shape)],
        out_specs=out_specs,
        out_shape=out_shape,
        scratch_shapes=[pltpu.VMEM((tm, wo.shape[0]), BF16)],
        compiler_params=_cparams(("arbitrary",)),
        name="mix_out",
    )(x, fo, ro, mo, wo, g, b)


def _gelu_tanh(x):
    c = math.sqrt(2.0 / math.pi)
    return x * (0.5 * (1.0 + jnp.tanh(c * (x + 0.044715 * (x * x * x)))))


def _conv_ffn_kernel(h_ref, wg_ref, wu_ref, wd_ref, cw_ref, cb_ref, past_ref, g_ref, b_ref,
                     y_ref, cv_ref, *rest, alpha, seq_len, tm, tc, emit_weights):
    wg_out, wu_out, wd_out = rest[:3] if emit_weights else (None, None, None)
    hb_ref, gbuf_ref, carry_ref = rest[3:] if emit_weights else rest

    def weight(ref, out, idx):
        w = ref[idx]
        if out is not None:
            w = w.astype(BF16)
            out[idx] = w
        return w

    i = pl.program_id(0)
    f = pl.program_id(1)
    nf = pl.num_programs(1)
    lt = min(seq_len, tm)
    ns = tm // lt
    halo = 8
    carried = seq_len > tm

    @pl.when(f == 0)
    def _():
        h = h_ref[...]
        hb_ref[...] = h.astype(BF16)
        y_ref[...] = alpha * h

    if carried:
        @pl.when(jnp.logical_and(i == 0, f == 0))
        def _():
            carry_ref[...] = jnp.zeros_like(carry_ref)

        tiles_per_seq = seq_len // tm
        gbuf_ref[0, halo - 2:halo, :] = jnp.where(i % tiles_per_seq == 0, past_ref[i // tiles_per_seq],
                                                  carry_ref[f])
    else:
        for s in range(ns):
            gbuf_ref[s, halo - 2:halo, :] = past_ref[i * ns + s]

    hb = hb_ref[...]
    hid = []
    for c0 in range(0, wg_ref.shape[1], tc):
        cs = slice(c0, c0 + tc)
        gate = _dot(hb, weight(wg_ref, wg_out, (slice(None), cs)))
        up = _dot(hb, weight(wu_ref, wu_out, (slice(None), cs)))
        w0 = cw_ref[0:1, cs]
        w1 = cw_ref[1:2, cs]
        w2 = cw_ref[2:3, cs]
        cb = cb_ref[:, cs]
        parts = []
        for s in range(ns):
            gs = gate[s * lt:(s + 1) * lt]
            gbuf_ref[s, halo:halo + lt, cs] = gs
            g1 = gbuf_ref[s, halo - 1:halo - 1 + lt, cs]
            g2 = gbuf_ref[s, halo - 2:halo - 2 + lt, cs]
            gc = cb + w0 * g2 + w1 * g1 + w2 * gs
            parts.append((_gelu_tanh(gc) * up[s * lt:(s + 1) * lt]).astype(BF16))
        hid.append(parts[0] if ns == 1 else jnp.concatenate(parts, axis=0))
    hid = hid[0] if len(hid) == 1 else jnp.concatenate(hid, axis=1)
    y_ref[...] += _dot(hid, weight(wd_ref, wd_out, (slice(None), slice(None))))

    for s in range(ns):
        last2 = gbuf_ref[s, halo + lt - 2:halo + lt, :]
        cv_ref[s] = last2
        if carried:
            carry_ref[f] = last2

    @pl.when(f == nf - 1)
    def _():
        y_ref[...] = _layernorm(y_ref[...], g_ref[...], b_ref[...])


def _conv_ffn(h1, wg, wu, wd, conv_w, conv_b, past, g, b, *, tm, tf, seq_len, alpha):
    tc = 2 * LANES
    assert tf % tc == 0
    m, d_model = h1.shape
    d_ff = wg.shape[1]
    n_seq = past.shape[0]
    assert m % tm == 0 and d_ff % tf == 0 and (seq_len % tm == 0 or tm % seq_len == 0)
    nf = d_ff // tf
    lt = min(seq_len, tm)
    emit = wg.dtype != BF16
    assert not emit or m == tm
    w_specs = [pl.BlockSpec((d_model, tf), lambda i, f: (0, f)),
               pl.BlockSpec((d_model, tf), lambda i, f: (0, f)),
               pl.BlockSpec((tf, d_model), lambda i, f: (f, 0))]
    w_shapes = [jax.ShapeDtypeStruct(t.shape, BF16) for t in (wg, wu, wd)] if emit else []
    y, cv_all, *w_bf = pl.pallas_call(
        functools.partial(_conv_ffn_kernel, alpha=alpha, seq_len=seq_len, tm=tm, tc=tc, emit_weights=emit),
        grid=(m // tm, nf),
        in_specs=[pl.BlockSpec((tm, d_model), lambda i, f: (i, 0)),
                  *w_specs,
                  pl.BlockSpec((CONV_W, tf), lambda i, f: (0, f)),
                  pl.BlockSpec((1, tf), lambda i, f: (0, f)),
                  pl.BlockSpec((n_seq, CONV_W - 1, tf), lambda i, f: (0, 0, f)),
                  pl.BlockSpec((1, d_model), lambda i, f: (0, 0)),
                  pl.BlockSpec((1, d_model), lambda i, f: (0, 0))],
        out_specs=(pl.BlockSpec((tm, d_model), lambda i, f: (i, 0)),
                   pl.BlockSpec((tm // lt, CONV_W - 1, tf), lambda i, f: (i, 0, f)),
                   *(w_specs if emit else [])),
        out_shape=(jax.ShapeDtypeStruct((m, d_model), F32),
                   jax.ShapeDtypeStruct((m // lt, CONV_W - 1, d_ff), F32), *w_shapes),
        scratch_shapes=[pltpu.VMEM((tm, d_model), BF16),
                        pltpu.VMEM((tm // lt, 8 + lt, tf), F32),
                        pltpu.VMEM((nf, CONV_W - 1, tf), F32)],
        compiler_params=_cparams(("arbitrary", "arbitrary")),
        name="conv_ffn",
    )(h1, wg, wu, wd, conv_w, conv_b, past, g, b)
    return y, cv_all.reshape(n_seq, seq_len // lt, CONV_W - 1, d_ff)[:, -1], w_bf


def _rope_tables(tile_base, tm):
    half = HEAD_DIM // 2
    inv = ROPE_BASE ** (-jnp.arange(half, dtype=F32) / half)
    inv = jnp.concatenate([inv, inv])
    sign = jnp.concatenate([-jnp.ones((half,), F32), jnp.ones((half,), F32)])
    a = tile_base.astype(F32)[:, None] * inv[None, :]
    b = jnp.arange(tm, dtype=F32)[:, None] * inv[None, :]
    base_tab = jnp.stack([jnp.cos(a), jnp.sin(a)], axis=1)
    base_tab = jnp.pad(base_tab, ((0, 0), (0, 6), (0, 0)))
    rot_tab = jnp.stack([jnp.cos(b), jnp.sin(b), sign * jnp.cos(b), sign * jnp.sin(b)])
    return base_tab, rot_tab


def _layer(x, seq_start, seq_len, mk, mv, w, *, alpha, tm_proj, tm_mix, tm_ffn, tf, t_ret, s0, conv_past,
           fox_fn, attn_block):
    tile_base = (seq_start[:, None] + jnp.arange(0, seq_len, tm_proj)[None, :]).reshape(-1)
    base_tab, rot_tab = _rope_tables(tile_base, tm_proj)
    fq, fk, fv, lf, c, rq, rk, rv, rg, mo, qa, ka, va = _in_proj(
        x, w["wa"], w["wf"], w["bf"], base_tab, rot_tab, mk, mv, tm=tm_proj, seq_len=seq_len,
        d_fox=w["d_fox"], d_ret=w["d_ret"], d_memq=w["d_memq"], attn_block=attn_block)
    fo = fox_fn(fq, fk, fv, lf, c, qa, ka, va)
    ro, s_new = _retention(rq, rk, rv, rg, s0, w["gn_g"], t_blk=t_ret, seq_len=seq_len,
                           n_heads=w["d_ret"] // HEAD_DIM)
    w_bf = {}
    h1 = _mix_out(x, fo, ro, mo, w["wo"], w["ln1_g"], w["ln1_b"], tm=tm_mix, alpha=alpha)
    if w["wo"].dtype != BF16:
        h1, w_bf["wo"] = h1
    y, cv, ffn_bf = _conv_ffn(h1, w["wg"], w["wu"], w["wd"], w["conv_w"], w["conv_b"], conv_past,
                              w["ln2_g"], w["ln2_b"], tm=tm_ffn, tf=tf, seq_len=seq_len, alpha=alpha)
    w_bf.update(zip(("wg", "wu", "wd"), ffn_bf))
    return y, fk, fv, lf, s_new, cv, w_bf


def kernel(x_prompt, x_sample, cache_fox_k, cache_fox_v, cache_fox_logf, state_ret, cache_mem_k,
           cache_mem_v, state_conv, mem_prompt, w_in, b_f, w_mem_kv, ret_gn_g, w_o, ln1_g, ln1_b,
           w_gate, w_up, conv_w, conv_b, w_down, ln2_g, ln2_b):
    batch, seq, d_model = x_prompt.shape
    dec_batch, dec_seq, _ = x_sample.shape
    depth = w_in.shape[0]
    past_len = cache_fox_k.shape[2]
    n_fox = cache_fox_k.shape[3]
    n_ret = state_ret.shape[2]
    n_mem = cache_mem_k.shape[3]
    n_mem_tok = cache_mem_k.shape[2]
    d_fox, d_ret, d_memq = n_fox * HEAD_DIM, n_ret * HEAD_DIM, n_mem * HEAD_DIM
    d_ff = w_gate.shape[2]
    assert batch == 1 and depth == 1 and n_fox <= 8
    alpha = (2.0 * depth) ** 0.25
    l = 0

    o_ff = 3 * d_fox
    w_fm = jnp.transpose(w_in, (2, 0, 1))
    wa, wf = _w_prep(w_fm, o_ff=o_ff, n_fox=n_fox, tn=2 * LANES)
    bfp = jnp.pad(b_f[l], (0, LANES - n_fox)).reshape(1, LANES)
    w = dict(wa=wa, wf=wf, bf=bfp, d_fox=d_fox, d_ret=d_ret, d_memq=d_memq,
             gn_g=ret_gn_g[l].reshape(1, d_ret), wo=w_o[l],
             ln1_g=ln1_g[l].reshape(1, d_model), ln1_b=ln1_b[l].reshape(1, d_model),
             wg=w_gate[l], wu=w_up[l], wd=w_down[l],
             conv_w=conv_w[l], conv_b=conv_b[l].reshape(1, d_ff),
             ln2_g=ln2_g[l].reshape(1, d_model), ln2_b=ln2_b[l].reshape(1, d_model))

    to_head_major = lambda t: jnp.swapaxes(t, -3, -2)

    m_s = dec_batch * dec_seq
    kc = to_head_major(cache_fox_k[l])
    vc = to_head_major(cache_fox_v[l])
    lfc_row = jnp.pad(jnp.swapaxes(cache_fox_logf[l], 1, 2), ((0, 0), (0, HEAD_ROWS - n_fox), (0, 0)))

    def fox_sample_fn(fq, fk, fv, lf, c, qa, ka, va):
        lfn_row = jnp.swapaxes(lf.reshape(dec_batch, dec_seq, LANES)[:, :, :HEAD_ROWS], 1, 2)
        return _fox_sample(fq, kc, vc, fk, fv, lfc_row, lfn_row, c, n_heads=n_fox, seq_len=dec_seq)

    y_s, fk_s, fv_s, lf_s, s_s, cv_s, w_bf = _layer(
        x_sample.reshape(m_s, d_model), jnp.full((dec_batch,), past_len, jnp.int32), dec_seq,
        to_head_major(cache_mem_k[l]), to_head_major(cache_mem_v[l]), w,
        alpha=alpha, tm_proj=dec_seq, tm_mix=m_s, tm_ffn=m_s, tf=2 * LANES, t_ret=dec_seq,
        s0=state_ret[l], conv_past=state_conv[l], fox_fn=fox_sample_fn, attn_block=dec_seq)

    mkv = _mem_kv(mem_prompt.reshape(n_mem_tok, d_model), w_mem_kv[l].astype(BF16))
    mk_p, mv_p = mkv[None, :n_mem], mkv[None, n_mem:]

    fox_tb = 1024

    def fox_prompt_fn(fq, fk, fv, lf, c, qa, ka, va):
        return _fox_prompt(qa, ka, va, c, tb=fox_tb, nsub=2)

    y_p, fk_p, fv_p, lf_p, s_p, cv_p, _ = _layer(
        x_prompt.reshape(seq, d_model), jnp.zeros((batch,), jnp.int32), seq, mk_p, mv_p, {**w, **w_bf},
        alpha=alpha, tm_proj=256, tm_mix=512, tm_ffn=1024, tf=512, t_ret=256,
        s0=jnp.zeros((1, n_ret, HEAD_DIM, HEAD_DIM), F32),
        conv_past=jnp.zeros((1, CONV_W - 1, d_ff), F32), fox_fn=fox_prompt_fn, attn_block=fox_tb)

    return (y_p.reshape(batch, seq, d_model), y_s.reshape(dec_batch, dec_seq, d_model),
            to_head_major(fk_p)[None], to_head_major(fv_p)[None],
            lf_p[:, :n_fox].reshape(1, batch, seq, n_fox),
            s_p.reshape(1, batch, n_ret, HEAD_DIM, HEAD_DIM),
            to_head_major(mk_p)[None], to_head_major(mv_p)[None],
            cv_p.reshape(1, batch, CONV_W - 1, d_ff),
            to_head_major(fk_s)[None], to_head_major(fv_s)[None],
            lf_s[:, :n_fox].reshape(1, dec_batch, dec_seq, n_fox),
            s_s.reshape(1, dec_batch, n_ret, HEAD_DIM, HEAD_DIM),
            cv_s.reshape(1, dec_batch, CONV_W - 1, d_ff))
```

```python
import functools
import math

import numpy as np
import jax
import jax.numpy as jnp
from jax import lax
from jax.experimental import pallas as pl
from jax.experimental.pallas import tpu as pltpu

F32 = jnp.float32
BF16 = jnp.bfloat16

HEAD_DIM = 128
CHUNK = 64
CONV_W = 3
ROPE_BASE = 10000.0
LN_EPS = 1e-5
GN_EPS = 1e-5
LANES = 128
V7X_VMEM_BYTES = 64 * 1024 * 1024
VMEM_LIMIT = V7X_VMEM_BYTES - 2 * 1024 * 1024
SCALE = HEAD_DIM ** -0.5
LOG2E = math.log2(math.e)
NEG_INF = float("-inf")
HEAD_ROWS = 16


def _cparams(sem):
    return pltpu.CompilerParams(dimension_semantics=sem, vmem_limit_bytes=VMEM_LIMIT)


def _resident(shape):
    nd = len(shape)
    return pl.BlockSpec(shape, lambda *_: (0,) * nd, pipeline_mode=pl.Buffered(1))


def _dot(a, b):
    return jnp.dot(a, b, preferred_element_type=F32)


def _dot_nt(a, b):
    return lax.dot_general(a, b, (((1,), (1,)), ((), ())), preferred_element_type=F32)


def _dot_tn(a, b):
    return lax.dot_general(a, b, (((0,), (0,)), ((), ())), preferred_element_type=F32)


def _split3_dot(a, tri):
    hi = a.astype(BF16)
    r1 = a - hi.astype(F32)
    mid = r1.astype(BF16)
    lo = (r1 - mid.astype(F32)).astype(BF16)
    return _dot(hi, tri) + _dot(mid, tri) + _dot(lo, tri)


def _layernorm(z, g, b):
    mu = jnp.mean(z, axis=-1, keepdims=True)
    zc = z - mu
    var = jnp.mean(zc * zc, axis=-1, keepdims=True)
    return zc * lax.rsqrt(var + LN_EPS) * g + b


def _memkv_kernel(m_ref, w_ref, o_ref):
    kv = _dot(m_ref[...].astype(BF16), w_ref[...])
    for j in range(o_ref.shape[0]):
        o_ref[j] = kv[:, j * HEAD_DIM:(j + 1) * HEAD_DIM]


def _mem_kv(mem, w_bf):
    m, _ = mem.shape
    return pl.pallas_call(
        _memkv_kernel,
        out_shape=jax.ShapeDtypeStruct((w_bf.shape[1] // HEAD_DIM, m, HEAD_DIM), F32),
        compiler_params=pltpu.CompilerParams(vmem_limit_bytes=VMEM_LIMIT),
        name="mem_kv",
    )(mem, w_bf)


def _w_prep_kernel(w_hbm, wa_ref, wf_ref, buf, sem, *, o_ff, n_fox, tn):
    j = pl.program_id(0)
    n = pl.num_programs(0)

    def copy(blk, slot):
        r0 = jnp.where(blk * tn < o_ff, blk * tn, blk * tn + n_fox)
        return pltpu.make_async_copy(w_hbm.at[pl.ds(r0, tn)], buf.at[slot], sem.at[slot])

    @pl.when(j == 0)
    def _():
        gate = pltpu.make_async_copy(w_hbm.at[pl.ds(o_ff, tn)], buf.at[1], sem.at[1])
        gate.start()
        copy(0, 0).start()
        gate.wait()
        rows = buf[1, 0:LANES, 0, :]
        keep = lax.broadcasted_iota(jnp.int32, rows.shape, 0) < n_fox
        wf_ref[...] = jnp.where(keep, rows, 0.0).T.astype(BF16)

    @pl.when(j + 1 < n)
    def _():
        copy(j + 1, (j + 1) % 2).start()

    copy(j, j % 2).wait()
    wa_ref[...] = buf[j % 2, :, 0, :].T.astype(BF16)


def _w_prep(w_fm, *, o_ff, n_fox, tn):
    n, _, d_model = w_fm.shape
    assert o_ff % tn == 0 and (n - n_fox) % tn == 0 and tn >= LANES and o_ff + tn <= n
    return pl.pallas_call(
        functools.partial(_w_prep_kernel, o_ff=o_ff, n_fox=n_fox, tn=tn),
        grid=((n - n_fox) // tn,),
        in_specs=[pl.BlockSpec(memory_space=pl.ANY)],
        out_specs=(pl.BlockSpec((d_model, tn), lambda j: (0, j)),
                   pl.BlockSpec((d_model, LANES), lambda j: (0, 0))),
        out_shape=(jax.ShapeDtypeStruct((d_model, n - n_fox), BF16),
                   jax.ShapeDtypeStruct((d_model, LANES), BF16)),
        scratch_shapes=[pltpu.VMEM((2, tn, 1, d_model), F32), pltpu.SemaphoreType.DMA((2,))],
        compiler_params=_cparams(("arbitrary",)),
        name="w_prep",
    )(w_fm)


def _in_proj_kernel(x_ref, wa_ref, wf_ref, bf_ref, base_ref, rot_ref, mk_ref, mv_ref, tri_ref,
                    fq_ref, fk_ref, fv_ref, lf_ref, c_ref, rq_ref, rk_ref, rv_ref, rg_ref, mo_ref,
                    qa_ref, ka_ref, va_ref, carry_ref, start_ref, *, d_fox, d_ret, d_memq, tiles_per_seq,
                    blk_tiles):
    i = pl.program_id(0)
    xb = x_ref[...].astype(BF16)

    def proj(lo, width):
        return _dot(xb, wa_ref[:, lo:lo + width])

    o_fk = d_fox
    o_fv = 2 * d_fox
    o_rq = 3 * d_fox
    o_rk = o_rq + d_ret
    o_rv = o_rk + d_ret
    o_rg = o_rv + d_ret
    o_mq = o_rg + d_ret

    heads = lambda d: [slice(h * HEAD_DIM, (h + 1) * HEAD_DIM) for h in range(d // HEAD_DIM)]

    z = _dot(xb, wf_ref[...]) + bf_ref[...]
    fq_ref[...] = (proj(0, d_fox) * (SCALE * LOG2E)).astype(BF16)
    lf = jnp.minimum(z, 0.0) - jnp.log1p(jnp.exp(-jnp.abs(z)))
    lf_ref[...] = lf

    @pl.when(i % tiles_per_seq == 0)
    def _():
        carry_ref[...] = jnp.zeros_like(carry_ref)

    parts = _dot(tri_ref[...], jnp.concatenate([t.astype(BF16) for t in _split3_f32(lf)], axis=1))
    c = parts[:, 0:LANES] + parts[:, LANES:2 * LANES] + parts[:, 2 * LANES:] + carry_ref[0:1, :]
    c_ref[...] = c
    tm = c.shape[0]
    carry_ref[...] = jnp.broadcast_to(c[tm - 1:tm, :], carry_ref.shape)

    @pl.when(i % blk_tiles == 0)
    def _():
        start_ref[...] = jnp.broadcast_to(c[0:1, :], start_ref.shape)

    rel3 = _split3_f32((c - start_ref[0:1, :]) * LOG2E)
    zfk = proj(o_fk, d_fox)
    zfv = proj(o_fv, d_fox)
    for h, sl in enumerate(heads(d_fox)):
        fk_ref[0, h] = zfk[:, sl]
        fv_ref[0, h] = zfv[:, sl]
        qa_ref[0, h], ka_ref[0, h], va_ref[0, h] = _fox_operands(fq_ref[:, sl], zfk[:, sl], zfv[:, sl], rel3, h)

    ca = base_ref[0, 0:1, :]
    sa = base_ref[0, 1:2, :]
    cos = ca * rot_ref[0] - sa * rot_ref[1]
    sin = sa * rot_ref[2] + ca * rot_ref[3]

    def rope(t):
        return t * cos + pltpu.roll(t, HEAD_DIM // 2, axis=1) * sin

    def ret_q():
        zq = proj(o_rq, d_ret)
        for sl in heads(d_ret):
            rq_ref[:, sl] = rope(zq[:, sl]).astype(BF16)

    def ret_k():
        zk = proj(o_rk, d_ret)
        for sl in heads(d_ret):
            rk_ref[:, sl] = (rope(zk[:, sl]) * SCALE).astype(BF16)

    def ret_v():
        rv_ref[...] = proj(o_rv, d_ret).astype(BF16)

    def ret_g():
        rg_ref[...] = proj(o_rg, d_ret).astype(BF16)

    zm = proj(o_mq, d_memq)
    pending = [ret_q, ret_k, ret_v, ret_g]
    for h, sl in enumerate(heads(d_memq)):
        s = _dot_nt((zm[:, sl] * SCALE).astype(BF16), mk_ref[0, h].astype(BF16))
        p = jnp.exp(s - jnp.max(s, axis=-1, keepdims=True))
        if pending:
            pending.pop(0)()
        o = _dot(p.astype(BF16), mv_ref[0, h].astype(BF16))
        mo_ref[:, sl] = (o / jnp.sum(p, axis=-1, keepdims=True)).astype(BF16)
    for projection in pending:
        projection()


def _in_proj(x, wa, wf, bfp, base_tab, rot_tab, mk, mv, *, tm, seq_len, d_fox, d_ret, d_memq, attn_block):
    m, d_model = x.shape
    assert m % tm == 0 and seq_len % tm == 0
    assert base_tab.shape == (m // tm, 8, HEAD_DIM) and rot_tab.shape == (4, tm, HEAD_DIM)
    tiles_per_seq = seq_len // tm
    n_fox = d_fox // HEAD_DIM
    tri = jnp.asarray(np.arange(tm)[:, None] >= np.arange(tm)[None, :], dtype=BF16)
    row = lambda w: pl.BlockSpec((tm, w), lambda i: (i, 0))
    memspec = pl.BlockSpec((1,) + mk.shape[1:], lambda i: (i // tiles_per_seq, 0, 0, 0))
    kvspec = pl.BlockSpec((1, n_fox, tm, HEAD_DIM), lambda i: (i // tiles_per_seq, 0, i % tiles_per_seq, 0))
    augspec = pl.BlockSpec((1, n_fox, tm, 2 * HEAD_DIM),
                           lambda i: (i // tiles_per_seq, 0, i % tiles_per_seq, 0))
    aug = jax.ShapeDtypeStruct((m // seq_len, n_fox, seq_len, 2 * HEAD_DIM), BF16)
    assert attn_block % tm == 0 and seq_len % attn_block == 0
    kern = functools.partial(_in_proj_kernel, d_fox=d_fox, d_ret=d_ret, d_memq=d_memq,
                             tiles_per_seq=tiles_per_seq, blk_tiles=attn_block // tm)
    out_shape = (
        jax.ShapeDtypeStruct((m, d_fox), BF16),
        jax.ShapeDtypeStruct((m // seq_len, n_fox, seq_len, HEAD_DIM), F32),
        jax.ShapeDtypeStruct((m // seq_len, n_fox, seq_len, HEAD_DIM), F32),
        jax.ShapeDtypeStruct((m, LANES), F32),
        jax.ShapeDtypeStruct((m, LANES), F32),
        jax.ShapeDtypeStruct((m, d_ret), BF16),
        jax.ShapeDtypeStruct((m, d_ret), BF16),
        jax.ShapeDtypeStruct((m, d_ret), BF16),
        jax.ShapeDtypeStruct((m, d_ret), BF16),
        jax.ShapeDtypeStruct((m, d_memq), BF16),
        aug, aug, aug,
    )
    return pl.pallas_call(
        kern,
        grid=(m // tm,),
        in_specs=[row(d_model), _resident(wa.shape), _resident(wf.shape), _resident(bfp.shape),
                  pl.BlockSpec((1, 8, HEAD_DIM), lambda i: (i, 0, 0)), _resident(rot_tab.shape),
                  memspec, memspec, _resident(tri.shape)],
        out_specs=(row(d_fox), kvspec, kvspec, row(LANES), row(LANES),
                   row(d_ret), row(d_ret), row(d_ret), row(d_ret), row(d_memq),
                   augspec, augspec, augspec),
        out_shape=out_shape,
        scratch_shapes=[pltpu.VMEM((8, LANES), F32), pltpu.VMEM((8, LANES), F32)],
        compiler_params=_cparams(("arbitrary",)),
        name="in_proj",
    )(x, wa, wf, bfp, base_tab, rot_tab, mk, mv, tri)


def _split3_f32(a):
    hi = a.astype(BF16).astype(F32)
    r1 = a - hi
    mid = r1.astype(BF16).astype(F32)
    lo = (r1 - mid).astype(BF16).astype(F32)
    return hi, mid, lo


def _fox_operands(q, k, v, rel3, h):
    lane = lax.broadcasted_iota(jnp.int32, q.shape, 1)
    hi, mid, lo = (t[:, h:h + 1] for t in rel3)
    qbias = jnp.where(lane == 0, hi, jnp.where(lane == 1, mid, jnp.where(lane == 2, lo,
                      jnp.where(lane < 6, 1.0, 0.0))))
    kbias = jnp.where(lane < 3, 1.0, jnp.where(lane == 3, -hi, jnp.where(lane == 4, -mid,
                      jnp.where(lane == 5, -lo, 0.0))))
    ones_col = jnp.where(lane == 0, 1.0, 0.0)
    cat = lambda a, b: jnp.concatenate([a.astype(BF16), b.astype(BF16)], axis=1)
    return cat(q, qbias), cat(k, kbias), cat(v, ones_col)


def _fox_prompt_kernel(qa, k_ref, v_ref, ccol_ref, o_ref, kbuf, vbuf, cstart, acc, sbuf, tbuf, *, tb, nsub,
                       kv_per_trip):
    h = pl.program_id(0)
    qi = pl.program_id(1)
    tq = tb * nsub
    row0 = pl.multiple_of(qi * tq, tq)
    kbuf[pl.ds(row0, tq), :] = k_ref[...]
    vbuf[pl.ds(row0, tq), :] = v_ref[...]
    lane = lax.broadcasted_iota(jnp.int32, (8, LANES), 1)
    for s in range(nsub):
        start = jnp.sum(jnp.where(lane == h, ccol_ref[s * tb:s * tb + 8, :], 0.0), axis=1, keepdims=True)
        cstart[nsub * qi + s] = jnp.broadcast_to(start[0:1], (8, LANES))
    acc[...] = jnp.zeros_like(acc)

    n_ct = tb // LANES
    subs = range(nsub)

    def lane_tiles(s2):
        return [s2[:, t * LANES:(t + 1) * LANES] for t in range(n_ct)]

    def scores(s, kj):
        col0 = pl.multiple_of(kj * tb, tb)
        s2 = _dot_nt(qa[s * tb:(s + 1) * tb, :], kbuf[pl.ds(col0, tb), :])
        return s2, functools.reduce(jnp.maximum, lane_tiles(s2))

    def softmax_pv(s, kj, sc, m):
        col0 = pl.multiple_of(kj * tb, tb)
        s2, tmax = sc
        tiles = lane_tiles(s2)
        delta = (cstart[nsub * qi + s][0:1, :] - cstart[kj][0:1, :]) * LOG2E
        m_rel = m - delta
        m_new = jnp.maximum(m_rel, jnp.max(tmax, axis=-1, keepdims=True))
        alpha = jnp.exp2(m_rel - m_new)
        p = jnp.concatenate([jnp.exp2(t - m_new).astype(BF16) for t in tiles], axis=1)
        pv = _dot(p, vbuf[pl.ds(col0, tb), :])
        a = acc[s]
        acc[s] = jnp.concatenate([alpha * a[:, 0:LANES], alpha * a[:, LANES:]], axis=1) + pv
        return m_new + delta

    for s in subs:
        sbuf[s], tbuf[s] = scores(s, 0)

    def body(t, ms):
        kj = kv_per_trip * t
        cur = [(sbuf[s], tbuf[s]) for s in subs]
        for u in range(kv_per_trip):
            nxt = [scores(s, kj + u + 1) for s in subs]
            ms = [softmax_pv(s, kj + u, cur[s], ms[s]) for s in subs]
            cur = nxt
        for s in subs:
            sbuf[s], tbuf[s] = cur[s]
        return tuple(ms)

    assert nsub % kv_per_trip == 0
    ms = lax.fori_loop(0, (nsub // kv_per_trip) * qi, body,
                       tuple(jnp.full((tb, LANES), NEG_INF, F32) for _ in subs))
    half = tb // 2
    r_half = lax.broadcasted_iota(jnp.int32, (half, LANES), 0)
    c_half = lax.broadcasted_iota(jnp.int32, (half, LANES), 1)

    def diagonal_unit(s, kj, m, s2_full):
        col0 = pl.multiple_of(kj * tb, tb)
        delta = (cstart[nsub * qi + s][0:1, :] - cstart[kj][0:1, :]) * LOG2E
        for ra in range(2):
            r0 = ra * half
            m_r = m[r0:r0 + half] - delta
            a = acc[s, r0:r0 + half, :]
            for ca in range(ra + 1):
                c0 = pl.multiple_of(col0 + ca * half, half)
                if s2_full is None:
                    s2 = _dot_nt(qa[s * tb + r0:s * tb + r0 + half, :], kbuf[pl.ds(c0, half), :])
                else:
                    s2 = s2_full[r0:r0 + half, ca * half:(ca + 1) * half]
                tiles = [s2[:, t * LANES:(t + 1) * LANES] for t in range(half // LANES)]
                if ca == ra:
                    tiles = [jnp.where(c_half + t * LANES <= r_half, tl, NEG_INF) for t, tl in enumerate(tiles)]
                m_new = jnp.maximum(m_r, jnp.max(functools.reduce(jnp.maximum, tiles), axis=-1, keepdims=True))
                alpha = jnp.exp2(m_r - m_new)
                p = jnp.concatenate([jnp.exp2(tl - m_new).astype(BF16) for tl in tiles], axis=1)
                pv = _dot(p, vbuf[pl.ds(c0, half), :])
                a = jnp.concatenate([alpha * a[:, 0:LANES], alpha * a[:, LANES:]], axis=1) + pv
                m_r = m_new
            acc[s, r0:r0 + half, :] = a

    for s in subs:
        m = ms[s]
        for t in range(s):
            sc = (sbuf[s], tbuf[s]) if t == 0 else scores(s, nsub * qi + t)
            m = softmax_pv(s, nsub * qi + t, sc, m)
        diagonal_unit(s, nsub * qi + s, m, sbuf[s] if s == 0 else None)
        a = acc[s]
        o_ref[s * tb:(s + 1) * tb, :] = (a[:, 0:HEAD_DIM] / a[:, HEAD_DIM:HEAD_DIM + 1]).astype(BF16)


def _fox_prompt(qa, ka, va, c_col, *, tb, nsub):
    _, n_heads, s, width = qa.shape
    tq = tb * nsub
    assert s % tq == 0
    op = lambda: pl.BlockSpec((None, None, tq, width), lambda h, qi: (0, h, qi, 0))
    return pl.pallas_call(
        functools.partial(_fox_prompt_kernel, tb=tb, nsub=nsub, kv_per_trip=min(nsub, 4)),
        grid=(n_heads, s // tq),
        in_specs=[op(), op(), op(), pl.BlockSpec((tq, LANES), lambda h, qi: (qi, 0))],
        out_specs=pl.BlockSpec((tq, HEAD_DIM), lambda h, qi: (qi, h)),
        out_shape=jax.ShapeDtypeStruct((s, n_heads * HEAD_DIM), BF16),
        scratch_shapes=[pltpu.VMEM((s, width), BF16),
                        pltpu.VMEM((s, width), BF16),
                        pltpu.VMEM((s // tb, 8, LANES), F32),
                        pltpu.VMEM((nsub, tb, width), F32),
                        pltpu.VMEM((nsub, tb, tb), F32),
                        pltpu.VMEM((nsub, tb, LANES), F32)],
        compiler_params=_cparams(("arbitrary", "arbitrary")),
        name="fox_prompt",
    )(qa, ka, va, c_col)


def _fox_sample_kernel(q_ref, kc_ref, vc_ref, kn_ref, vn_ref, lfc_ref, lfn_ref, ccol_ref,
                       tric_ref, trin_ref, o_ref, *, n_heads):
    c_cache = _split3_dot(lfc_ref[0], tric_ref[...])
    p_len = c_cache.shape[1]
    ck_cache = c_cache - c_cache[:, p_len - 1:p_len]
    ck_new = _split3_dot(lfn_ref[0], trin_ref[...])
    ccol = ccol_ref[...]
    n_new = ck_new.shape[1]
    r = lax.broadcasted_iota(jnp.int32, (n_new, n_new), 0)
    c = lax.broadcasted_iota(jnp.int32, (n_new, n_new), 1)
    for h in range(n_heads):
        sl = slice(h * HEAD_DIM, (h + 1) * HEAD_DIM)
        q = q_ref[:, sl]
        cq = ccol[:, h:h + 1]
        s_c = _dot_nt(q, kc_ref[0, h].astype(BF16)) + (cq - ck_cache[h:h + 1, :]) * LOG2E
        s_n = _dot_nt(q, kn_ref[0, h].astype(BF16)) + (cq - ck_new[h:h + 1, :]) * LOG2E
        s_n = jnp.where(c <= r, s_n, NEG_INF)
        m = jnp.maximum(jnp.max(s_c, axis=-1, keepdims=True), jnp.max(s_n, axis=-1, keepdims=True))
        p_c = jnp.exp2(s_c - m)
        p_n = jnp.exp2(s_n - m)
        l = jnp.sum(p_c, axis=-1, keepdims=True) + jnp.sum(p_n, axis=-1, keepdims=True)
        o = _dot(p_c.astype(BF16), vc_ref[0, h].astype(BF16)) + \
            _dot(p_n.astype(BF16), vn_ref[0, h].astype(BF16))
        o_ref[:, sl] = (o / l).astype(BF16)


def _fox_sample(fq, kc, vc, kn, vn, lfc_row, lfn_row, c_col, *, n_heads, seq_len):
    m, d_fox = fq.shape
    nb, _, p_len, _ = kc.shape
    tric = jnp.asarray(np.arange(p_len)[:, None] <= np.arange(p_len)[None, :], dtype=BF16)
    trin = jnp.asarray(np.arange(seq_len)[:, None] <= np.arange(seq_len)[None, :], dtype=BF16)
    row = lambda w: pl.BlockSpec((seq_len, w), lambda b: (b, 0))
    cache = pl.BlockSpec((1, n_heads, p_len, HEAD_DIM), lambda b: (b, 0, 0, 0))
    new = pl.BlockSpec((1, n_heads, seq_len, HEAD_DIM), lambda b: (b, 0, 0, 0))
    return pl.pallas_call(
        functools.partial(_fox_sample_kernel, n_heads=n_heads),
        grid=(nb,),
        in_specs=[row(d_fox), cache, cache, new, new,
                  pl.BlockSpec((1, HEAD_ROWS, p_len), lambda b: (b, 0, 0)),
                  pl.BlockSpec((1, HEAD_ROWS, seq_len), lambda b: (b, 0, 0)),
                  row(LANES), _resident(tric.shape), _resident(trin.shape)],
        out_specs=row(d_fox),
        out_shape=jax.ShapeDtypeStruct((m, d_fox), BF16),
        compiler_params=_cparams(("arbitrary",)),
        name="fox_sample",
    )(fq, kc, vc, kn, vn, lfc_row, lfn_row, c_col, tric, trin)


def _log_gamma(h):
    return float(np.log1p(-np.exp2(np.float32(-5.0 - h)), dtype=np.float32))


def _retention_kernel(q_ref, k_ref, v_ref, g_ref, s0_ref, gn_ref, o_ref, s_ref, decay_ref, *, n_heads,
                      t_blk):
    step = pl.program_id(1)

    @pl.when(step == 0)
    def _():
        s_ref[...] = s0_ref[...]

    @pl.when(jnp.logical_and(pl.program_id(0) == 0, step == 0))
    def _():
        ti = lax.broadcasted_iota(jnp.int32, (t_blk, t_blk), 0)
        si = lax.broadcasted_iota(jnp.int32, (t_blk, t_blk), 1)
        dist = jnp.abs(ti - si).astype(F32)
        shift = CHUNK.bit_length() - 1
        visible = jnp.right_shift(si, shift) <= jnp.right_shift(ti, shift)
        for h in range(n_heads):
            decay_ref[h] = jnp.where(visible, jnp.exp(_log_gamma(h) * dist), 0.0)

    tcol = lax.broadcasted_iota(jnp.int32, (t_blk, 1), 0).astype(F32)
    for h in range(n_heads):
        lg = _log_gamma(h)
        sl = slice(h * HEAD_DIM, (h + 1) * HEAD_DIM)
        q = q_ref[:, sl]
        k = k_ref[:, sl]
        v = v_ref[:, sl]
        a = _dot_nt(q, k) * decay_ref[h]
        state = s_ref[0, h]
        o = _dot(a.astype(BF16), v) + _dot(q, state.astype(BF16)) * jnp.exp(lg * (tcol + 1.0))
        kd = (k.astype(F32) * jnp.exp(lg * (t_blk - 1.0 - tcol))).astype(BF16)
        s_ref[0, h] = math.exp(lg * t_blk) * state + _dot_tn(kd, v)
        mu = jnp.mean(o, axis=-1, keepdims=True)
        oc = o - mu
        var = jnp.mean(oc * oc, axis=-1, keepdims=True)
        rn = oc * lax.rsqrt(var + GN_EPS) * gn_ref[:, sl]
        gate = g_ref[:, sl].astype(F32)
        o_ref[:, sl] = (gate / (1.0 + jnp.exp(-gate)) * rn).astype(BF16)


def _retention(rq, rk, rv, rg, s0, gn_g, *, t_blk, seq_len, n_heads):
    m, d_ret = rq.shape
    nb = m // seq_len
    steps = seq_len // t_blk
    row = lambda: pl.BlockSpec((t_blk, d_ret), lambda b, s: (b * steps + s, 0))
    st = lambda: pl.BlockSpec((1, n_heads, HEAD_DIM, HEAD_DIM), lambda b, s: (b, 0, 0, 0))
    return pl.pallas_call(
        functools.partial(_retention_kernel, n_heads=n_heads, t_blk=t_blk),
        grid=(nb, steps),
        in_specs=[row(), row(), row(), row(), st(), pl.BlockSpec((1, d_ret), lambda b, s: (0, 0))],
        out_specs=(row(), st()),
        out_shape=(jax.ShapeDtypeStruct((m, d_ret), BF16),
                   jax.ShapeDtypeStruct((nb, n_heads, HEAD_DIM, HEAD_DIM), F32)),
        scratch_shapes=[pltpu.VMEM((n_heads, t_blk, t_blk), F32)],
        compiler_params=_cparams(("arbitrary", "arbitrary")),
        name="retention",
    )(rq, rk, rv, rg, s0, gn_g)


def _mix_out_kernel(x_ref, fo_ref, ro_ref, mo_ref, wo_ref, g_ref, b_ref, o_ref, *rest, alpha, emit_weight):
    wo_out, cat_ref = rest if emit_weight else (None,) + rest
    d_fox = fo_ref.shape[1]
    d_ret = ro_ref.shape[1]
    cat_ref[:, 0:d_fox] = fo_ref[...]
    cat_ref[:, d_fox:d_fox + d_ret] = ro_ref[...]
    cat_ref[:, d_fox + d_ret:] = mo_ref[...]
    wo = wo_ref[...]
    if emit_weight:
        wo = wo.astype(BF16)
        wo_out[...] = wo
    tm = x_ref.shape[0]
    pieces = [slice(0, tm // 2), slice(tm // 2, tm)] if tm % 32 == 0 else [slice(0, tm)]
    mix = [_dot(cat_ref[r, :], wo) for r in pieces]
    for r, mx in zip(pieces, mix):
        o_ref[r, :] = _layernorm(alpha * x_ref[r, :] + mx, g_ref[...], b_ref[...])


def _mix_out(x, fo, ro, mo, wo, g, b, *, tm, alpha):
    m, d_model = x.shape
    emit = wo.dtype != BF16
    assert not emit or m == tm
    row = lambda w: pl.BlockSpec((tm, w), lambda i: (i, 0))
    out_specs, out_shape = row(d_model), jax.ShapeDtypeStruct((m, d_model), F32)
    if emit:
        out_specs = (out_specs, pl.BlockSpec(wo.shape, lambda i: (0, 0)))
        out_shape = (out_shape, jax.ShapeDtypeStruct(wo.shape, BF16))
    return pl.pallas_call(
        functools.partial(_mix_out_kernel, alpha=alpha, emit_weight=emit),
        grid=(m // tm,),
        in_specs=[row(d_model), row(fo.shape[1]), row(ro.shape[1]), row(mo.shape[1]),
                  _resident(wo.shape), _resident(g.shape), _resident(b.shape)],
        out_specs=out_specs,
        out_shape=out_shape,
        scratch_shapes=[pltpu.VMEM((tm, wo.shape[0]), BF16)],
        compiler_params=_cparams(("arbitrary",)),
        name="mix_out",
    )(x, fo, ro, mo, wo, g, b)


def _gelu_tanh(x):
    c = math.sqrt(2.0 / math.pi)
    return x * (0.5 * (1.0 + jnp.tanh(c * (x + 0.044715 * (x * x * x)))))


def _conv_ffn_kernel(h_ref, wg_ref, wu_ref, wd_ref, cw_ref, cb_ref, past_ref, g_ref, b_ref,
                     y_ref, cv_ref, *rest, alpha, seq_len, tm, tc, emit_weights):
    wg_out, wu_out, wd_out = rest[:3] if emit_weights else (None, None, None)
    hb_ref, gbuf_ref, carry_ref = rest[3:] if emit_weights else rest

    def weight(ref, out, idx):
        w = ref[idx]
        if out is not None:
            w = w.astype(BF16)
            out[idx] = w
        return w

    i = pl.program_id(0)
    f = pl.program_id(1)
    nf = pl.num_programs(1)
    lt = min(seq_len, tm)
    ns = tm // lt
    halo = 8
    carried = seq_len > tm

    @pl.when(f == 0)
    def _():
        h = h_ref[...]
        hb_ref[...] = h.astype(BF16)
        y_ref[...] = alpha * h

    if carried:
        @pl.when(jnp.logical_and(i == 0, f == 0))
        def _():
            carry_ref[...] = jnp.zeros_like(carry_ref)

        tiles_per_seq = seq_len // tm
        gbuf_ref[0, halo - 2:halo, :] = jnp.where(i % tiles_per_seq == 0, past_ref[i // tiles_per_seq],
                                                  carry_ref[f])
    else:
        for s in range(ns):
            gbuf_ref[s, halo - 2:halo, :] = past_ref[i * ns + s]

    hb = hb_ref[...]
    hid = []
    for c0 in range(0, wg_ref.shape[1], tc):
        cs = slice(c0, c0 + tc)
        gate = _dot(hb, weight(wg_ref, wg_out, (slice(None), cs)))
        up = _dot(hb, weight(wu_ref, wu_out, (slice(None), cs)))
        w0 = cw_ref[0:1, cs]
        w1 = cw_ref[1:2, cs]
        w2 = cw_ref[2:3, cs]
        cb = cb_ref[:, cs]
        parts = []
        for s in range(ns):
            gs = gate[s * lt:(s + 1) * lt]
            gbuf_ref[s, halo:halo + lt, cs] = gs
            g1 = gbuf_ref[s, halo - 1:halo - 1 + lt, cs]
            g2 = gbuf_ref[s, halo - 2:halo - 2 + lt, cs]
            gc = cb + w0 * g2 + w1 * g1 + w2 * gs
            parts.append((_gelu_tanh(gc) * up[s * lt:(s + 1) * lt]).astype(BF16))
        hid.append(parts[0] if ns == 1 else jnp.concatenate(parts, axis=0))
    hid = hid[0] if len(hid) == 1 else jnp.concatenate(hid, axis=1)
    y_ref[...] += _dot(hid, weight(wd_ref, wd_out, (slice(None), slice(None))))

    for s in range(ns):
        last2 = gbuf_ref[s, halo + lt - 2:halo + lt, :]
        cv_ref[s] = last2
        if carried:
            carry_ref[f] = last2

    @pl.when(f == nf - 1)
    def _():
        y_ref[...] = _layernorm(y_ref[...], g_ref[...], b_ref[...])


def _conv_ffn(h1, wg, wu, wd, conv_w, conv_b, past, g, b, *, tm, tf, seq_len, alpha):
    tc = 2 * LANES
    assert tf % tc == 0
    m, d_model = h1.shape
    d_ff = wg.shape[1]
    n_seq = past.shape[0]
    assert m % tm == 0 and d_ff % tf == 0 and (seq_len % tm == 0 or tm % seq_len == 0)
    nf = d_ff // tf
    lt = min(seq_len, tm)
    emit = wg.dtype != BF16
    assert not emit or m == tm
    w_specs = [pl.BlockSpec((d_model, tf), lambda i, f: (0, f)),
               pl.BlockSpec((d_model, tf), lambda i, f: (0, f)),
               pl.BlockSpec((tf, d_model), lambda i, f: (f, 0))]
    w_shapes = [jax.ShapeDtypeStruct(t.shape, BF16) for t in (wg, wu, wd)] if emit else []
    y, cv_all, *w_bf = pl.pallas_call(
        functools.partial(_conv_ffn_kernel, alpha=alpha, seq_len=seq_len, tm=tm, tc=tc, emit_weights=emit),
        grid=(m // tm, nf),
        in_specs=[pl.BlockSpec((tm, d_model), lambda i, f: (i, 0)),
                  *w_specs,
                  pl.BlockSpec((CONV_W, tf), lambda i, f: (0, f)),
                  pl.BlockSpec((1, tf), lambda i, f: (0, f)),
                  pl.BlockSpec((n_seq, CONV_W - 1, tf), lambda i, f: (0, 0, f)),
                  pl.BlockSpec((1, d_model), lambda i, f: (0, 0)),
                  pl.BlockSpec((1, d_model), lambda i, f: (0, 0))],
        out_specs=(pl.BlockSpec((tm, d_model), lambda i, f: (i, 0)),
                   pl.BlockSpec((tm // lt, CONV_W - 1, tf), lambda i, f: (i, 0, f)),
                   *(w_specs if emit else [])),
        out_shape=(jax.ShapeDtypeStruct((m, d_model), F32),
                   jax.ShapeDtypeStruct((m // lt, CONV_W - 1, d_ff), F32), *w_shapes),
        scratch_shapes=[pltpu.VMEM((tm, d_model), BF16),
                        pltpu.VMEM((tm // lt, 8 + lt, tf), F32),
                        pltpu.VMEM((nf, CONV_W - 1, tf), F32)],
        compiler_params=_cparams(("arbitrary", "arbitrary")),
        name="conv_ffn",
    )(h1, wg, wu, wd, conv_w, conv_b, past, g, b)
    return y, cv_all.reshape(n_seq, seq_len // lt, CONV_W - 1, d_ff)[:, -1], w_bf


def _rope_tables(tile_base, tm):
    half = HEAD_DIM // 2
    inv = ROPE_BASE ** (-jnp.arange(half, dtype=F32) / half)
    inv = jnp.concatenate([inv, inv])
    sign = jnp.concatenate([-jnp.ones((half,), F32), jnp.ones((half,), F32)])
    a = tile_base.astype(F32)[:, None] * inv[None, :]
    b = jnp.arange(tm, dtype=F32)[:, None] * inv[None, :]
    base_tab = jnp.stack([jnp.cos(a), jnp.sin(a)], axis=1)
    base_tab = jnp.pad(base_tab, ((0, 0), (0, 6), (0, 0)))
    rot_tab = jnp.stack([jnp.cos(b), jnp.sin(b), sign * jnp.cos(b), sign * jnp.sin(b)])
    return base_tab, rot_tab


def _layer(x, seq_start, seq_len, mk, mv, w, *, alpha, tm_proj, tm_mix, tm_ffn, tf, t_ret, s0, conv_past,
           fox_fn, attn_block):
    tile_base = (seq_start[:, None] + jnp.arange(0, seq_len, tm_proj)[None, :]).reshape(-1)
    base_tab, rot_tab = _rope_tables(tile_base, tm_proj)
    fq, fk, fv, lf, c, rq, rk, rv, rg, mo, qa, ka, va = _in_proj(
        x, w["wa"], w["wf"], w["bf"], base_tab, rot_tab, mk, mv, tm=tm_proj, seq_len=seq_len,
        d_fox=w["d_fox"], d_ret=w["d_ret"], d_memq=w["d_memq"], attn_block=attn_block)
    fo = fox_fn(fq, fk, fv, lf, c, qa, ka, va)
    ro, s_new = _retention(rq, rk, rv, rg, s0, w["gn_g"], t_blk=t_ret, seq_len=seq_len,
                           n_heads=w["d_ret"] // HEAD_DIM)
    w_bf = {}
    h1 = _mix_out(x, fo, ro, mo, w["wo"], w["ln1_g"], w["ln1_b"], tm=tm_mix, alpha=alpha)
    if w["wo"].dtype != BF16:
        h1, w_bf["wo"] = h1
    y, cv, ffn_bf = _conv_ffn(h1, w["wg"], w["wu"], w["wd"], w["conv_w"], w["conv_b"], conv_past,
                              w["ln2_g"], w["ln2_b"], tm=tm_ffn, tf=tf, seq_len=seq_len, alpha=alpha)
    w_bf.update(zip(("wg", "wu", "wd"), ffn_bf))
    return y, fk, fv, lf, s_new, cv, w_bf


def kernel(x_prompt, x_sample, cache_fox_k, cache_fox_v, cache_fox_logf, state_ret, cache_mem_k,
           cache_mem_v, state_conv, mem_prompt, w_in, b_f, w_mem_kv, ret_gn_g, w_o, ln1_g, ln1_b,
           w_gate, w_up, conv_w, conv_b, w_down, ln2_g, ln2_b):
    batch, seq, d_model = x_prompt.shape
    dec_batch, dec_seq, _ = x_sample.shape
    depth = w_in.shape[0]
    past_len = cache_fox_k.shape[2]
    n_fox = cache_fox_k.shape[3]
    n_ret = state_ret.shape[2]
    n_mem = cache_mem_k.shape[3]
    n_mem_tok = cache_mem_k.shape[2]
    d_fox, d_ret, d_memq = n_fox * HEAD_DIM, n_ret * HEAD_DIM, n_mem * HEAD_DIM
    d_ff = w_gate.shape[2]
    assert batch == 1 and depth == 1 and n_fox <= 8
    alpha = (2.0 * depth) ** 0.25
    l = 0

    o_ff = 3 * d_fox
    w_fm = jnp.transpose(w_in, (2, 0, 1))
    wa, wf = _w_prep(w_fm, o_ff=o_ff, n_fox=n_fox, tn=2 * LANES)
    bfp = jnp.pad(b_f[l], (0, LANES - n_fox)).reshape(1, LANES)
    w = dict(wa=wa, wf=wf, bf=bfp, d_fox=d_fox, d_ret=d_ret, d_memq=d_memq,
             gn_g=ret_gn_g[l].reshape(1, d_ret), wo=w_o[l],
             ln1_g=ln1_g[l].reshape(1, d_model), ln1_b=ln1_b[l].reshape(1, d_model),
             wg=w_gate[l], wu=w_up[l], wd=w_down[l],
             conv_w=conv_w[l], conv_b=conv_b[l].reshape(1, d_ff),
             ln2_g=ln2_g[l].reshape(1, d_model), ln2_b=ln2_b[l].reshape(1, d_model))

    to_head_major = lambda t: jnp.swapaxes(t, -3, -2)

    m_s = dec_batch * dec_seq
    kc = to_head_major(cache_fox_k[l])
    vc = to_head_major(cache_fox_v[l])
    lfc_row = jnp.pad(jnp.swapaxes(cache_fox_logf[l], 1, 2), ((0, 0), (0, HEAD_ROWS - n_fox), (0, 0)))

    def fox_sample_fn(fq, fk, fv, lf, c, qa, ka, va):
        lfn_row = jnp.swapaxes(lf.reshape(dec_batch, dec_seq, LANES)[:, :, :HEAD_ROWS], 1, 2)
        return _fox_sample(fq, kc, vc, fk, fv, lfc_row, lfn_row, c, n_heads=n_fox, seq_len=dec_seq)

    y_s, fk_s, fv_s, lf_s, s_s, cv_s, w_bf = _layer(
        x_sample.reshape(m_s, d_model), jnp.full((dec_batch,), past_len, jnp.int32), dec_seq,
        to_head_major(cache_mem_k[l]), to_head_major(cache_mem_v[l]), w,
        alpha=alpha, tm_proj=dec_seq, tm_mix=m_s, tm_ffn=m_s, tf=2 * LANES, t_ret=dec_seq,
        s0=state_ret[l], conv_past=state_conv[l], fox_fn=fox_sample_fn, attn_block=dec_seq)

    mkv = _mem_kv(mem_prompt.reshape(n_mem_tok, d_model), w_mem_kv[l].astype(BF16))
    mk_p, mv_p = mkv[None, :n_mem], mkv[None, n_mem:]

    fox_tb = 1024

    def fox_prompt_fn(fq, fk, fv, lf, c, qa, ka, va):
        return _fox_prompt(qa, ka, va, c, tb=fox_tb, nsub=2)

    y_p, fk_p, fv_p, lf_p, s_p, cv_p, _ = _layer(
        x_prompt.reshape(seq, d_model), jnp.zeros((batch,), jnp.int32), seq, mk_p, mv_p, {**w, **w_bf},
        alpha=alpha, tm_proj=256, tm_mix=512, tm_ffn=1024, tf=512, t_ret=256,
        s0=jnp.zeros((1, n_ret, HEAD_DIM, HEAD_DIM), F32),
        conv_past=jnp.zeros((1, CONV_W - 1, d_ff), F32), fox_fn=fox_prompt_fn, attn_block=fox_tb)

    return (y_p.reshape(batch, seq, d_model), y_s.reshape(dec_batch, dec_seq, d_model),
            to_head_major(fk_p)[None], to_head_major(fv_p)[None],
            lf_p[:, :n_fox].reshape(1, batch, seq, n_fox),
            s_p.reshape(1, batch, n_ret, HEAD_DIM, HEAD_DIM),
            to_head_major(mk_p)[None], to_head_major(mv_p)[None],
            cv_p.reshape(1, batch, CONV_W - 1, d_ff),
            to_head_major(fk_s)[None], to_head_major(fv_s)[None],
            lf_s[:, :n_fox].reshape(1, dec_batch, dec_seq, n_fox),
            s_s.reshape(1, dec_batch, n_ret, HEAD_DIM, HEAD_DIM),
            cv_s.reshape(1, dec_batch, CONV_W - 1, d_ff))
```

```python
import functools
import math

import numpy as np
import jax
import jax.numpy as jnp
from jax import lax
from jax.experimental import pallas as pl
from jax.experimental.pallas import tpu as pltpu

F32 = jnp.float32
BF16 = jnp.bfloat16

HEAD_DIM = 128
CHUNK = 64
CONV_W = 3
ROPE_BASE = 10000.0
LN_EPS = 1e-5
GN_EPS = 1e-5
LANES = 128
V7X_VMEM_BYTES = 64 * 1024 * 1024
VMEM_LIMIT = V7X_VMEM_BYTES - 2 * 1024 * 1024
SCALE = HEAD_DIM ** -0.5
LOG2E = math.log2(math.e)
NEG_INF = float("-inf")
HEAD_ROWS = 16


def _cparams(sem):
    return pltpu.CompilerParams(dimension_semantics=sem, vmem_limit_bytes=VMEM_LIMIT)


def _resident(shape):
    nd = len(shape)
    return pl.BlockSpec(shape, lambda *_: (0,) * nd, pipeline_mode=pl.Buffered(1))


def _dot(a, b):
    return jnp.dot(a, b, preferred_element_type=F32)


def _dot_nt(a, b):
    return lax.dot_general(a, b, (((1,), (1,)), ((), ())), preferred_element_type=F32)


def _dot_tn(a, b):
    return lax.dot_general(a, b, (((0,), (0,)), ((), ())), preferred_element_type=F32)


def _split3_dot(a, tri):
    hi = a.astype(BF16)
    r1 = a - hi.astype(F32)
    mid = r1.astype(BF16)
    lo = (r1 - mid.astype(F32)).astype(BF16)
    return _dot(hi, tri) + _dot(mid, tri) + _dot(lo, tri)


def _layernorm(z, g, b):
    mu = jnp.mean(z, axis=-1, keepdims=True)
    zc = z - mu
    var = jnp.mean(zc * zc, axis=-1, keepdims=True)
    return zc * lax.rsqrt(var + LN_EPS) * g + b


def _memkv_kernel(m_ref, w_ref, o_ref):
    kv = _dot(m_ref[...].astype(BF16), w_ref[...])
    for j in range(o_ref.shape[0]):
        o_ref[j] = kv[:, j * HEAD_DIM:(j + 1) * HEAD_DIM]


def _mem_kv(mem, w_bf):
    m, _ = mem.shape
    return pl.pallas_call(
        _memkv_kernel,
        out_shape=jax.ShapeDtypeStruct((w_bf.shape[1] // HEAD_DIM, m, HEAD_DIM), F32),
        compiler_params=pltpu.CompilerParams(vmem_limit_bytes=VMEM_LIMIT),
        name="mem_kv",
    )(mem, w_bf)


def _w_prep_kernel(w_hbm, wa_ref, wf_ref, buf, sem, *, o_ff, n_fox, tn):
    j = pl.program_id(0)
    n = pl.num_programs(0)

    def copy(blk, slot):
        r0 = jnp.where(blk * tn < o_ff, blk * tn, blk * tn + n_fox)
        return pltpu.make_async_copy(w_hbm.at[pl.ds(r0, tn)], buf.at[slot], sem.at[slot])

    @pl.when(j == 0)
    def _():
        gate = pltpu.make_async_copy(w_hbm.at[pl.ds(o_ff, tn)], buf.at[1], sem.at[1])
        gate.start()
        copy(0, 0).start()
        gate.wait()
        rows = buf[1, 0:LANES, 0, :]
        keep = lax.broadcasted_iota(jnp.int32, rows.shape, 0) < n_fox
        wf_ref[...] = jnp.where(keep, rows, 0.0).T.astype(BF16)

    @pl.when(j + 1 < n)
    def _():
        copy(j + 1, (j + 1) % 2).start()

    copy(j, j % 2).wait()
    wa_ref[...] = buf[j % 2, :, 0, :].T.astype(BF16)


def _w_prep(w_fm, *, o_ff, n_fox, tn):
    n, _, d_model = w_fm.shape
    assert o_ff % tn == 0 and (n - n_fox) % tn == 0 and tn >= LANES and o_ff + tn <= n
    return pl.pallas_call(
        functools.partial(_w_prep_kernel, o_ff=o_ff, n_fox=n_fox, tn=tn),
        grid=((n - n_fox) // tn,),
        in_specs=[pl.BlockSpec(memory_space=pl.ANY)],
        out_specs=(pl.BlockSpec((d_model, tn), lambda j: (0, j)),
                   pl.BlockSpec((d_model, LANES), lambda j: (0, 0))),
        out_shape=(jax.ShapeDtypeStruct((d_model, n - n_fox), BF16),
                   jax.ShapeDtypeStruct((d_model, LANES), BF16)),
        scratch_shapes=[pltpu.VMEM((2, tn, 1, d_model), F32), pltpu.SemaphoreType.DMA((2,))],
        compiler_params=_cparams(("arbitrary",)),
        name="w_prep",
    )(w_fm)


def _in_proj_kernel(x_ref, wa_ref, wf_ref, bf_ref, base_ref, rot_ref, mk_ref, mv_ref, tri_ref,
                    fq_ref, fk_ref, fv_ref, lf_ref, c_ref, rq_ref, rk_ref, rv_ref, rg_ref, mo_ref,
                    qa_ref, ka_ref, va_ref, carry_ref, start_ref, *, d_fox, d_ret, d_memq, tiles_per_seq,
                    blk_tiles):
    i = pl.program_id(0)
    xb = x_ref[...].astype(BF16)

    def proj(lo, width):
        return _dot(xb, wa_ref[:, lo:lo + width])

    o_fk = d_fox
    o_fv = 2 * d_fox
    o_rq = 3 * d_fox
    o_rk = o_rq + d_ret
    o_rv = o_rk + d_ret
    o_rg = o_rv + d_ret
    o_mq = o_rg + d_ret

    heads = lambda d: [slice(h * HEAD_DIM, (h + 1) * HEAD_DIM) for h in range(d // HEAD_DIM)]

    z = _dot(xb, wf_ref[...]) + bf_ref[...]
    fq_ref[...] = (proj(0, d_fox) * (SCALE * LOG2E)).astype(BF16)
    lf = jnp.minimum(z, 0.0) - jnp.log1p(jnp.exp(-jnp.abs(z)))
    lf_ref[...] = lf

    @pl.when(i % tiles_per_seq == 0)
    def _():
        carry_ref[...] = jnp.zeros_like(carry_ref)

    parts = _dot(tri_ref[...], jnp.concatenate([t.astype(BF16) for t in _split3_f32(lf)], axis=1))
    c = parts[:, 0:LANES] + parts[:, LANES:2 * LANES] + parts[:, 2 * LANES:] + carry_ref[0:1, :]
    c_ref[...] = c
    tm = c.shape[0]
    carry_ref[...] = jnp.broadcast_to(c[tm - 1:tm, :], carry_ref.shape)

    @pl.when(i % blk_tiles == 0)
    def _():
        start_ref[...] = jnp.broadcast_to(c[0:1, :], start_ref.shape)

    rel3 = _split3_f32((c - start_ref[0:1, :]) * LOG2E)
    zfk = proj(o_fk, d_fox)
    zfv = proj(o_fv, d_fox)
    for h, sl in enumerate(heads(d_fox)):
        fk_ref[0, h] = zfk[:, sl]
        fv_ref[0, h] = zfv[:, sl]
        qa_ref[0, h], ka_ref[0, h], va_ref[0, h] = _fox_operands(fq_ref[:, sl], zfk[:, sl], zfv[:, sl], rel3, h)

    ca = base_ref[0, 0:1, :]
    sa = base_ref[0, 1:2, :]
    cos = ca * rot_ref[0] - sa * rot_ref[1]
    sin = sa * rot_ref[2] + ca * rot_ref[3]

    def rope(t):
        return t * cos + pltpu.roll(t, HEAD_DIM // 2, axis=1) * sin

    def ret_q():
        zq = proj(o_rq, d_ret)
        for sl in heads(d_ret):
            rq_ref[:, sl] = rope(zq[:, sl]).astype(BF16)

    def ret_k():
        zk = proj(o_rk, d_ret)
        for sl in heads(d_ret):
            rk_ref[:, sl] = (rope(zk[:, sl]) * SCALE).astype(BF16)

    def ret_v():
        rv_ref[...] = proj(o_rv, d_ret).astype(BF16)

    def ret_g():
        rg_ref[...] = proj(o_rg, d_ret).astype(BF16)

    zm = proj(o_mq, d_memq)
    pending = [ret_q, ret_k, ret_v, ret_g]
    for h, sl in enumerate(heads(d_memq)):
        s = _dot_nt((zm[:, sl] * SCALE).astype(BF16), mk_ref[0, h].astype(BF16))
        p = jnp.exp(s - jnp.max(s, axis=-1, keepdims=True))
        if pending:
            pending.pop(0)()
        o = _dot(p.astype(BF16), mv_ref[0, h].astype(BF16))
        mo_ref[:, sl] = (o / jnp.sum(p, axis=-1, keepdims=True)).astype(BF16)
    for projection in pending:
        projection()


def _in_proj(x, wa, wf, bfp, base_tab, rot_tab, mk, mv, *, tm, seq_len, d_fox, d_ret, d_memq, attn_block):
    m, d_model = x.shape
    assert m % tm == 0 and seq_len % tm == 0
    assert base_tab.shape == (m // tm, 8, HEAD_DIM) and rot_tab.shape == (4, tm, HEAD_DIM)
    tiles_per_seq = seq_len // tm
    n_fox = d_fox // HEAD_DIM
    tri = jnp.asarray(np.arange(tm)[:, None] >= np.arange(tm)[None, :], dtype=BF16)
    row = lambda w: pl.BlockSpec((tm, w), lambda i: (i, 0))
    memspec = pl.BlockSpec((1,) + mk.shape[1:], lambda i: (i // tiles_per_seq, 0, 0, 0))
    kvspec = pl.BlockSpec((1, n_fox, tm, HEAD_DIM), lambda i: (i // tiles_per_seq, 0, i % tiles_per_seq, 0))
    augspec = pl.BlockSpec((1, n_fox, tm, 2 * HEAD_DIM),
                           lambda i: (i // tiles_per_seq, 0, i % tiles_per_seq, 0))
    aug = jax.ShapeDtypeStruct((m // seq_len, n_fox, seq_len, 2 * HEAD_DIM), BF16)
    assert attn_block % tm == 0 and seq_len % attn_block == 0
    kern = functools.partial(_in_proj_kernel, d_fox=d_fox, d_ret=d_ret, d_memq=d_memq,
                             tiles_per_seq=tiles_per_seq, blk_tiles=attn_block // tm)
    out_shape = (
        jax.ShapeDtypeStruct((m, d_fox), BF16),
        jax.ShapeDtypeStruct((m // seq_len, n_fox, seq_len, HEAD_DIM), F32),
        jax.ShapeDtypeStruct((m // seq_len, n_fox, seq_len, HEAD_DIM), F32),
        jax.ShapeDtypeStruct((m, LANES), F32),
        jax.ShapeDtypeStruct((m, LANES), F32),
        jax.ShapeDtypeStruct((m, d_ret), BF16),
        jax.ShapeDtypeStruct((m, d_ret), BF16),
        jax.ShapeDtypeStruct((m, d_ret), BF16),
        jax.ShapeDtypeStruct((m, d_ret), BF16),
        jax.ShapeDtypeStruct((m, d_memq), BF16),
        aug, aug, aug,
    )
    return pl.pallas_call(
        kern,
        grid=(m // tm,),
        in_specs=[row(d_model), _resident(wa.shape), _resident(wf.shape), _resident(bfp.shape),
                  pl.BlockSpec((1, 8, HEAD_DIM), lambda i: (i, 0, 0)), _resident(rot_tab.shape),
                  memspec, memspec, _resident(tri.shape)],
        out_specs=(row(d_fox), kvspec, kvspec, row(LANES), row(LANES),
                   row(d_ret), row(d_ret), row(d_ret), row(d_ret), row(d_memq),
                   augspec, augspec, augspec),
        out_shape=out_shape,
        scratch_shapes=[pltpu.VMEM((8, LANES), F32), pltpu.VMEM((8, LANES), F32)],
        compiler_params=_cparams(("arbitrary",)),
        name="in_proj",
    )(x, wa, wf, bfp, base_tab, rot_tab, mk, mv, tri)


def _split3_f32(a):
    hi = a.astype(BF16).astype(F32)
    r1 = a - hi
    mid = r1.astype(BF16).astype(F32)
    lo = (r1 - mid).astype(BF16).astype(F32)
    return hi, mid, lo


def _fox_operands(q, k, v, rel3, h):
    lane = lax.broadcasted_iota(jnp.int32, q.shape, 1)
    hi, mid, lo = (t[:, h:h + 1] for t in rel3)
    qbias = jnp.where(lane == 0, hi, jnp.where(lane == 1, mid, jnp.where(lane == 2, lo,
                      jnp.where(lane < 6, 1.0, 0.0))))
    kbias = jnp.where(lane < 3, 1.0, jnp.where(lane == 3, -hi, jnp.where(lane == 4, -mid,
                      jnp.where(lane == 5, -lo, 0.0))))
    ones_col = jnp.where(lane == 0, 1.0, 0.0)
    cat = lambda a, b: jnp.concatenate([a.astype(BF16), b.astype(BF16)], axis=1)
    return cat(q, qbias), cat(k, kbias), cat(v, ones_col)


def _fox_prompt_kernel(qa, k_ref, v_ref, ccol_ref, o_ref, kbuf, vbuf, cstart, acc, sbuf, tbuf, *, tb, nsub,
                       kv_per_trip):
    h = pl.program_id(0)
    qi = pl.program_id(1)
    tq = tb * nsub
    row0 = pl.multiple_of(qi * tq, tq)
    kbuf[pl.ds(row0, tq), :] = k_ref[...]
    vbuf[pl.ds(row0, tq), :] = v_ref[...]
    lane = lax.broadcasted_iota(jnp.int32, (8, LANES), 1)
    for s in range(nsub):
        start = jnp.sum(jnp.where(lane == h, ccol_ref[s * tb:s * tb + 8, :], 0.0), axis=1, keepdims=True)
        cstart[nsub * qi + s] = jnp.broadcast_to(start[0:1], (8, LANES))
    acc[...] = jnp.zeros_like(acc)

    n_ct = tb // LANES
    subs = range(nsub)

    def lane_tiles(s2):
        return [s2[:, t * LANES:(t + 1) * LANES] for t in range(n_ct)]

    def scores(s, kj):
        col0 = pl.multiple_of(kj * tb, tb)
        s2 = _dot_nt(qa[s * tb:(s + 1) * tb, :], kbuf[pl.ds(col0, tb), :])
        return s2, functools.reduce(jnp.maximum, lane_tiles(s2))

    def softmax_pv(s, kj, sc, m):
        col0 = pl.multiple_of(kj * tb, tb)
        s2, tmax = sc
        tiles = lane_tiles(s2)
        delta = (cstart[nsub * qi + s][0:1, :] - cstart[kj][0:1, :]) * LOG2E
        m_rel = m - delta
        m_new = jnp.maximum(m_rel, jnp.max(tmax, axis=-1, keepdims=True))
        alpha = jnp.exp2(m_rel - m_new)
        p = jnp.concatenate([jnp.exp2(t - m_new).astype(BF16) for t in tiles], axis=1)
        pv = _dot(p, vbuf[pl.ds(col0, tb), :])
        a = acc[s]
        acc[s] = jnp.concatenate([alpha * a[:, 0:LANES], alpha * a[:, LANES:]], axis=1) + pv
        return m_new + delta

    for s in subs:
        sbuf[s], tbuf[s] = scores(s, 0)

    def body(t, ms):
        kj = kv_per_trip * t
        cur = [(sbuf[s], tbuf[s]) for s in subs]
        for u in range(kv_per_trip):
            nxt = [scores(s, kj + u + 1) for s in subs]
            ms = [softmax_pv(s, kj + u, cur[s], ms[s]) for s in subs]
            cur = nxt
        for s in subs:
            sbuf[s], tbuf[s] = cur[s]
        return tuple(ms)

    assert nsub % kv_per_trip == 0
    ms = lax.fori_loop(0, (nsub // kv_per_trip) * qi, body,
                       tuple(jnp.full((tb, LANES), NEG_INF, F32) for _ in subs))
    half = tb // 2
    r_half = lax.broadcasted_iota(jnp.int32, (half, LANES), 0)
    c_half = lax.broadcasted_iota(jnp.int32, (half, LANES), 1)

    def diagonal_unit(s, kj, m, s2_full):
        col0 = pl.multiple_of(kj * tb, tb)
        delta = (cstart[nsub * qi + s][0:1, :] - cstart[kj][0:1, :]) * LOG2E
        for ra in range(2):
            r0 = ra * half
            m_r = m[r0:r0 + half] - delta
            a = acc[s, r0:r0 + half, :]
            for ca in range(ra + 1):
                c0 = pl.multiple_of(col0 + ca * half, half)
                if s2_full is None:
                    s2 = _dot_nt(qa[s * tb + r0:s * tb + r0 + half, :], kbuf[pl.ds(c0, half), :])
                else:
                    s2 = s2_full[r0:r0 + half, ca * half:(ca + 1) * half]
                tiles = [s2[:, t * LANES:(t + 1) * LANES] for t in range(half // LANES)]
                if ca == ra:
                    tiles = [jnp.where(c_half + t * LANES <= r_half, tl, NEG_INF) for t, tl in enumerate(tiles)]
                m_new = jnp.maximum(m_r, jnp.max(functools.reduce(jnp.maximum, tiles), axis=-1, keepdims=True))
                alpha = jnp.exp2(m_r - m_new)
                p = jnp.concatenate([jnp.exp2(tl - m_new).astype(BF16) for tl in tiles], axis=1)
                pv = _dot(p, vbuf[pl.ds(c0, half), :])
                a = jnp.concatenate([alpha * a[:, 0:LANES], alpha * a[:, LANES:]], axis=1) + pv
                m_r = m_new
            acc[s, r0:r0 + half, :] = a

    for s in subs:
        m = ms[s]
        for t in range(s):
            sc = (sbuf[s], tbuf[s]) if t == 0 else scores(s, nsub * qi + t)
            m = softmax_pv(s, nsub * qi + t, sc, m)
        diagonal_unit(s, nsub * qi + s, m, sbuf[s] if s == 0 else None)
        a = acc[s]
        o_ref[s * tb:(s + 1) * tb, :] = (a[:, 0:HEAD_DIM] / a[:, HEAD_DIM:HEAD_DIM + 1]).astype(BF16)


def _fox_prompt(qa, ka, va, c_col, *, tb, nsub):
    _, n_heads, s, width = qa.shape
    tq = tb * nsub
    assert s % tq == 0
    op = lambda: pl.BlockSpec((None, None, tq, width), lambda h, qi: (0, h, qi, 0))
    return pl.pallas_call(
        functools.partial(_fox_prompt_kernel, tb=tb, nsub=nsub, kv_per_trip=min(nsub, 4)),
        grid=(n_heads, s // tq),
        in_specs=[op(), op(), op(), pl.BlockSpec((tq, LANES), lambda h, qi: (qi, 0))],
        out_specs=pl.BlockSpec((tq, HEAD_DIM), lambda h, qi: (qi, h)),
        out_shape=jax.ShapeDtypeStruct((s, n_heads * HEAD_DIM), BF16),
        scratch_shapes=[pltpu.VMEM((s, width), BF16),
                        pltpu.VMEM((s, width), BF16),
                        pltpu.VMEM((s // tb, 8, LANES), F32),
                        pltpu.VMEM((nsub, tb, width), F32),
                        pltpu.VMEM((nsub, tb, tb), F32),
                        pltpu.VMEM((nsub, tb, LANES), F32)],
        compiler_params=_cparams(("arbitrary", "arbitrary")),
        name="fox_prompt",
    )(qa, ka, va, c_col)


def _fox_sample_kernel(q_ref, kc_ref, vc_ref, kn_ref, vn_ref, lfc_ref, lfn_ref, ccol_ref,
                       tric_ref, trin_ref, o_ref, *, n_heads):
    c_cache = _split3_dot(lfc_ref[0], tric_ref[...])
    p_len = c_cache.shape[1]
    ck_cache = c_cache - c_cache[:, p_len - 1:p_len]
    ck_new = _split3_dot(lfn_ref[0], trin_ref[...])
    ccol = ccol_ref[...]
    n_new = ck_new.shape[1]
    r = lax.broadcasted_iota(jnp.int32, (n_new, n_new), 0)
    c = lax.broadcasted_iota(jnp.int32, (n_new, n_new), 1)
    for h in range(n_heads):
        sl = slice(h * HEAD_DIM, (h + 1) * HEAD_DIM)
        q = q_ref[:, sl]
        cq = ccol[:, h:h + 1]
        s_c = _dot_nt(q, kc_ref[0, h].astype(BF16)) + (cq - ck_cache[h:h + 1, :]) * LOG2E
        s_n = _dot_nt(q, kn_ref[0, h].astype(BF16)) + (cq - ck_new[h:h + 1, :]) * LOG2E
        s_n = jnp.where(c <= r, s_n, NEG_INF)
        m = jnp.maximum(jnp.max(s_c, axis=-1, keepdims=True), jnp.max(s_n, axis=-1, keepdims=True))
        p_c = jnp.exp2(s_c - m)
        p_n = jnp.exp2(s_n - m)
        l = jnp.sum(p_c, axis=-1, keepdims=True) + jnp.sum(p_n, axis=-1, keepdims=True)
        o = _dot(p_c.astype(BF16), vc_ref[0, h].astype(BF16)) + \
            _dot(p_n.astype(BF16), vn_ref[0, h].astype(BF16))
        o_ref[:, sl] = (o / l).astype(BF16)


def _fox_sample(fq, kc, vc, kn, vn, lfc_row, lfn_row, c_col, *, n_heads, seq_len):
    m, d_fox = fq.shape
    nb, _, p_len, _ = kc.shape
    tric = jnp.asarray(np.arange(p_len)[:, None] <= np.arange(p_len)[None, :], dtype=BF16)
    trin = jnp.asarray(np.arange(seq_len)[:, None] <= np.arange(seq_len)[None, :], dtype=BF16)
    row = lambda w: pl.BlockSpec((seq_len, w), lambda b: (b, 0))
    cache = pl.BlockSpec((1, n_heads, p_len, HEAD_DIM), lambda b: (b, 0, 0, 0))
    new = pl.BlockSpec((1, n_heads, seq_len, HEAD_DIM), lambda b: (b, 0, 0, 0))
    return pl.pallas_call(
        functools.partial(_fox_sample_kernel, n_heads=n_heads),
        grid=(nb,),
        in_specs=[row(d_fox), cache, cache, new, new,
                  pl.BlockSpec((1, HEAD_ROWS, p_len), lambda b: (b, 0, 0)),
                  pl.BlockSpec((1, HEAD_ROWS, seq_len), lambda b: (b, 0, 0)),
                  row(LANES), _resident(tric.shape), _resident(trin.shape)],
        out_specs=row(d_fox),
        out_shape=jax.ShapeDtypeStruct((m, d_fox), BF16),
        compiler_params=_cparams(("arbitrary",)),
        name="fox_sample",
    )(fq, kc, vc, kn, vn, lfc_row, lfn_row, c_col, tric, trin)


def _log_gamma(h):
    return float(np.log1p(-np.exp2(np.float32(-5.0 - h)), dtype=np.float32))


def _retention_kernel(q_ref, k_ref, v_ref, g_ref, s0_ref, gn_ref, o_ref, s_ref, decay_ref, *, n_heads,
                      t_blk):
    step = pl.program_id(1)

    @pl.when(step == 0)
    def _():
        s_ref[...] = s0_ref[...]

    @pl.when(jnp.logical_and(pl.program_id(0) == 0, step == 0))
    def _():
        ti = lax.broadcasted_iota(jnp.int32, (t_blk, t_blk), 0)
        si = lax.broadcasted_iota(jnp.int32, (t_blk, t_blk), 1)
        dist = jnp.abs(ti - si).astype(F32)
        shift = CHUNK.bit_length() - 1
        visible = jnp.right_shift(si, shift) <= jnp.right_shift(ti, shift)
        for h in range(n_heads):
            decay_ref[h] = jnp.where(visible, jnp.exp(_log_gamma(h) * dist), 0.0)

    tcol = lax.broadcasted_iota(jnp.int32, (t_blk, 1), 0).astype(F32)
    for h in range(n_heads):
        lg = _log_gamma(h)
        sl = slice(h * HEAD_DIM, (h + 1) * HEAD_DIM)
        q = q_ref[:, sl]
        k = k_ref[:, sl]
        v = v_ref[:, sl]
        a = _dot_nt(q, k) * decay_ref[h]
        state = s_ref[0, h]
        o = _dot(a.astype(BF16), v) + _dot(q, state.astype(BF16)) * jnp.exp(lg * (tcol + 1.0))
        kd = (k.astype(F32) * jnp.exp(lg * (t_blk - 1.0 - tcol))).astype(BF16)
        s_ref[0, h] = math.exp(lg * t_blk) * state + _dot_tn(kd, v)
        mu = jnp.mean(o, axis=-1, keepdims=True)
        oc = o - mu
        var = jnp.mean(oc * oc, axis=-1, keepdims=True)
        rn = oc * lax.rsqrt(var + GN_EPS) * gn_ref[:, sl]
        gate = g_ref[:, sl].astype(F32)
        o_ref[:, sl] = (gate / (1.0 + jnp.exp(-gate)) * rn).astype(BF16)


def _retention(rq, rk, rv, rg, s0, gn_g, *, t_blk, seq_len, n_heads):
    m, d_ret = rq.shape
    nb = m // seq_len
    steps = seq_len // t_blk
    row = lambda: pl.BlockSpec((t_blk, d_ret), lambda b, s: (b * steps + s, 0))
    st = lambda: pl.BlockSpec((1, n_heads, HEAD_DIM, HEAD_DIM), lambda b, s: (b, 0, 0, 0))
    return pl.pallas_call(
        functools.partial(_retention_kernel, n_heads=n_heads, t_blk=t_blk),
        grid=(nb, steps),
        in_specs=[row(), row(), row(), row(), st(), pl.BlockSpec((1, d_ret), lambda b, s: (0, 0))],
        out_specs=(row(), st()),
        out_shape=(jax.ShapeDtypeStruct((m, d_ret), BF16),
                   jax.ShapeDtypeStruct((nb, n_heads, HEAD_DIM, HEAD_DIM), F32)),
        scratch_shapes=[pltpu.VMEM((n_heads, t_blk, t_blk), F32)],
        compiler_params=_cparams(("arbitrary", "arbitrary")),
        name="retention",
    )(rq, rk, rv, rg, s0, gn_g)


def _mix_out_kernel(x_ref, fo_ref, ro_ref, mo_ref, wo_ref, g_ref, b_ref, o_ref, *rest, alpha, emit_weight):
    wo_out, cat_ref = rest if emit_weight else (None,) + rest
    d_fox = fo_ref.shape[1]
    d_ret = ro_ref.shape[1]
    cat_ref[:, 0:d_fox] = fo_ref[...]
    cat_ref[:, d_fox:d_fox + d_ret] = ro_ref[...]
    cat_ref[:, d_fox + d_ret:] = mo_ref[...]
    wo = wo_ref[...]
    if emit_weight:
        wo = wo.astype(BF16)
        wo_out[...] = wo
    tm = x_ref.shape[0]
    pieces = [slice(0, tm // 2), slice(tm // 2, tm)] if tm % 32 == 0 else [slice(0, tm)]
    mix = [_dot(cat_ref[r, :], wo) for r in pieces]
    for r, mx in zip(pieces, mix):
        o_ref[r, :] = _layernorm(alpha * x_ref[r, :] + mx, g_ref[...], b_ref[...])


def _mix_out(x, fo, ro, mo, wo, g, b, *, tm, alpha):
    m, d_model = x.shape
    emit = wo.dtype != BF16
    assert not emit or m == tm
    row = lambda w: pl.BlockSpec((tm, w), lambda i: (i, 0))
    out_specs, out_shape = row(d_model), jax.ShapeDtypeStruct((m, d_model), F32)
    if emit:
        out_specs = (out_specs, pl.BlockSpec(wo.shape, lambda i: (0, 0)))
        out_shape = (out_shape, jax.ShapeDtypeStruct(wo.shape, BF16))
    return pl.pallas_call(
        functools.partial(_mix_out_kernel, alpha=alpha, emit_weight=emit),
        grid=(m // tm,),
        in_specs=[row(d_model), row(fo.shape[1]), row(ro.shape[1]), row(mo.shape[1]),
                  _resident(wo.shape), _resident(g.shape), _resident(b.shape)],
        out_specs=out_specs,
        out_shape=out_shape,
        scratch_shapes=[pltpu.VMEM((tm, wo.shape[0]), BF16)],
        compiler_params=_cparams(("arbitrary",)),
        name="mix_out",
    )(x, fo, ro, mo, wo, g, b)


def _gelu_tanh(x):
    c = math.sqrt(2.0 / math.pi)
    return x * (0.5 * (1.0 + jnp.tanh(c * (x + 0.044715 * (x * x * x)))))


def _conv_ffn_kernel(h_ref, wg_ref, wu_ref, wd_ref, cw_ref, cb_ref, past_ref, g_ref, b_ref,
                     y_ref, cv_ref, *rest, alpha, seq_len, tm, tc, emit_weights):
    wg_out, wu_out, wd_out = rest[:3] if emit_weights else (None, None, None)
    hb_ref, gbuf_ref, carry_ref = rest[3:] if emit_weights else rest

    def weight(ref, out, idx):
        w = ref[idx]
        if out is not None:
            w = w.astype(BF16)
            out[idx] = w
        return w

    i = pl.program_id(0)
    f = pl.program_id(1)
    nf = pl.num_programs(1)
    lt = min(seq_len, tm)
    ns = tm // lt
    halo = 8
    carried = seq_len > tm

    @pl.when(f == 0)
    def _():
        h = h_ref[...]
        hb_ref[...] = h.astype(BF16)
        y_ref[...] = alpha * h

    if carried:
        @pl.when(jnp.logical_and(i == 0, f == 0))
        def _():
            carry_ref[...] = jnp.zeros_like(carry_ref)

        tiles_per_seq = seq_len // tm
        gbuf_ref[0, halo - 2:halo, :] = jnp.where(i % tiles_per_seq == 0, past_ref[i // tiles_per_seq],
                                                  carry_ref[f])
    else:
        for s in range(ns):
            gbuf_ref[s, halo - 2:halo, :] = past_ref[i * ns + s]

    hb = hb_ref[...]
    hid = []
    for c0 in range(0, wg_ref.shape[1], tc):
        cs = slice(c0, c0 + tc)
        gate = _dot(hb, weight(wg_ref, wg_out, (slice(None), cs)))
        up = _dot(hb, weight(wu_ref, wu_out, (slice(None), cs)))
        w0 = cw_ref[0:1, cs]
        w1 = cw_ref[1:2, cs]
        w2 = cw_ref[2:3, cs]
        cb = cb_ref[:, cs]
        parts = []
        for s in range(ns):
            gs = gate[s * lt:(s + 1) * lt]
            gbuf_ref[s, halo:halo + lt, cs] = gs
            g1 = gbuf_ref[s, halo - 1:halo - 1 + lt, cs]
            g2 = gbuf_ref[s, halo - 2:halo - 2 + lt, cs]
            gc = cb + w0 * g2 + w1 * g1 + w2 * gs
            parts.append((_gelu_tanh(gc) * up[s * lt:(s + 1) * lt]).astype(BF16))
        hid.append(parts[0] if ns == 1 else jnp.concatenate(parts, axis=0))
    hid = hid[0] if len(hid) == 1 else jnp.concatenate(hid, axis=1)
    y_ref[...] += _dot(hid, weight(wd_ref, wd_out, (slice(None), slice(None))))

    for s in range(ns):
        last2 = gbuf_ref[s, halo + lt - 2:halo + lt, :]
        cv_ref[s] = last2
        if carried:
            carry_ref[f] = last2

    @pl.when(f == nf - 1)
    def _():
        y_ref[...] = _layernorm(y_ref[...], g_ref[...], b_ref[...])


def _conv_ffn(h1, wg, wu, wd, conv_w, conv_b, past, g, b, *, tm, tf, seq_len, alpha):
    tc = 2 * LANES
    assert tf % tc == 0
    m, d_model = h1.shape
    d_ff = wg.shape[1]
    n_seq = past.shape[0]
    assert m % tm == 0 and d_ff % tf == 0 and (seq_len % tm == 0 or tm % seq_len == 0)
    nf = d_ff // tf
    lt = min(seq_len, tm)
    emit = wg.dtype != BF16
    assert not emit or m == tm
    w_specs = [pl.BlockSpec((d_model, tf), lambda i, f: (0, f)),
               pl.BlockSpec((d_model, tf), lambda i, f: (0, f)),
               pl.BlockSpec((tf, d_model), lambda i, f: (f, 0))]
    w_shapes = [jax.ShapeDtypeStruct(t.shape, BF16) for t in (wg, wu, wd)] if emit else []
    y, cv_all, *w_bf = pl.pallas_call(
        functools.partial(_conv_ffn_kernel, alpha=alpha, seq_len=seq_len, tm=tm, tc=tc, emit_weights=emit),
        grid=(m // tm, nf),
        in_specs=[pl.BlockSpec((tm, d_model), lambda i, f: (i, 0)),
                  *w_specs,
                  pl.BlockSpec((CONV_W, tf), lambda i, f: (0, f)),
                  pl.BlockSpec((1, tf), lambda i, f: (0, f)),
                  pl.BlockSpec((n_seq, CONV_W - 1, tf), lambda i, f: (0, 0, f)),
                  pl.BlockSpec((1, d_model), lambda i, f: (0, 0)),
                  pl.BlockSpec((1, d_model), lambda i, f: (0, 0))],
        out_specs=(pl.BlockSpec((tm, d_model), lambda i, f: (i, 0)),
                   pl.BlockSpec((tm // lt, CONV_W - 1, tf), lambda i, f: (i, 0, f)),
                   *(w_specs if emit else [])),
        out_shape=(jax.ShapeDtypeStruct((m, d_model), F32),
                   jax.ShapeDtypeStruct((m // lt, CONV_W - 1, d_ff), F32), *w_shapes),
        scratch_shapes=[pltpu.VMEM((tm, d_model), BF16),
                        pltpu.VMEM((tm // lt, 8 + lt, tf), F32),
                        pltpu.VMEM((nf, CONV_W - 1, tf), F32)],
        compiler_params=_cparams(("arbitrary", "arbitrary")),
        name="conv_ffn",
    )(h1, wg, wu, wd, conv_w, conv_b, past, g, b)
    return y, cv_all.reshape(n_seq, seq_len // lt, CONV_W - 1, d_ff)[:, -1], w_bf


def _rope_tables(tile_base, tm):
    half = HEAD_DIM // 2
    inv = ROPE_BASE ** (-jnp.arange(half, dtype=F32) / half)
    inv = jnp.concatenate([inv, inv])
    sign = jnp.concatenate([-jnp.ones((half,), F32), jnp.ones((half,), F32)])
    a = tile_base.astype(F32)[:, None] * inv[None, :]
    b = jnp.arange(tm, dtype=F32)[:, None] * inv[None, :]
    base_tab = jnp.stack([jnp.cos(a), jnp.sin(a)], axis=1)
    base_tab = jnp.pad(base_tab, ((0, 0), (0, 6), (0, 0)))
    rot_tab = jnp.stack([jnp.cos(b), jnp.sin(b), sign * jnp.cos(b), sign * jnp.sin(b)])
    return base_tab, rot_tab


def _layer(x, seq_start, seq_len, mk, mv, w, *, alpha, tm_proj, tm_mix, tm_ffn, tf, t_ret, s0, conv_past,
           fox_fn, attn_block):
    tile_base = (seq_start[:, None] + jnp.arange(0, seq_len, tm_proj)[None, :]).reshape(-1)
    base_tab, rot_tab = _rope_tables(tile_base, tm_proj)
    fq, fk, fv, lf, c, rq, rk, rv, rg, mo, qa, ka, va = _in_proj(
        x, w["wa"], w["wf"], w["bf"], base_tab, rot_tab, mk, mv, tm=tm_proj, seq_len=seq_len,
        d_fox=w["d_fox"], d_ret=w["d_ret"], d_memq=w["d_memq"], attn_block=attn_block)
    fo = fox_fn(fq, fk, fv, lf, c, qa, ka, va)
    ro, s_new = _retention(rq, rk, rv, rg, s0, w["gn_g"], t_blk=t_ret, seq_len=seq_len,
                           n_heads=w["d_ret"] // HEAD_DIM)
    w_bf = {}
    h1 = _mix_out(x, fo, ro, mo, w["wo"], w["ln1_g"], w["ln1_b"], tm=tm_mix, alpha=alpha)
    if w["wo"].dtype != BF16:
        h1, w_bf["wo"] = h1
    y, cv, ffn_bf = _conv_ffn(h1, w["wg"], w["wu"], w["wd"], w["conv_w"], w["conv_b"], conv_past,
                              w["ln2_g"], w["ln2_b"], tm=tm_ffn, tf=tf, seq_len=seq_len, alpha=alpha)
    w_bf.update(zip(("wg", "wu", "wd"), ffn_bf))
    return y, fk, fv, lf, s_new, cv, w_bf


def kernel(x_prompt, x_sample, cache_fox_k, cache_fox_v, cache_fox_logf, state_ret, cache_mem_k,
           cache_mem_v, state_conv, mem_prompt, w_in, b_f, w_mem_kv, ret_gn_g, w_o, ln1_g, ln1_b,
           w_gate, w_up, conv_w, conv_b, w_down, ln2_g, ln2_b):
    batch, seq, d_model = x_prompt.shape
    dec_batch, dec_seq, _ = x_sample.shape
    depth = w_in.shape[0]
    past_len = cache_fox_k.shape[2]
    n_fox = cache_fox_k.shape[3]
    n_ret = state_ret.shape[2]
    n_mem = cache_mem_k.shape[3]
    n_mem_tok = cache_mem_k.shape[2]
    d_fox, d_ret, d_memq = n_fox * HEAD_DIM, n_ret * HEAD_DIM, n_mem * HEAD_DIM
    d_ff = w_gate.shape[2]
    assert batch == 1 and depth == 1 and n_fox <= 8
    alpha = (2.0 * depth) ** 0.25
    l = 0

    o_ff = 3 * d_fox
    w_fm = jnp.transpose(w_in, (2, 0, 1))
    wa, wf = _w_prep(w_fm, o_ff=o_ff, n_fox=n_fox, tn=2 * LANES)
    bfp = jnp.pad(b_f[l], (0, LANES - n_fox)).reshape(1, LANES)
    w = dict(wa=wa, wf=wf, bf=bfp, d_fox=d_fox, d_ret=d_ret, d_memq=d_memq,
             gn_g=ret_gn_g[l].reshape(1, d_ret), wo=w_o[l],
             ln1_g=ln1_g[l].reshape(1, d_model), ln1_b=ln1_b[l].reshape(1, d_model),
             wg=w_gate[l], wu=w_up[l], wd=w_down[l],
             conv_w=conv_w[l], conv_b=conv_b[l].reshape(1, d_ff),
             ln2_g=ln2_g[l].reshape(1, d_model), ln2_b=ln2_b[l].reshape(1, d_model))

    to_head_major = lambda t: jnp.swapaxes(t, -3, -2)

    m_s = dec_batch * dec_seq
    kc = to_head_major(cache_fox_k[l])
    vc = to_head_major(cache_fox_v[l])
    lfc_row = jnp.pad(jnp.swapaxes(cache_fox_logf[l], 1, 2), ((0, 0), (0, HEAD_ROWS - n_fox), (0, 0)))

    def fox_sample_fn(fq, fk, fv, lf, c, qa, ka, va):
        lfn_row = jnp.swapaxes(lf.reshape(dec_batch, dec_seq, LANES)[:, :, :HEAD_ROWS], 1, 2)
        return _fox_sample(fq, kc, vc, fk, fv, lfc_row, lfn_row, c, n_heads=n_fox, seq_len=dec_seq)

    y_s, fk_s, fv_s, lf_s, s_s, cv_s, w_bf = _layer(
        x_sample.reshape(m_s, d_model), jnp.full((dec_batch,), past_len, jnp.int32), dec_seq,
        to_head_major(cache_mem_k[l]), to_head_major(cache_mem_v[l]), w,
        alpha=alpha, tm_proj=dec_seq, tm_mix=m_s, tm_ffn=m_s, tf=4 * LANES, t_ret=dec_seq,
        s0=state_ret[l], conv_past=state_conv[l], fox_fn=fox_sample_fn, attn_block=dec_seq)

    mkv = _mem_kv(mem_prompt.reshape(n_mem_tok, d_model), w_mem_kv[l].astype(BF16))
    mk_p, mv_p = mkv[None, :n_mem], mkv[None, n_mem:]

    fox_tb = 1024

    def fox_prompt_fn(fq, fk, fv, lf, c, qa, ka, va):
        return _fox_prompt(qa, ka, va, c, tb=fox_tb, nsub=2)

    y_p, fk_p, fv_p, lf_p, s_p, cv_p, _ = _layer(
        x_prompt.reshape(seq, d_model), jnp.zeros((batch,), jnp.int32), seq, mk_p, mv_p, {**w, **w_bf},
        alpha=alpha, tm_proj=256, tm_mix=512, tm_ffn=1024, tf=512, t_ret=256,
        s0=jnp.zeros((1, n_ret, HEAD_DIM, HEAD_DIM), F32),
        conv_past=jnp.zeros((1, CONV_W - 1, d_ff), F32), fox_fn=fox_prompt_fn, attn_block=fox_tb)

    return (y_p.reshape(batch, seq, d_model), y_s.reshape(dec_batch, dec_seq, d_model),
            to_head_major(fk_p)[None], to_head_major(fv_p)[None],
            lf_p[:, :n_fox].reshape(1, batch, seq, n_fox),
            s_p.reshape(1, batch, n_ret, HEAD_DIM, HEAD_DIM),
            to_head_major(mk_p)[None], to_head_major(mv_p)[None],
            cv_p.reshape(1, batch, CONV_W - 1, d_ff),
            to_head_major(fk_s)[None], to_head_major(fv_s)[None],
            lf_s[:, :n_fox].reshape(1, dec_batch, dec_seq, n_fox),
            s_s.reshape(1, dec_batch, n_ret, HEAD_DIM, HEAD_DIM),
            cv_s.reshape(1, dec_batch, CONV_W - 1, d_ff))
```

```python
import functools
import math

import numpy as np
import jax
import jax.numpy as jnp
from jax import lax
from jax.experimental import pallas as pl
from jax.experimental.pallas import tpu as pltpu

F32 = jnp.float32
BF16 = jnp.bfloat16

HEAD_DIM = 128
CHUNK = 64
CONV_W = 3
ROPE_BASE = 10000.0
LN_EPS = 1e-5
GN_EPS = 1e-5
LANES = 128
V7X_VMEM_BYTES = 64 * 1024 * 1024
VMEM_LIMIT = V7X_VMEM_BYTES - 2 * 1024 * 1024
SCALE = HEAD_DIM ** -0.5
LOG2E = math.log2(math.e)
NEG_INF = float("-inf")
HEAD_ROWS = 16


def _cparams(sem):
    return pltpu.CompilerParams(dimension_semantics=sem, vmem_limit_bytes=VMEM_LIMIT)


def _resident(shape):
    nd = len(shape)
    return pl.BlockSpec(shape, lambda *_: (0,) * nd, pipeline_mode=pl.Buffered(1))


def _dot(a, b):
    return jnp.dot(a, b, preferred_element_type=F32)


def _dot_nt(a, b):
    return lax.dot_general(a, b, (((1,), (1,)), ((), ())), preferred_element_type=F32)


def _dot_tn(a, b):
    return lax.dot_general(a, b, (((0,), (0,)), ((), ())), preferred_element_type=F32)


def _split3_dot(a, tri):
    hi = a.astype(BF16)
    r1 = a - hi.astype(F32)
    mid = r1.astype(BF16)
    lo = (r1 - mid.astype(F32)).astype(BF16)
    return _dot(hi, tri) + _dot(mid, tri) + _dot(lo, tri)


def _layernorm(z, g, b):
    mu = jnp.mean(z, axis=-1, keepdims=True)
    zc = z - mu
    var = jnp.mean(zc * zc, axis=-1, keepdims=True)
    return zc * lax.rsqrt(var + LN_EPS) * g + b


def _memkv_kernel(m_ref, w_ref, o_ref):
    kv = _dot(m_ref[...].astype(BF16), w_ref[...])
    for j in range(o_ref.shape[0]):
        o_ref[j] = kv[:, j * HEAD_DIM:(j + 1) * HEAD_DIM]


def _mem_kv(mem, w_bf):
    m, _ = mem.shape
    return pl.pallas_call(
        _memkv_kernel,
        out_shape=jax.ShapeDtypeStruct((w_bf.shape[1] // HEAD_DIM, m, HEAD_DIM), F32),
        compiler_params=pltpu.CompilerParams(vmem_limit_bytes=VMEM_LIMIT),
        name="mem_kv",
    )(mem, w_bf)


def _w_prep_kernel(w_hbm, wa_ref, wf_ref, buf, sem, *, o_ff, n_fox, tn):
    j = pl.program_id(0)
    n = pl.num_programs(0)

    def copy(blk, slot):
        r0 = jnp.where(blk * tn < o_ff, blk * tn, blk * tn + n_fox)
        return pltpu.make_async_copy(w_hbm.at[pl.ds(r0, tn)], buf.at[slot], sem.at[slot])

    @pl.when(j == 0)
    def _():
        gate = pltpu.make_async_copy(w_hbm.at[pl.ds(o_ff, tn)], buf.at[1], sem.at[1])
        gate.start()
        copy(0, 0).start()
        gate.wait()
        rows = buf[1, 0:LANES, 0, :]
        keep = lax.broadcasted_iota(jnp.int32, rows.shape, 0) < n_fox
        wf_ref[...] = jnp.where(keep, rows, 0.0).T.astype(BF16)

    @pl.when(j + 1 < n)
    def _():
        copy(j + 1, (j + 1) % 2).start()

    copy(j, j % 2).wait()
    wa_ref[...] = buf[j % 2, :, 0, :].T.astype(BF16)


def _w_prep(w_fm, *, o_ff, n_fox, tn):
    n, _, d_model = w_fm.shape
    assert o_ff % tn == 0 and (n - n_fox) % tn == 0 and tn >= LANES and o_ff + tn <= n
    return pl.pallas_call(
        functools.partial(_w_prep_kernel, o_ff=o_ff, n_fox=n_fox, tn=tn),
        grid=((n - n_fox) // tn,),
        in_specs=[pl.BlockSpec(memory_space=pl.ANY)],
        out_specs=(pl.BlockSpec((d_model, tn), lambda j: (0, j)),
                   pl.BlockSpec((d_model, LANES), lambda j: (0, 0))),
        out_shape=(jax.ShapeDtypeStruct((d_model, n - n_fox), BF16),
                   jax.ShapeDtypeStruct((d_model, LANES), BF16)),
        scratch_shapes=[pltpu.VMEM((2, tn, 1, d_model), F32), pltpu.SemaphoreType.DMA((2,))],
        compiler_params=_cparams(("arbitrary",)),
        name="w_prep",
    )(w_fm)


def _in_proj_kernel(x_ref, wa_ref, wf_ref, bf_ref, base_ref, rot_ref, mk_ref, mv_ref, tri_ref,
                    fq_ref, fk_ref, fv_ref, lf_ref, c_ref, rq_ref, rk_ref, rv_ref, rg_ref, mo_ref,
                    qa_ref, ka_ref, va_ref, carry_ref, start_ref, *, d_fox, d_ret, d_memq, tiles_per_seq,
                    blk_tiles):
    i = pl.program_id(0)
    xb = x_ref[...].astype(BF16)

    def proj(lo, width):
        return _dot(xb, wa_ref[:, lo:lo + width])

    o_fk = d_fox
    o_fv = 2 * d_fox
    o_rq = 3 * d_fox
    o_rk = o_rq + d_ret
    o_rv = o_rk + d_ret
    o_rg = o_rv + d_ret
    o_mq = o_rg + d_ret

    heads = lambda d: [slice(h * HEAD_DIM, (h + 1) * HEAD_DIM) for h in range(d // HEAD_DIM)]

    z = _dot(xb, wf_ref[...]) + bf_ref[...]
    fq_ref[...] = (proj(0, d_fox) * (SCALE * LOG2E)).astype(BF16)
    lf = jnp.minimum(z, 0.0) - jnp.log1p(jnp.exp(-jnp.abs(z)))
    lf_ref[...] = lf

    @pl.when(i % tiles_per_seq == 0)
    def _():
        carry_ref[...] = jnp.zeros_like(carry_ref)

    parts = _dot(tri_ref[...], jnp.concatenate([t.astype(BF16) for t in _split3_f32(lf)], axis=1))
    c = parts[:, 0:LANES] + parts[:, LANES:2 * LANES] + parts[:, 2 * LANES:] + carry_ref[0:1, :]
    c_ref[...] = c
    tm = c.shape[0]
    carry_ref[...] = jnp.broadcast_to(c[tm - 1:tm, :], carry_ref.shape)

    @pl.when(i % blk_tiles == 0)
    def _():
        start_ref[...] = jnp.broadcast_to(c[0:1, :], start_ref.shape)

    rel3 = _split3_f32((c - start_ref[0:1, :]) * LOG2E)
    zfk = proj(o_fk, d_fox)
    zfv = proj(o_fv, d_fox)
    for h, sl in enumerate(heads(d_fox)):
        fk_ref[0, h] = zfk[:, sl]
        fv_ref[0, h] = zfv[:, sl]
        qa_ref[0, h], ka_ref[0, h], va_ref[0, h] = _fox_operands(fq_ref[:, sl], zfk[:, sl], zfv[:, sl], rel3, h)

    ca = base_ref[0, 0:1, :]
    sa = base_ref[0, 1:2, :]
    cos = ca * rot_ref[0] - sa * rot_ref[1]
    sin = sa * rot_ref[2] + ca * rot_ref[3]

    def rope(t):
        return t * cos + pltpu.roll(t, HEAD_DIM // 2, axis=1) * sin

    def ret_q():
        zq = proj(o_rq, d_ret)
        for sl in heads(d_ret):
            rq_ref[:, sl] = rope(zq[:, sl]).astype(BF16)

    def ret_k():
        zk = proj(o_rk, d_ret)
        for sl in heads(d_ret):
            rk_ref[:, sl] = (rope(zk[:, sl]) * SCALE).astype(BF16)

    def ret_v():
        rv_ref[...] = proj(o_rv, d_ret).astype(BF16)

    def ret_g():
        rg_ref[...] = proj(o_rg, d_ret).astype(BF16)

    zm = proj(o_mq, d_memq)
    pending = [ret_q, ret_k, ret_v, ret_g]
    for h, sl in enumerate(heads(d_memq)):
        s = _dot_nt((zm[:, sl] * SCALE).astype(BF16), mk_ref[0, h].astype(BF16))
        p = jnp.exp(s - jnp.max(s, axis=-1, keepdims=True))
        if pending:
            pending.pop(0)()
        o = _dot(p.astype(BF16), mv_ref[0, h].astype(BF16))
        mo_ref[:, sl] = (o / jnp.sum(p, axis=-1, keepdims=True)).astype(BF16)
    for projection in pending:
        projection()


def _in_proj(x, wa, wf, bfp, base_tab, rot_tab, mk, mv, *, tm, seq_len, d_fox, d_ret, d_memq, attn_block):
    m, d_model = x.shape
    assert m % tm == 0 and seq_len % tm == 0
    assert base_tab.shape == (m // tm, 8, HEAD_DIM) and rot_tab.shape == (4, tm, HEAD_DIM)
    tiles_per_seq = seq_len // tm
    n_fox = d_fox // HEAD_DIM
    tri = jnp.asarray(np.arange(tm)[:, None] >= np.arange(tm)[None, :], dtype=BF16)
    row = lambda w: pl.BlockSpec((tm, w), lambda i: (i, 0))
    memspec = pl.BlockSpec((1,) + mk.shape[1:], lambda i: (i // tiles_per_seq, 0, 0, 0))
    kvspec = pl.BlockSpec((1, n_fox, tm, HEAD_DIM), lambda i: (i // tiles_per_seq, 0, i % tiles_per_seq, 0))
    augspec = pl.BlockSpec((1, n_fox, tm, 2 * HEAD_DIM),
                           lambda i: (i // tiles_per_seq, 0, i % tiles_per_seq, 0))
    aug = jax.ShapeDtypeStruct((m // seq_len, n_fox, seq_len, 2 * HEAD_DIM), BF16)
    assert attn_block % tm == 0 and seq_len % attn_block == 0
    kern = functools.partial(_in_proj_kernel, d_fox=d_fox, d_ret=d_ret, d_memq=d_memq,
                             tiles_per_seq=tiles_per_seq, blk_tiles=attn_block // tm)
    out_shape = (
        jax.ShapeDtypeStruct((m, d_fox), BF16),
        jax.ShapeDtypeStruct((m // seq_len, n_fox, seq_len, HEAD_DIM), F32),
        jax.ShapeDtypeStruct((m // seq_len, n_fox, seq_len, HEAD_DIM), F32),
        jax.ShapeDtypeStruct((m, LANES), F32),
        jax.ShapeDtypeStruct((m, LANES), F32),
        jax.ShapeDtypeStruct((m, d_ret), BF16),
        jax.ShapeDtypeStruct((m, d_ret), BF16),
        jax.ShapeDtypeStruct((m, d_ret), BF16),
        jax.ShapeDtypeStruct((m, d_ret), BF16),
        jax.ShapeDtypeStruct((m, d_memq), BF16),
        aug, aug, aug,
    )
    return pl.pallas_call(
        kern,
        grid=(m // tm,),
        in_specs=[row(d_model), _resident(wa.shape), _resident(wf.shape), _resident(bfp.shape),
                  pl.BlockSpec((1, 8, HEAD_DIM), lambda i: (i, 0, 0)), _resident(rot_tab.shape),
                  memspec, memspec, _resident(tri.shape)],
        out_specs=(row(d_fox), kvspec, kvspec, row(LANES), row(LANES),
                   row(d_ret), row(d_ret), row(d_ret), row(d_ret), row(d_memq),
                   augspec, augspec, augspec),
        out_shape=out_shape,
        scratch_shapes=[pltpu.VMEM((8, LANES), F32), pltpu.VMEM((8, LANES), F32)],
        compiler_params=_cparams(("arbitrary",)),
        name="in_proj",
    )(x, wa, wf, bfp, base_tab, rot_tab, mk, mv, tri)


def _split3_f32(a):
    hi = a.astype(BF16).astype(F32)
    r1 = a - hi
    mid = r1.astype(BF16).astype(F32)
    lo = (r1 - mid).astype(BF16).astype(F32)
    return hi, mid, lo


def _fox_operands(q, k, v, rel3, h):
    lane = lax.broadcasted_iota(jnp.int32, q.shape, 1)
    hi, mid, lo = (t[:, h:h + 1] for t in rel3)
    qbias = jnp.where(lane == 0, hi, jnp.where(lane == 1, mid, jnp.where(lane == 2, lo,
                      jnp.where(lane < 6, 1.0, 0.0))))
    kbias = jnp.where(lane < 3, 1.0, jnp.where(lane == 3, -hi, jnp.where(lane == 4, -mid,
                      jnp.where(lane == 5, -lo, 0.0))))
    ones_col = jnp.where(lane == 0, 1.0, 0.0)
    cat = lambda a, b: jnp.concatenate([a.astype(BF16), b.astype(BF16)], axis=1)
    return cat(q, qbias), cat(k, kbias), cat(v, ones_col)


def _fox_prompt_kernel(qa, k_ref, v_ref, ccol_ref, o_ref, kbuf, vbuf, cstart, acc, sbuf, tbuf, sem, *, tb,
                       nsub, kv_per_trip):
    h = pl.program_id(0)
    qi = pl.program_id(1)
    tq = tb * nsub
    row0 = pl.multiple_of(qi * tq, tq)

    def key_copies():
        return [pltpu.make_async_copy(src.at[0, h, pl.ds(row0, tq)], dst.at[pl.ds(row0, tq)], sem.at[n])
                for n, (src, dst) in enumerate(((k_ref, kbuf), (v_ref, vbuf)))]

    for cp in key_copies():
        cp.start()

    @pl.when(qi == 0)
    def _():
        for cp in key_copies():
            cp.wait()

    lane = lax.broadcasted_iota(jnp.int32, (8, LANES), 1)
    for s in range(nsub):
        start = jnp.sum(jnp.where(lane == h, ccol_ref[s * tb:s * tb + 8, :], 0.0), axis=1, keepdims=True)
        cstart[nsub * qi + s] = jnp.broadcast_to(start[0:1], (8, LANES))
    acc[...] = jnp.zeros_like(acc)

    n_ct = tb // LANES
    subs = range(nsub)

    def lane_tiles(s2):
        return [s2[:, t * LANES:(t + 1) * LANES] for t in range(n_ct)]

    def scores(s, kj):
        col0 = pl.multiple_of(kj * tb, tb)
        s2 = _dot_nt(qa[s * tb:(s + 1) * tb, :], kbuf[pl.ds(col0, tb), :])
        return s2, functools.reduce(jnp.maximum, lane_tiles(s2))

    def softmax_pv(s, kj, sc, m):
        col0 = pl.multiple_of(kj * tb, tb)
        s2, tmax = sc
        tiles = lane_tiles(s2)
        delta = (cstart[nsub * qi + s][0:1, :] - cstart[kj][0:1, :]) * LOG2E
        m_rel = m - delta
        m_new = jnp.maximum(m_rel, jnp.max(tmax, axis=-1, keepdims=True))
        alpha = jnp.exp2(m_rel - m_new)
        p = jnp.concatenate([jnp.exp2(t - m_new).astype(BF16) for t in tiles], axis=1)
        pv = _dot(p, vbuf[pl.ds(col0, tb), :])
        a = acc[s]
        acc[s] = jnp.concatenate([alpha * a[:, 0:LANES], alpha * a[:, LANES:]], axis=1) + pv
        return m_new + delta

    for s in subs:
        sbuf[s], tbuf[s] = scores(s, 0)

    @pl.when(qi > 0)
    def _():
        for cp in key_copies():
            cp.wait()

    def body(t, ms):
        kj = kv_per_trip * t
        cur = [(sbuf[s], tbuf[s]) for s in subs]
        for u in range(kv_per_trip):
            nxt = [scores(s, kj + u + 1) for s in subs]
            ms = [softmax_pv(s, kj + u, cur[s], ms[s]) for s in subs]
            cur = nxt
        for s in subs:
            sbuf[s], tbuf[s] = cur[s]
        return tuple(ms)

    assert nsub % kv_per_trip == 0
    ms = lax.fori_loop(0, (nsub // kv_per_trip) * qi, body,
                       tuple(jnp.full((tb, LANES), NEG_INF, F32) for _ in subs))
    half = tb // 2
    r_half = lax.broadcasted_iota(jnp.int32, (half, LANES), 0)
    c_half = lax.broadcasted_iota(jnp.int32, (half, LANES), 1)

    def diagonal_unit(s, kj, m, s2_full):
        col0 = pl.multiple_of(kj * tb, tb)
        delta = (cstart[nsub * qi + s][0:1, :] - cstart[kj][0:1, :]) * LOG2E
        for ra in range(2):
            r0 = ra * half
            m_r = m[r0:r0 + half] - delta
            a = acc[s, r0:r0 + half, :]
            for ca in range(ra + 1):
                c0 = pl.multiple_of(col0 + ca * half, half)
                if s2_full is None:
                    s2 = _dot_nt(qa[s * tb + r0:s * tb + r0 + half, :], kbuf[pl.ds(c0, half), :])
                else:
                    s2 = s2_full[r0:r0 + half, ca * half:(ca + 1) * half]
                tiles = [s2[:, t * LANES:(t + 1) * LANES] for t in range(half // LANES)]
                if ca == ra:
                    tiles = [jnp.where(c_half + t * LANES <= r_half, tl, NEG_INF) for t, tl in enumerate(tiles)]
                m_new = jnp.maximum(m_r, jnp.max(functools.reduce(jnp.maximum, tiles), axis=-1, keepdims=True))
                alpha = jnp.exp2(m_r - m_new)
                p = jnp.concatenate([jnp.exp2(tl - m_new).astype(BF16) for tl in tiles], axis=1)
                pv = _dot(p, vbuf[pl.ds(c0, half), :])
                a = jnp.concatenate([alpha * a[:, 0:LANES], alpha * a[:, LANES:]], axis=1) + pv
                m_r = m_new
            acc[s, r0:r0 + half, :] = a

    for s in subs:
        m = ms[s]
        for t in range(s):
            sc = (sbuf[s], tbuf[s]) if t == 0 else scores(s, nsub * qi + t)
            m = softmax_pv(s, nsub * qi + t, sc, m)
        diagonal_unit(s, nsub * qi + s, m, sbuf[s] if s == 0 else None)
        a = acc[s]
        o_ref[s * tb:(s + 1) * tb, :] = (a[:, 0:HEAD_DIM] / a[:, HEAD_DIM:HEAD_DIM + 1]).astype(BF16)


def _fox_prompt(qa, ka, va, c_col, *, tb, nsub):
    _, n_heads, s, width = qa.shape
    tq = tb * nsub
    assert s % tq == 0
    op = lambda: pl.BlockSpec((None, None, tq, width), lambda h, qi: (0, h, qi, 0))
    return pl.pallas_call(
        functools.partial(_fox_prompt_kernel, tb=tb, nsub=nsub, kv_per_trip=min(nsub, 4)),
        grid=(n_heads, s // tq),
        in_specs=[op(), pl.BlockSpec(memory_space=pl.ANY), pl.BlockSpec(memory_space=pl.ANY),
                  pl.BlockSpec((tq, LANES), lambda h, qi: (qi, 0))],
        out_specs=pl.BlockSpec((tq, HEAD_DIM), lambda h, qi: (qi, h)),
        out_shape=jax.ShapeDtypeStruct((s, n_heads * HEAD_DIM), BF16),
        scratch_shapes=[pltpu.VMEM((s, width), BF16),
                        pltpu.VMEM((s, width), BF16),
                        pltpu.VMEM((s // tb, 8, LANES), F32),
                        pltpu.VMEM((nsub, tb, width), F32),
                        pltpu.VMEM((nsub, tb, tb), F32),
                        pltpu.VMEM((nsub, tb, LANES), F32),
                        pltpu.SemaphoreType.DMA((2,))],
        compiler_params=_cparams(("arbitrary", "arbitrary")),
        name="fox_prompt",
    )(qa, ka, va, c_col)


def _fox_sample_kernel(q_ref, kc_ref, vc_ref, kn_ref, vn_ref, lfc_ref, lfn_ref, ccol_ref,
                       tric_ref, trin_ref, o_ref, *, n_heads):
    c_cache = _split3_dot(lfc_ref[0], tric_ref[...])
    p_len = c_cache.shape[1]
    ck_cache = c_cache - c_cache[:, p_len - 1:p_len]
    ck_new = _split3_dot(lfn_ref[0], trin_ref[...])
    ccol = ccol_ref[...]
    n_new = ck_new.shape[1]
    r = lax.broadcasted_iota(jnp.int32, (n_new, n_new), 0)
    c = lax.broadcasted_iota(jnp.int32, (n_new, n_new), 1)
    for h in range(n_heads):
        sl = slice(h * HEAD_DIM, (h + 1) * HEAD_DIM)
        q = q_ref[:, sl]
        cq = ccol[:, h:h + 1]
        s_c = _dot_nt(q, kc_ref[0, h].astype(BF16)) + (cq - ck_cache[h:h + 1, :]) * LOG2E
        s_n = _dot_nt(q, kn_ref[0, h].astype(BF16)) + (cq - ck_new[h:h + 1, :]) * LOG2E
        s_n = jnp.where(c <= r, s_n, NEG_INF)
        m = jnp.maximum(jnp.max(s_c, axis=-1, keepdims=True), jnp.max(s_n, axis=-1, keepdims=True))
        p_c = jnp.exp2(s_c - m)
        p_n = jnp.exp2(s_n - m)
        l = jnp.sum(p_c, axis=-1, keepdims=True) + jnp.sum(p_n, axis=-1, keepdims=True)
        o = _dot(p_c.astype(BF16), vc_ref[0, h].astype(BF16)) + \
            _dot(p_n.astype(BF16), vn_ref[0, h].astype(BF16))
        o_ref[:, sl] = (o / l).astype(BF16)


def _fox_sample(fq, kc, vc, kn, vn, lfc_row, lfn_row, c_col, *, n_heads, seq_len):
    m, d_fox = fq.shape
    nb, _, p_len, _ = kc.shape
    tric = jnp.asarray(np.arange(p_len)[:, None] <= np.arange(p_len)[None, :], dtype=BF16)
    trin = jnp.asarray(np.arange(seq_len)[:, None] <= np.arange(seq_len)[None, :], dtype=BF16)
    row = lambda w: pl.BlockSpec((seq_len, w), lambda b: (b, 0))
    cache = pl.BlockSpec((1, n_heads, p_len, HEAD_DIM), lambda b: (b, 0, 0, 0))
    new = pl.BlockSpec((1, n_heads, seq_len, HEAD_DIM), lambda b: (b, 0, 0, 0))
    return pl.pallas_call(
        functools.partial(_fox_sample_kernel, n_heads=n_heads),
        grid=(nb,),
        in_specs=[row(d_fox), cache, cache, new, new,
                  pl.BlockSpec((1, HEAD_ROWS, p_len), lambda b: (b, 0, 0)),
                  pl.BlockSpec((1, HEAD_ROWS, seq_len), lambda b: (b, 0, 0)),
                  row(LANES), _resident(tric.shape), _resident(trin.shape)],
        out_specs=row(d_fox),
        out_shape=jax.ShapeDtypeStruct((m, d_fox), BF16),
        compiler_params=_cparams(("arbitrary",)),
        name="fox_sample",
    )(fq, kc, vc, kn, vn, lfc_row, lfn_row, c_col, tric, trin)


def _log_gamma(h):
    return float(np.log1p(-np.exp2(np.float32(-5.0 - h)), dtype=np.float32))


def _retention_kernel(q_ref, k_ref, v_ref, g_ref, s0_ref, gn_ref, o_ref, s_ref, decay_ref, *, n_heads,
                      t_blk):
    step = pl.program_id(1)

    @pl.when(step == 0)
    def _():
        s_ref[...] = s0_ref[...]

    @pl.when(jnp.logical_and(pl.program_id(0) == 0, step == 0))
    def _():
        ti = lax.broadcasted_iota(jnp.int32, (t_blk, t_blk), 0)
        si = lax.broadcasted_iota(jnp.int32, (t_blk, t_blk), 1)
        dist = jnp.abs(ti - si).astype(F32)
        shift = CHUNK.bit_length() - 1
        visible = jnp.right_shift(si, shift) <= jnp.right_shift(ti, shift)
        for h in range(n_heads):
            decay_ref[h] = jnp.where(visible, jnp.exp(_log_gamma(h) * dist), 0.0)

    tcol = lax.broadcasted_iota(jnp.int32, (t_blk, 1), 0).astype(F32)
    for h in range(n_heads):
        lg = _log_gamma(h)
        sl = slice(h * HEAD_DIM, (h + 1) * HEAD_DIM)
        q = q_ref[:, sl]
        k = k_ref[:, sl]
        v = v_ref[:, sl]
        a = _dot_nt(q, k) * decay_ref[h]
        state = s_ref[0, h]
        o = _dot(a.astype(BF16), v) + _dot(q, state.astype(BF16)) * jnp.exp(lg * (tcol + 1.0))
        kd = (k.astype(F32) * jnp.exp(lg * (t_blk - 1.0 - tcol))).astype(BF16)
        s_ref[0, h] = math.exp(lg * t_blk) * state + _dot_tn(kd, v)
        mu = jnp.mean(o, axis=-1, keepdims=True)
        oc = o - mu
        var = jnp.mean(oc * oc, axis=-1, keepdims=True)
        rn = oc * lax.rsqrt(var + GN_EPS) * gn_ref[:, sl]
        gate = g_ref[:, sl].astype(F32)
        o_ref[:, sl] = (gate / (1.0 + jnp.exp(-gate)) * rn).astype(BF16)


def _retention(rq, rk, rv, rg, s0, gn_g, *, t_blk, seq_len, n_heads):
    m, d_ret = rq.shape
    nb = m // seq_len
    steps = seq_len // t_blk
    row = lambda: pl.BlockSpec((t_blk, d_ret), lambda b, s: (b * steps + s, 0))
    st = lambda: pl.BlockSpec((1, n_heads, HEAD_DIM, HEAD_DIM), lambda b, s: (b, 0, 0, 0))
    return pl.pallas_call(
        functools.partial(_retention_kernel, n_heads=n_heads, t_blk=t_blk),
        grid=(nb, steps),
        in_specs=[row(), row(), row(), row(), st(), pl.BlockSpec((1, d_ret), lambda b, s: (0, 0))],
        out_specs=(row(), st()),
        out_shape=(jax.ShapeDtypeStruct((m, d_ret), BF16),
                   jax.ShapeDtypeStruct((nb, n_heads, HEAD_DIM, HEAD_DIM), F32)),
        scratch_shapes=[pltpu.VMEM((n_heads, t_blk, t_blk), F32)],
        compiler_params=_cparams(("arbitrary", "arbitrary")),
        name="retention",
    )(rq, rk, rv, rg, s0, gn_g)


def _mix_out_kernel(x_ref, fo_ref, ro_ref, mo_ref, wo_ref, g_ref, b_ref, o_ref, *rest, alpha, emit_weight):
    wo_out, cat_ref = rest if emit_weight else (None,) + rest
    d_fox = fo_ref.shape[1]
    d_ret = ro_ref.shape[1]
    cat_ref[:, 0:d_fox] = fo_ref[...]
    cat_ref[:, d_fox:d_fox + d_ret] = ro_ref[...]
    cat_ref[:, d_fox + d_ret:] = mo_ref[...]
    wo = wo_ref[...]
    if emit_weight:
        wo = wo.astype(BF16)
        wo_out[...] = wo
    tm = x_ref.shape[0]
    pieces = [slice(0, tm // 2), slice(tm // 2, tm)] if tm % 32 == 0 else [slice(0, tm)]
    mix = [_dot(cat_ref[r, :], wo) for r in pieces]
    for r, mx in zip(pieces, mix):
        o_ref[r, :] = _layernorm(alpha * x_ref[r, :] + mx, g_ref[...], b_ref[...])


def _mix_out(x, fo, ro, mo, wo, g, b, *, tm, alpha):
    m, d_model = x.shape
    emit = wo.dtype != BF16
    assert not emit or m == tm
    row = lambda w: pl.BlockSpec((tm, w), lambda i: (i, 0))
    out_specs, out_shape = row(d_model), jax.ShapeDtypeStruct((m, d_model), F32)
    if emit:
        out_specs = (out_specs, pl.BlockSpec(wo.shape, lambda i: (0, 0)))
        out_shape = (out_shape, jax.ShapeDtypeStruct(wo.shape, BF16))
    return pl.pallas_call(
        functools.partial(_mix_out_kernel, alpha=alpha, emit_weight=emit),
        grid=(m // tm,),
        in_specs=[row(d_model), row(fo.shape[1]), row(ro.shape[1]), row(mo.shape[1]),
                  _resident(wo.shape), _resident(g.shape), _resident(b.shape)],
        out_specs=out_specs,
        out_shape=out_shape,
        scratch_shapes=[pltpu.VMEM((tm, wo.shape[0]), BF16)],
        compiler_params=_cparams(("arbitrary",)),
        name="mix_out",
    )(x, fo, ro, mo, wo, g, b)


def _gelu_tanh(x):
    c = math.sqrt(2.0 / math.pi)
    return x * (0.5 * (1.0 + jnp.tanh(c * (x + 0.044715 * (x * x * x)))))


def _conv_ffn_kernel(h_ref, wg_ref, wu_ref, wd_ref, cw_ref, cb_ref, past_ref, g_ref, b_ref,
                     y_ref, cv_ref, *rest, alpha, seq_len, tm, tc, emit_weights):
    wg_out, wu_out, wd_out = rest[:3] if emit_weights else (None, None, None)
    hb_ref, gbuf_ref, carry_ref = rest[3:] if emit_weights else rest

    def weight(ref, out, idx):
        w = ref[idx]
        if out is not None:
            w = w.astype(BF16)
            out[idx] = w
        return w

    i = pl.program_id(0)
    f = pl.program_id(1)
    nf = pl.num_programs(1)
    lt = min(seq_len, tm)
    ns = tm // lt
    halo = 8
    carried = seq_len > tm

    @pl.when(f == 0)
    def _():
        h = h_ref[...]
        hb_ref[...] = h.astype(BF16)
        y_ref[...] = alpha * h

    if carried:
        @pl.when(jnp.logical_and(i == 0, f == 0))
        def _():
            carry_ref[...] = jnp.zeros_like(carry_ref)

        tiles_per_seq = seq_len // tm
        gbuf_ref[0, halo - 2:halo, :] = jnp.where(i % tiles_per_seq == 0, past_ref[i // tiles_per_seq],
                                                  carry_ref[f])
    else:
        for s in range(ns):
            gbuf_ref[s, halo - 2:halo, :] = past_ref[i * ns + s]

    hb = hb_ref[...]
    hid = []
    for c0 in range(0, wg_ref.shape[1], tc):
        cs = slice(c0, c0 + tc)
        gate = _dot(hb, weight(wg_ref, wg_out, (slice(None), cs)))
        up = _dot(hb, weight(wu_ref, wu_out, (slice(None), cs)))
        w0 = cw_ref[0:1, cs]
        w1 = cw_ref[1:2, cs]
        w2 = cw_ref[2:3, cs]
        cb = cb_ref[:, cs]
        parts = []
        for s in range(ns):
            gs = gate[s * lt:(s + 1) * lt]
            gbuf_ref[s, halo:halo + lt, cs] = gs
            g1 = gbuf_ref[s, halo - 1:halo - 1 + lt, cs]
            g2 = gbuf_ref[s, halo - 2:halo - 2 + lt, cs]
            gc = cb + w0 * g2 + w1 * g1 + w2 * gs
            parts.append((_gelu_tanh(gc) * up[s * lt:(s + 1) * lt]).astype(BF16))
        hid.append(parts[0] if ns == 1 else jnp.concatenate(parts, axis=0))
    hid = hid[0] if len(hid) == 1 else jnp.concatenate(hid, axis=1)
    y_ref[...] += _dot(hid, weight(wd_ref, wd_out, (slice(None), slice(None))))

    for s in range(ns):
        last2 = gbuf_ref[s, halo + lt - 2:halo + lt, :]
        cv_ref[s] = last2
        if carried:
            carry_ref[f] = last2

    @pl.when(f == nf - 1)
    def _():
        y_ref[...] = _layernorm(y_ref[...], g_ref[...], b_ref[...])


def _conv_ffn(h1, wg, wu, wd, conv_w, conv_b, past, g, b, *, tm, tf, seq_len, alpha):
    tc = 2 * LANES
    assert tf % tc == 0
    m, d_model = h1.shape
    d_ff = wg.shape[1]
    n_seq = past.shape[0]
    assert m % tm == 0 and d_ff % tf == 0 and (seq_len % tm == 0 or tm % seq_len == 0)
    nf = d_ff // tf
    lt = min(seq_len, tm)
    emit = wg.dtype != BF16
    assert not emit or m == tm
    w_specs = [pl.BlockSpec((d_model, tf), lambda i, f: (0, f)),
               pl.BlockSpec((d_model, tf), lambda i, f: (0, f)),
               pl.BlockSpec((tf, d_model), lambda i, f: (f, 0))]
    w_shapes = [jax.ShapeDtypeStruct(t.shape, BF16) for t in (wg, wu, wd)] if emit else []
    y, cv_all, *w_bf = pl.pallas_call(
        functools.partial(_conv_ffn_kernel, alpha=alpha, seq_len=seq_len, tm=tm, tc=tc, emit_weights=emit),
        grid=(m // tm, nf),
        in_specs=[pl.BlockSpec((tm, d_model), lambda i, f: (i, 0)),
                  *w_specs,
                  pl.BlockSpec((CONV_W, tf), lambda i, f: (0, f)),
                  pl.BlockSpec((1, tf), lambda i, f: (0, f)),
                  pl.BlockSpec((n_seq, CONV_W - 1, tf), lambda i, f: (0, 0, f)),
                  pl.BlockSpec((1, d_model), lambda i, f: (0, 0)),
                  pl.BlockSpec((1, d_model), lambda i, f: (0, 0))],
        out_specs=(pl.BlockSpec((tm, d_model), lambda i, f: (i, 0)),
                   pl.BlockSpec((tm // lt, CONV_W - 1, tf), lambda i, f: (i, 0, f)),
                   *(w_specs if emit else [])),
        out_shape=(jax.ShapeDtypeStruct((m, d_model), F32),
                   jax.ShapeDtypeStruct((m // lt, CONV_W - 1, d_ff), F32), *w_shapes),
        scratch_shapes=[pltpu.VMEM((tm, d_model), BF16),
                        pltpu.VMEM((tm // lt, 8 + lt, tf), F32),
                        pltpu.VMEM((nf, CONV_W - 1, tf), F32)],
        compiler_params=_cparams(("arbitrary", "arbitrary")),
        name="conv_ffn",
    )(h1, wg, wu, wd, conv_w, conv_b, past, g, b)
    return y, cv_all.reshape(n_seq, seq_len // lt, CONV_W - 1, d_ff)[:, -1], w_bf


def _rope_tables(tile_base, tm):
    half = HEAD_DIM // 2
    inv = ROPE_BASE ** (-jnp.arange(half, dtype=F32) / half)
    inv = jnp.concatenate([inv, inv])
    sign = jnp.concatenate([-jnp.ones((half,), F32), jnp.ones((half,), F32)])
    a = tile_base.astype(F32)[:, None] * inv[None, :]
    b = jnp.arange(tm, dtype=F32)[:, None] * inv[None, :]
    base_tab = jnp.stack([jnp.cos(a), jnp.sin(a)], axis=1)
    base_tab = jnp.pad(base_tab, ((0, 0), (0, 6), (0, 0)))
    rot_tab = jnp.stack([jnp.cos(b), jnp.sin(b), sign * jnp.cos(b), sign * jnp.sin(b)])
    return base_tab, rot_tab


def _layer(x, seq_start, seq_len, mk, mv, w, *, alpha, tm_proj, tm_mix, tm_ffn, tf, t_ret, s0, conv_past,
           fox_fn, attn_block):
    tile_base = (seq_start[:, None] + jnp.arange(0, seq_len, tm_proj)[None, :]).reshape(-1)
    base_tab, rot_tab = _rope_tables(tile_base, tm_proj)
    fq, fk, fv, lf, c, rq, rk, rv, rg, mo, qa, ka, va = _in_proj(
        x, w["wa"], w["wf"], w["bf"], base_tab, rot_tab, mk, mv, tm=tm_proj, seq_len=seq_len,
        d_fox=w["d_fox"], d_ret=w["d_ret"], d_memq=w["d_memq"], attn_block=attn_block)
    fo = fox_fn(fq, fk, fv, lf, c, qa, ka, va)
    ro, s_new = _retention(rq, rk, rv, rg, s0, w["gn_g"], t_blk=t_ret, seq_len=seq_len,
                           n_heads=w["d_ret"] // HEAD_DIM)
    w_bf = {}
    h1 = _mix_out(x, fo, ro, mo, w["wo"], w["ln1_g"], w["ln1_b"], tm=tm_mix, alpha=alpha)
    if w["wo"].dtype != BF16:
        h1, w_bf["wo"] = h1
    y, cv, ffn_bf = _conv_ffn(h1, w["wg"], w["wu"], w["wd"], w["conv_w"], w["conv_b"], conv_past,
                              w["ln2_g"], w["ln2_b"], tm=tm_ffn, tf=tf, seq_len=seq_len, alpha=alpha)
    w_bf.update(zip(("wg", "wu", "wd"), ffn_bf))
    return y, fk, fv, lf, s_new, cv, w_bf


def kernel(x_prompt, x_sample, cache_fox_k, cache_fox_v, cache_fox_logf, state_ret, cache_mem_k,
           cache_mem_v, state_conv, mem_prompt, w_in, b_f, w_mem_kv, ret_gn_g, w_o, ln1_g, ln1_b,
           w_gate, w_up, conv_w, conv_b, w_down, ln2_g, ln2_b):
    batch, seq, d_model = x_prompt.shape
    dec_batch, dec_seq, _ = x_sample.shape
    depth = w_in.shape[0]
    past_len = cache_fox_k.shape[2]
    n_fox = cache_fox_k.shape[3]
    n_ret = state_ret.shape[2]
    n_mem = cache_mem_k.shape[3]
    n_mem_tok = cache_mem_k.shape[2]
    d_fox, d_ret, d_memq = n_fox * HEAD_DIM, n_ret * HEAD_DIM, n_mem * HEAD_DIM
    d_ff = w_gate.shape[2]
    assert batch == 1 and depth == 1 and n_fox <= 8
    alpha = (2.0 * depth) ** 0.25
    l = 0

    o_ff = 3 * d_fox
    w_fm = jnp.transpose(w_in, (2, 0, 1))
    wa, wf = _w_prep(w_fm, o_ff=o_ff, n_fox=n_fox, tn=2 * LANES)
    bfp = jnp.pad(b_f[l], (0, LANES - n_fox)).reshape(1, LANES)
    w = dict(wa=wa, wf=wf, bf=bfp, d_fox=d_fox, d_ret=d_ret, d_memq=d_memq,
             gn_g=ret_gn_g[l].reshape(1, d_ret), wo=w_o[l],
             ln1_g=ln1_g[l].reshape(1, d_model), ln1_b=ln1_b[l].reshape(1, d_model),
             wg=w_gate[l], wu=w_up[l], wd=w_down[l],
             conv_w=conv_w[l], conv_b=conv_b[l].reshape(1, d_ff),
             ln2_g=ln2_g[l].reshape(1, d_model), ln2_b=ln2_b[l].reshape(1, d_model))

    to_head_major = lambda t: jnp.swapaxes(t, -3, -2)

    m_s = dec_batch * dec_seq
    kc = to_head_major(cache_fox_k[l])
    vc = to_head_major(cache_fox_v[l])
    lfc_row = jnp.pad(jnp.swapaxes(cache_fox_logf[l], 1, 2), ((0, 0), (0, HEAD_ROWS - n_fox), (0, 0)))

    def fox_sample_fn(fq, fk, fv, lf, c, qa, ka, va):
        lfn_row = jnp.swapaxes(lf.reshape(dec_batch, dec_seq, LANES)[:, :, :HEAD_ROWS], 1, 2)
        return _fox_sample(fq, kc, vc, fk, fv, lfc_row, lfn_row, c, n_heads=n_fox, seq_len=dec_seq)

    y_s, fk_s, fv_s, lf_s, s_s, cv_s, w_bf = _layer(
        x_sample.reshape(m_s, d_model), jnp.full((dec_batch,), past_len, jnp.int32), dec_seq,
        to_head_major(cache_mem_k[l]), to_head_major(cache_mem_v[l]), w,
        alpha=alpha, tm_proj=dec_seq, tm_mix=m_s, tm_ffn=m_s, tf=2 * LANES, t_ret=dec_seq,
        s0=state_ret[l], conv_past=state_conv[l], fox_fn=fox_sample_fn, attn_block=dec_seq)

    mkv = _mem_kv(mem_prompt.reshape(n_mem_tok, d_model), w_mem_kv[l].astype(BF16))
    mk_p, mv_p = mkv[None, :n_mem], mkv[None, n_mem:]

    fox_tb = 1024

    def fox_prompt_fn(fq, fk, fv, lf, c, qa, ka, va):
        return _fox_prompt(qa, ka, va, c, tb=fox_tb, nsub=2)

    y_p, fk_p, fv_p, lf_p, s_p, cv_p, _ = _layer(
        x_prompt.reshape(seq, d_model), jnp.zeros((batch,), jnp.int32), seq, mk_p, mv_p, {**w, **w_bf},
        alpha=alpha, tm_proj=256, tm_mix=512, tm_ffn=1024, tf=512, t_ret=256,
        s0=jnp.zeros((1, n_ret, HEAD_DIM, HEAD_DIM), F32),
        conv_past=jnp.zeros((1, CONV_W - 1, d_ff), F32), fox_fn=fox_prompt_fn, attn_block=fox_tb)

    return (y_p.reshape(batch, seq, d_model), y_s.reshape(dec_batch, dec_seq, d_model),
            to_head_major(fk_p)[None], to_head_major(fv_p)[None],
            lf_p[:, :n_fox].reshape(1, batch, seq, n_fox),
            s_p.reshape(1, batch, n_ret, HEAD_DIM, HEAD_DIM),
            to_head_major(mk_p)[None], to_head_major(mv_p)[None],
            cv_p.reshape(1, batch, CONV_W - 1, d_ff),
            to_head_major(fk_s)[None], to_head_major(fv_s)[None],
            lf_s[:, :n_fox].reshape(1, dec_batch, dec_seq, n_fox),
            s_s.reshape(1, dec_batch, n_ret, HEAD_DIM, HEAD_DIM),
            cv_s.reshape(1, dec_batch, CONV_W - 1, d_ff))
```
